```python
import math
import jax, jax.numpy as jnp
from jax import lax
import numpy as np

D_MODEL = 1024
BATCH = 16
SEQ = 4096
DEPTH = 1

HEAD_DIM = 64
FOX_HEADS = 8
FOX_WIDTH = FOX_HEADS * HEAD_DIM
DIFF_HEADS = 4
DIFF_QK_WIDTH = DIFF_HEADS * 2 * HEAD_DIM
DIFF_V_DIM = 2 * HEAD_DIM
DIFF_V_WIDTH = DIFF_HEADS * DIFF_V_DIM
IN_WIDTH = 3 * FOX_WIDTH + FOX_HEADS + 2 * DIFF_QK_WIDTH + DIFF_V_WIDTH + 2 * D_MODEL
MEM_LEN = 256
CROSS_HEADS = 4
CROSS_WIDTH = CROSS_HEADS * HEAD_DIM
N_GROUPS = 4
EXPERTS_PER_GROUP = 8
N_EXPERTS = N_GROUPS * EXPERTS_PER_GROUP
TOP_K = 2
D_FF_EXPERT = 512
MOE_BLOCK = 256
Q_BLOCK = 128
ROPE_THETA = 10000.0
NORM_EPS = 1e-6
SUBLN_EPS = 1e-5
FGATE_BIAS_INIT = 3.0

kernel_name = 'hybrid_fox_diffattn_hmoe_block'


def _rms_norm(t, g, eps=NORM_EPS):
    tf = t.astype(jnp.float32)
    tf = tf * lax.rsqrt(jnp.mean(tf * tf, axis=-1, keepdims=True) + eps)
    return (tf * g.astype(jnp.float32)).astype(t.dtype)


def _split_heads(t, n):
    b, s, w = t.shape
    return t.reshape(b, s, n, w // n).transpose(0, 2, 1, 3)


def _merge_heads(t):
    b, h, s, d = t.shape
    return t.transpose(0, 2, 1, 3).reshape(b, s, h * d)


def _in_offsets():
    sizes = (FOX_WIDTH, FOX_WIDTH, FOX_WIDTH, FOX_HEADS, DIFF_QK_WIDTH, DIFF_QK_WIDTH,
             DIFF_V_WIDTH, D_MODEL, D_MODEL)
    return [int(o) for o in np.cumsum(sizes)[:-1]]


def _rope_tables(positions, dtype):
    half = HEAD_DIM // 2
    inv_freq = ROPE_THETA ** (-jnp.arange(half, dtype=jnp.float32) * 2.0 / HEAD_DIM)
    ang = positions.astype(jnp.float32)[..., None] * inv_freq
    cos = jnp.cos(ang)[:, :, None, None, :].astype(dtype)
    sin = jnp.sin(ang)[:, :, None, None, :].astype(dtype)
    return cos, sin


def _rope(t, cos, sin):
    half = t.shape[-1] // 2
    t1, t2 = t[..., :half], t[..., half:]
    return jnp.concatenate([t1 * cos - t2 * sin, t2 * cos + t1 * sin], axis=-1)


def _causal_mask(q0, q1):
    qpos = q0 + jnp.arange(q1 - q0)
    kpos = jnp.arange(q1)
    return qpos[:, None] >= kpos[None, :]


def _fox_attention(q, k, v, log_f):
    c = jnp.cumsum(log_f, axis=-1)
    scale = HEAD_DIM ** -0.5
    n_q = q.shape[2]
    outs = []
    for blk in range(n_q // Q_BLOCK):
        q0, q1 = blk * Q_BLOCK, (blk + 1) * Q_BLOCK
        s = jnp.einsum('bhqd,bhkd->bhqk', q[:, :, q0:q1], k[:, :, :q1]).astype(jnp.float32) * scale
        s = s + c[:, :, q0:q1, None] - c[:, :, None, :q1]
        s = jnp.where(_causal_mask(q0, q1), s, -jnp.inf)
        p = jax.nn.softmax(s, axis=-1).astype(v.dtype)
        outs.append(jnp.einsum('bhqk,bhkd->bhqd', p, v[:, :, :q1]))
    return jnp.concatenate(outs, axis=2)


def _diff_attention(q1, q2, k1, k2, v, lam):
    scale = HEAD_DIM ** -0.5
    n_q = q1.shape[2]
    outs = []
    for blk in range(n_q // Q_BLOCK):
        q0, qe = blk * Q_BLOCK, (blk + 1) * Q_BLOCK
        mask = _causal_mask(q0, qe)
        s1 = jnp.einsum('bhqd,bhkd->bhqk', q1[:, :, q0:qe], k1[:, :, :qe]).astype(jnp.float32) * scale
        s2 = jnp.einsum('bhqd,bhkd->bhqk', q2[:, :, q0:qe], k2[:, :, :qe]).astype(jnp.float32) * scale
        p1 = jax.nn.softmax(jnp.where(mask, s1, -jnp.inf), axis=-1)
        p2 = jax.nn.softmax(jnp.where(mask, s2, -jnp.inf), axis=-1)
        a = (p1 - lam * p2).astype(v.dtype)
        outs.append(jnp.einsum('bhqk,bhkd->bhqd', a, v[:, :, :qe]))
    return jnp.concatenate(outs, axis=2)


def _cross_attention(hx, hm, w_q, w_kv, w_o):
    b, m = hm.shape[0], hm.shape[1]
    q = _split_heads(hx @ w_q, CROSS_HEADS)
    kv = (hm @ w_kv).reshape(b, m, 2, CROSS_HEADS, HEAD_DIM)
    k = kv[:, :, 0].transpose(0, 2, 1, 3)
    v = kv[:, :, 1].transpose(0, 2, 1, 3)
    s = jnp.einsum('bhsd,bhmd->bhsm', q, k).astype(jnp.float32) * HEAD_DIM ** -0.5
    p = jax.nn.softmax(s, axis=-1).astype(v.dtype)
    return _merge_heads(jnp.einsum('bhsm,bhmd->bhsd', p, v)) @ w_o


def _routed_experts(t, expert_ids, gate_w, w_gate, w_up, w_down):
    n_tok, d = t.shape
    n_assign = n_tok * TOP_K
    flat_e = expert_ids.reshape(n_assign).astype(jnp.int32)
    flat_tok = jnp.arange(n_assign, dtype=jnp.int32) // TOP_K
    flat_w = gate_w.reshape(n_assign)
    order = jnp.argsort(flat_e)
    e_sorted = flat_e[order]
    counts = jnp.zeros((N_EXPERTS,), jnp.int32).at[flat_e].add(1)
    padded = (counts + MOE_BLOCK - 1) // MOE_BLOCK * MOE_BLOCK
    pad_end = jnp.cumsum(padded)
    pad_start = pad_end - padded
    seg_start = jnp.cumsum(counts) - counts
    dest = pad_start[e_sorted] + jnp.arange(n_assign, dtype=jnp.int32) - seg_start[e_sorted]
    n_blocks = -(-n_assign // MOE_BLOCK) + N_EXPERTS
    n_rows = n_blocks * MOE_BLOCK
    row_tok = jnp.full((n_rows,), n_tok, jnp.int32).at[dest].set(flat_tok[order])
    row_w = jnp.zeros((n_rows,), t.dtype).at[dest].set(flat_w[order])
    block_start = jnp.arange(n_blocks, dtype=jnp.int32) * MOE_BLOCK
    block_expert = jnp.minimum(jnp.searchsorted(pad_end, block_start, side='right'), N_EXPERTS - 1)
    t_pad = jnp.concatenate([t, jnp.zeros((1, d), t.dtype)], axis=0)
    xs = t_pad[row_tok].reshape(n_blocks, MOE_BLOCK, d)

    def expert_block(args):
        xb, e = args
        hb = jax.nn.silu(xb @ w_gate[e]) * (xb @ w_up[e])
        return hb @ w_down[e]

    ys = lax.map(expert_block, (xs, block_expert)).reshape(n_rows, d)
    out = jnp.zeros((n_tok + 1, d), t.dtype).at[row_tok].add(ys * row_w[:, None])
    return out[:n_tok]


def _hier_moe(t, w_group, b_group, w_expert, b_expert, w_gate, w_up, w_down):
    n_tok = t.shape[0]
    group_logits = (t @ w_group + b_group).astype(jnp.float32)
    group_prob = jax.nn.softmax(group_logits, axis=-1)
    g_sel = jnp.argmax(group_logits, axis=-1)
    g_w = jnp.take_along_axis(group_prob, g_sel[:, None], axis=-1)
    exp_logits = (t @ w_expert + b_expert).astype(jnp.float32).reshape(n_tok, N_GROUPS, EXPERTS_PER_GROUP)
    sel_logits = jnp.take_along_axis(exp_logits, g_sel[:, None, None], axis=1)[:, 0]
    top_vals, top_idx = lax.top_k(sel_logits, TOP_K)
    weights = jax.nn.softmax(top_vals, axis=-1) * g_w
    expert_ids = g_sel[:, None].astype(jnp.int32) * EXPERTS_PER_GROUP + top_idx.astype(jnp.int32)
    return _routed_experts(t, expert_ids, weights.astype(t.dtype), w_gate, w_up, w_down)


def setup_inputs(seed: int = 0) -> dict:
    key = jax.random.key(seed)
    ks = jax.random.split(key, 32)
    f32 = jnp.float32
    L = DEPTH

    def nrm(k, shape, fan_in):
        return jax.random.normal(k, shape, f32) * fan_in ** -0.5

    def gain(k, shape):
        return 1.0 + 0.02 * jax.random.normal(k, shape, f32)

    return {
        'x': jax.random.normal(ks[0], (BATCH, SEQ, D_MODEL), f32),
        'mem': jax.random.normal(ks[1], (BATCH, MEM_LEN, D_MODEL), f32),
        'positions': jnp.broadcast_to(jnp.arange(SEQ, dtype=jnp.int32), (BATCH, SEQ)),
        'g_mix': gain(ks[2], (L, D_MODEL)),
        'w_in': nrm(ks[3], (L, D_MODEL, IN_WIDTH), D_MODEL),
        'b_fgate': FGATE_BIAS_INIT + 0.1 * jax.random.normal(ks[4], (L, FOX_HEADS), f32),
        'w_branch_a': nrm(ks[5], (L, FOX_WIDTH, D_MODEL), FOX_WIDTH),
        'w_branch_b': nrm(ks[6], (L, DIFF_V_WIDTH, D_MODEL), DIFF_V_WIDTH),
        'w_out': nrm(ks[7], (L, D_MODEL, D_MODEL), D_MODEL),
        'lambda_q1': 0.1 * jax.random.normal(ks[8], (L, HEAD_DIM), f32),
        'lambda_k1': 0.1 * jax.random.normal(ks[9], (L, HEAD_DIM), f32),
        'lambda_q2': 0.1 * jax.random.normal(ks[10], (L, HEAD_DIM), f32),
        'lambda_k2': 0.1 * jax.random.normal(ks[11], (L, HEAD_DIM), f32),
        'g_diff_sub': gain(ks[12], (L, DIFF_V_DIM)),
        'g_cross': gain(ks[13], (L, D_MODEL)),
        'g_mem': gain(ks[14], (L, D_MODEL)),
        'w_cq': nrm(ks[15], (L, D_MODEL, CROSS_WIDTH), D_MODEL),
        'w_ckv': nrm(ks[16], (L, D_MODEL, 2 * CROSS_WIDTH), D_MODEL),
        'w_co': nrm(ks[17], (L, CROSS_WIDTH, D_MODEL), CROSS_WIDTH),
        'g_ffn': gain(ks[18], (L, D_MODEL)),
        'w_group': nrm(ks[19], (L, D_MODEL, N_GROUPS), D_MODEL),
        'b_group': 0.01 * jax.random.normal(ks[20], (L, N_GROUPS), f32),
        'w_expert': nrm(ks[21], (L, D_MODEL, N_EXPERTS), D_MODEL),
        'b_expert': 0.01 * jax.random.normal(ks[22], (L, N_EXPERTS), f32),
        'w_exp_gate': nrm(ks[23], (L, N_EXPERTS, D_MODEL, D_FF_EXPERT), D_MODEL),
        'w_exp_up': nrm(ks[24], (L, N_EXPERTS, D_MODEL, D_FF_EXPERT), D_MODEL),
        'w_exp_down': nrm(ks[25], (L, N_EXPERTS, D_FF_EXPERT, D_MODEL), D_FF_EXPERT),
        'g_final': gain(ks[26], (D_MODEL,)),
    }


def reference(x, mem, positions, g_mix, w_in, b_fgate, w_branch_a, w_branch_b, w_out,
              lambda_q1, lambda_k1, lambda_q2, lambda_k2, g_diff_sub, g_cross, g_mem,
              w_cq, w_ckv, w_co, g_ffn, w_group, b_group, w_expert, b_expert,
              w_exp_gate, w_exp_up, w_exp_down, g_final):
    b, s, d = x.shape
    cos, sin = _rope_tables(positions, x.dtype)
    offsets = _in_offsets()
    for l in range(DEPTH):
        lambda_init = 0.8 - 0.6 * math.exp(-0.3 * l)
        h = _rms_norm(x, g_mix[l])
        fq, fk, fv, f_logit, dq, dk, dv, gate_a, gate_b = jnp.split(h @ w_in[l], offsets, axis=-1)
        log_f = jax.nn.log_sigmoid((f_logit + b_fgate[l]).astype(jnp.float32)).transpose(0, 2, 1)
        y_a = _fox_attention(_split_heads(fq, FOX_HEADS), _split_heads(fk, FOX_HEADS),
                             _split_heads(fv, FOX_HEADS), log_f)
        y_a = _merge_heads(y_a)
        dq = _rope(dq.reshape(b, s, DIFF_HEADS, 2, HEAD_DIM), cos, sin)
        dk = _rope(dk.reshape(b, s, DIFF_HEADS, 2, HEAD_DIM), cos, sin)
        lam = (jnp.exp(jnp.sum(lambda_q1[l].astype(jnp.float32) * lambda_k1[l].astype(jnp.float32)))
               - jnp.exp(jnp.sum(lambda_q2[l].astype(jnp.float32) * lambda_k2[l].astype(jnp.float32)))
               + lambda_init)
        y_b = _diff_attention(dq[:, :, :, 0].transpose(0, 2, 1, 3), dq[:, :, :, 1].transpose(0, 2, 1, 3),
                              dk[:, :, :, 0].transpose(0, 2, 1, 3), dk[:, :, :, 1].transpose(0, 2, 1, 3),
                              _split_heads(dv, DIFF_HEADS), lam)
        y_b = _merge_heads(_rms_norm(y_b, g_diff_sub[l], SUBLN_EPS) * (1.0 - lambda_init))
        merged = (jax.nn.sigmoid(gate_a) * (y_a @ w_branch_a[l])
                  + jax.nn.sigmoid(gate_b) * (y_b @ w_branch_b[l]))
        x = x + merged @ w_out[l]
        x = x + _cross_attention(_rms_norm(x, g_cross[l]), _rms_norm(mem, g_mem[l]),
                                 w_cq[l], w_ckv[l], w_co[l])
        hm = _rms_norm(x, g_ffn[l]).reshape(b * s, d)
        x = x + _hier_moe(hm, w_group[l], b_group[l], w_expert[l], b_expert[l],
                          w_exp_gate[l], w_exp_up[l], w_exp_down[l]).reshape(b, s, d)
    return _rms_norm(x, g_final)
```

```python
import functools
import math

import jax
import jax.numpy as jnp
from jax import lax
from jax.experimental import pallas as pl
from jax.experimental.pallas import tpu as pltpu

HEAD_DIM = 64
LANES = 128
FOX_HEADS = 8
DIFF_HEADS = 4
CROSS_HEADS = 4
N_GROUPS = 4
EXPERTS_PER_GROUP = 8
N_EXPERTS = N_GROUPS * EXPERTS_PER_GROUP
TOP_K = 2
MOE_BLOCK = 256
NORM_EPS = 1e-6
SUBLN_EPS = 1e-5
QK_SCALE = HEAD_DIM ** -0.5
NEG_BIG = -1e30
VMEM_LIMIT = 56 * 2**20

BF16 = jnp.bfloat16
F32 = jnp.float32


def _rms(t, g, eps):
    return t * lax.rsqrt(jnp.mean(t * t, axis=-1, keepdims=True) + eps) * g


def _dot(a, b):
    return jnp.dot(a, b, preferred_element_type=F32)


def _dot_nt(a, b):
    return lax.dot_general(a, b, (((1,), (1,)), ((), ())), preferred_element_type=F32)


def _params(*sem):
    return pltpu.CompilerParams(dimension_semantics=sem, vmem_limit_bytes=VMEM_LIMIT)


def _mem_kv_kernel(mem_ref, g_ref, w_ref, k_ref, v_ref):
    h = _rms(mem_ref[...], g_ref[...], NORM_EPS).astype(BF16)
    kv = _dot(h, w_ref[...])
    width = k_ref.shape[1]
    k_ref[...] = kv[:, :width].astype(BF16)
    v_ref[...] = kv[:, width:].astype(BF16)


def _mem_kv(mem2, g_mem, w_ckv, n_batch, mem_len):
    d = mem2.shape[1]
    cw = w_ckv.shape[1] // 2
    out = jax.ShapeDtypeStruct((n_batch * mem_len, cw), BF16)
    return pl.pallas_call(
        _mem_kv_kernel,
        grid=(n_batch,),
        in_specs=[pl.BlockSpec((mem_len, d), lambda b: (b, 0)),
                  pl.BlockSpec((1, d), lambda b: (0, 0)),
                  pl.BlockSpec((d, 2 * cw), lambda b: (0, 0))],
        out_specs=[pl.BlockSpec((mem_len, cw), lambda b: (b, 0))] * 2,
        out_shape=[out, out],
        compiler_params=_params("arbitrary"),
        name="mem_kv",
    )(mem2, g_mem, w_ckv)


def _rope(t, cos, sin_signed, first_half):
    fwd = pltpu.roll(t, LANES - HEAD_DIM // 2, axis=1)
    bwd = pltpu.roll(t, HEAD_DIM // 2, axis=1)
    return t * cos + jnp.where(first_half, fwd, bwd) * sin_signed


def _in_proj_kernel(x_ref, g_ref, w_ref, wf_ref, cos_ref, sin_ref,
                    fq_ref, fk_ref, fv_ref, dq_ref, dk_ref, dv_ref, ga_ref, gb_ref, fl_ref):
    h = _rms(x_ref[...], g_ref[...], NORM_EPS).astype(BF16)
    tm = h.shape[0]
    cw = fq_ref.shape[1]

    def proj(chunk):
        return _dot(h, w_ref[:, chunk * cw:(chunk + 1) * cw])

    fq_ref[...] = (proj(0) * QK_SCALE).astype(BF16)
    fk_ref[...] = proj(1).astype(BF16)
    fv_ref[...] = proj(2).astype(BF16)

    cos = cos_ref[...]
    sin = sin_ref[...]
    lane = lax.broadcasted_iota(jnp.int32, (tm, LANES), 1)
    first_half = (lane % HEAD_DIM) < HEAD_DIM // 2
    for out_ref, chunk, scale in ((dq_ref, 3, QK_SCALE), (dk_ref, 4, 1.0)):
        t = proj(chunk)
        for c in range(cw // LANES):
            blk = _rope(t[:, c * LANES:(c + 1) * LANES], cos, sin, first_half)
            out_ref[:, c * LANES:(c + 1) * LANES] = (blk * scale).astype(BF16)

    dv_ref[...] = proj(5).astype(BF16)
    for out_ref, chunk in ((ga_ref, 6), (gb_ref, 8)):
        for c in range(2):
            out_ref[:, c * cw:(c + 1) * cw] = proj(chunk + c).astype(BF16)
    fl_ref[...] = _dot_nt(wf_ref[...], h)


def _in_proj(x2, g_mix, w_main, w_f, cos_t, sin_t, tm):
    n_tok, d = x2.shape
    cw = 512
    tok = lambda i: (i, 0)
    const = lambda i: (0, 0)
    o512 = jax.ShapeDtypeStruct((n_tok, cw), BF16)
    o1024 = jax.ShapeDtypeStruct((n_tok, 2 * cw), BF16)
    return pl.pallas_call(
        _in_proj_kernel,
        grid=(n_tok // tm,),
        in_specs=[pl.BlockSpec((tm, d), tok),
                  pl.BlockSpec((1, d), const),
                  pl.BlockSpec(w_main.shape, const, pipeline_mode=pl.Buffered(1)),
                  pl.BlockSpec(w_f.shape, const),
                  pl.BlockSpec((tm, LANES), tok),
                  pl.BlockSpec((tm, LANES), tok)],
        out_specs=[pl.BlockSpec((tm, cw), tok)] * 6
                  + [pl.BlockSpec((tm, 2 * cw), tok)] * 2
                  + [pl.BlockSpec((w_f.shape[0], tm), lambda i: (0, i))],
        out_shape=[o512] * 6 + [o1024] * 2
                  + [jax.ShapeDtypeStruct((w_f.shape[0], n_tok), F32)],
        compiler_params=_params("arbitrary"),
        name="in_proj",
    )(x2, g_mix, w_main, w_f, cos_t, sin_t)


def _fox_prep_kernel(fl_ref, b_ref, out_ref, carry_ref):
    @pl.when(pl.program_id(1) == 0)
    def _():
        carry_ref[...] = jnp.zeros_like(carry_ref)

    z = fl_ref[...] + b_ref[...]
    log_f = jnp.minimum(z, 0.0) - jnp.log(1.0 + jnp.exp(-jnp.abs(z)))
    tc = z.shape[1]
    row = lax.broadcasted_iota(jnp.int32, (tc, tc), 0)
    col = lax.broadcasted_iota(jnp.int32, (tc, tc), 1)
    upper = (row <= col).astype(F32)
    c = jnp.dot(log_f, upper, precision=lax.Precision.HIGHEST,
                preferred_element_type=F32) + carry_ref[...]
    carry_ref[...] = c[:, tc - 1:tc]
    out_ref[...] = -c


def _fox_prep(fl_t, b_col, n_batch, seq, tc):
    rows = fl_t.shape[0]
    nblk = seq // tc
    return pl.pallas_call(
        _fox_prep_kernel,
        grid=(n_batch, nblk),
        in_specs=[pl.BlockSpec((rows, tc), lambda b, j: (0, b * nblk + j)),
                  pl.BlockSpec((rows, 1), lambda b, j: (0, 0))],
        out_specs=pl.BlockSpec((rows, tc), lambda b, j: (0, b * nblk + j)),
        out_shape=jax.ShapeDtypeStruct(fl_t.shape, F32),
        scratch_shapes=[pltpu.VMEM((rows, 1), F32)],
        compiler_params=_params("arbitrary", "arbitrary"),
        name="fox_prep",
    )(fl_t, b_col)


def _attn_kernel(*refs, blk, fox, lambda_init):
    if fox:
        q_ref, k_ref, v_ref, bias_ref, out_ref = refs
    else:
        q_ref, k_ref, v_ref, lam_ref, gsub_ref, out_ref = refs
    hp = pl.program_id(1)
    qi = pl.program_id(2)

    q = q_ref[...]
    lane = lax.broadcasted_iota(jnp.int32, q.shape, 1)
    zero = jnp.zeros_like(q)
    q_heads = (jnp.where(lane < HEAD_DIM, q, zero), jnp.where(lane >= HEAD_DIM, q, zero))
    row = lax.broadcasted_iota(jnp.int32, (blk, blk), 0)
    col = lax.broadcasted_iota(jnp.int32, (blk, blk), 1)
    causal = row >= col

    def step(ki, carry, masked):
        k0 = pl.multiple_of(ki * blk, blk)
        kb = k_ref[pl.ds(k0, blk), :]
        vb = v_ref[pl.ds(k0, blk), :]
        out = []
        for idx in range(2):
            m, l, acc = carry[idx]
            s = _dot_nt(q_heads[idx], kb)
            if fox:
                s = s + bias_ref[pl.ds(2 * hp + idx, 1), pl.ds(k0, blk)]
            if masked:
                s = jnp.where(causal, s, NEG_BIG)
            m_new = jnp.maximum(m, jnp.max(s, axis=1, keepdims=True))
            alpha = jnp.exp(m - m_new)
            p = jnp.exp(s - m_new)
            l = alpha * l + jnp.sum(p, axis=1, keepdims=True)
            acc = alpha * acc + _dot(p.astype(BF16), vb)
            out.append((m_new, l, acc))
        return tuple(out)

    init = tuple((jnp.full((blk, 1), NEG_BIG, F32), jnp.zeros((blk, 1), F32),
                  jnp.zeros((blk, LANES), F32)) for _ in range(2))
    carry = lax.fori_loop(0, qi, lambda ki, c: step(ki, c, False), init)
    (_, l_a, acc_a), (_, l_b, acc_b) = step(qi, carry, True)
    o_a = acc_a * (1.0 / l_a)
    o_b = acc_b * (1.0 / l_b)
    if fox:
        out_ref[...] = jnp.where(lane < HEAD_DIM, o_a, o_b).astype(out_ref.dtype)
    else:
        lam_vec = lam_ref[...]
        lam = (jnp.exp(jnp.sum(lam_vec[0:1] * lam_vec[1:2], axis=1, keepdims=True))
               - jnp.exp(jnp.sum(lam_vec[2:3] * lam_vec[3:4], axis=1, keepdims=True))
               + lambda_init)
        y = o_a - lam * o_b
        y = _rms(y, gsub_ref[...], SUBLN_EPS) * (1.0 - lambda_init)
        out_ref[...] = y.astype(out_ref.dtype)


def _attention(q, k, v, extra, n_batch, seq, blk, fox, lambda_init=0.0):
    n_tok, width = q.shape
    n_pairs = width // LANES
    nq = seq // blk
    q_spec = pl.BlockSpec((blk, LANES), lambda b, hp, qi: (b * nq + qi, hp))
    kv_spec = pl.BlockSpec((seq, LANES), lambda b, hp, qi: (b, hp))
    if fox:
        (bias,) = extra
        extra_specs = [pl.BlockSpec((bias.shape[0], seq), lambda b, hp, qi: (0, b))]
    else:
        lam_vec, g_sub = extra
        extra_specs = [pl.BlockSpec(lam_vec.shape, lambda b, hp, qi: (0, 0)),
                       pl.BlockSpec(g_sub.shape, lambda b, hp, qi: (0, 0))]
    return pl.pallas_call(
        functools.partial(_attn_kernel, blk=blk, fox=fox, lambda_init=lambda_init),
        grid=(n_batch, n_pairs, nq),
        in_specs=[q_spec, kv_spec, kv_spec] + extra_specs,
        out_specs=q_spec,
        out_shape=jax.ShapeDtypeStruct((n_tok, width), BF16),
        compiler_params=_params("arbitrary", "arbitrary", "arbitrary"),
        name="fox_attn" if fox else "diff_attn",
    )(q, k, v, *extra)


def _post_mixer_kernel(ya_ref, yb_ref, ga_ref, gb_ref, x_ref, wa_ref, wb_ref, wo_ref,
                       gc_ref, wcq_ref, kc_ref, vc_ref, wco_ref, gf_ref, wr_ref, br_ref,
                       x2_ref, hm_ref, lg_ref):
    def sigmoid(t):
        return 1.0 / (1.0 + jnp.exp(-t))

    merged = (sigmoid(ga_ref[...].astype(F32)) * _dot(ya_ref[...], wa_ref[...])
              + sigmoid(gb_ref[...].astype(F32)) * _dot(yb_ref[...], wb_ref[...]))
    x1 = x_ref[...] + _dot(merged.astype(BF16), wo_ref[...])

    hx = _rms(x1, gc_ref[...], NORM_EPS).astype(BF16)
    qc = (_dot(hx, wcq_ref[...]) * QK_SCALE).astype(BF16)
    kc = kc_ref[...]
    vc = vc_ref[...]
    lane = lax.broadcasted_iota(jnp.int32, qc.shape, 1)
    zero = jnp.zeros_like(qc)
    o = jnp.zeros(qc.shape, F32)
    for h in range(CROSS_HEADS):
        in_head = (lane >= h * HEAD_DIM) & (lane < (h + 1) * HEAD_DIM)
        s = _dot_nt(jnp.where(in_head, qc, zero), kc)
        p = jnp.exp(s - jnp.max(s, axis=1, keepdims=True))
        p = p * (1.0 / jnp.sum(p, axis=1, keepdims=True))
        o = jnp.where(in_head, _dot(p.astype(BF16), vc), o)
    x2 = x1 + _dot(o.astype(BF16), wco_ref[...])
    x2_ref[...] = x2

    hm = _rms(x2, gf_ref[...], NORM_EPS)
    hm_ref[...] = hm.astype(BF16)
    lg_ref[...] = jnp.dot(hm, wr_ref[...], precision=lax.Precision.HIGHEST,
                          preferred_element_type=F32) + br_ref[...]


def _post_mixer(ya, yb, ga, gb, x2d, wa, wb, wo, g_cross, wcq, kc, vc, wco, g_ffn, wr, br,
                n_batch, seq, mem_len, tm):
    n_tok, d = x2d.shape
    nblk = seq // tm
    tok = lambda b, j: (b * nblk + j, 0)
    const = lambda b, j: (0, 0)
    full = lambda a: pl.BlockSpec(a.shape, const)
    return pl.pallas_call(
        _post_mixer_kernel,
        grid=(n_batch, nblk),
        in_specs=[pl.BlockSpec((tm, ya.shape[1]), tok), pl.BlockSpec((tm, yb.shape[1]), tok),
                  pl.BlockSpec((tm, d), tok), pl.BlockSpec((tm, d), tok), pl.BlockSpec((tm, d), tok),
                  full(wa), full(wb), full(wo), full(g_cross), full(wcq),
                  pl.BlockSpec((mem_len, kc.shape[1]), lambda b, j: (b, 0)),
                  pl.BlockSpec((mem_len, vc.shape[1]), lambda b, j: (b, 0)),
                  full(wco), full(g_ffn), full(wr), full(br)],
        out_specs=[pl.BlockSpec((tm, d), tok), pl.BlockSpec((tm, d), tok),
                   pl.BlockSpec((tm, LANES), tok)],
        out_shape=[jax.ShapeDtypeStruct((n_tok, d), F32),
                   jax.ShapeDtypeStruct((n_tok, d), BF16),
                   jax.ShapeDtypeStruct((n_tok, LANES), F32)],
        compiler_params=_params("arbitrary", "arbitrary"),
        name="post_mixer",
    )(ya, yb, ga, gb, x2d, wa, wb, wo, g_cross, wcq, kc, vc, wco, g_ffn, wr, br)


def _expert_kernel(be_ref, nused_ref, xs_ref, rw_ref, wg_ref, wu_ref, wd_ref, ys_ref):
    i = pl.program_id(0)

    @pl.when(i < nused_ref[0])
    def _():
        xb = xs_ref[...]
        g = _dot(xb, wg_ref[0])
        u = _dot(xb, wu_ref[0])
        hb = (g * (1.0 / (1.0 + jnp.exp(-g))) * u).astype(BF16)
        ys_ref[...] = _dot(hb, wd_ref[0]) * rw_ref[...]

    @pl.when(i >= nused_ref[0])
    def _():
        ys_ref[...] = jnp.zeros_like(ys_ref)


def _experts(block_expert, n_used, xs, row_w, wg, wu, wd):
    n_rows, d = xs.shape
    n_blocks = n_rows // MOE_BLOCK
    ff = wg.shape[2]
    grid_spec = pltpu.PrefetchScalarGridSpec(
        num_scalar_prefetch=2,
        grid=(n_blocks,),
        in_specs=[pl.BlockSpec((MOE_BLOCK, d), lambda i, be, nu: (i, 0)),
                  pl.BlockSpec((MOE_BLOCK, 1), lambda i, be, nu: (i, 0)),
                  pl.BlockSpec((1, d, ff), lambda i, be, nu: (be[i], 0, 0)),
                  pl.BlockSpec((1, d, ff), lambda i, be, nu: (be[i], 0, 0)),
                  pl.BlockSpec((1, ff, d), lambda i, be, nu: (be[i], 0, 0))],
        out_specs=pl.BlockSpec((MOE_BLOCK, d), lambda i, be, nu: (i, 0)),
    )
    return pl.pallas_call(
        _expert_kernel,
        grid_spec=grid_spec,
        out_shape=jax.ShapeDtypeStruct((n_rows, d), F32),
        compiler_params=_params("arbitrary"),
        name="experts",
    )(block_expert, n_used, xs, row_w, wg, wu, wd)


def _final_kernel(x_ref, moe_ref, g_ref, out_ref):
    out_ref[...] = _rms(x_ref[...] + moe_ref[...], g_ref[...], NORM_EPS)


def _final(x2, moe, g_final, tm):
    n_tok, d = x2.shape
    tok = lambda i: (i, 0)
    return pl.pallas_call(
        _final_kernel,
        grid=(n_tok // tm,),
        in_specs=[pl.BlockSpec((tm, d), tok), pl.BlockSpec((tm, d), tok),
                  pl.BlockSpec((1, d), lambda i: (0, 0))],
        out_specs=pl.BlockSpec((tm, d), tok),
        out_shape=jax.ShapeDtypeStruct((n_tok, d), F32),
        compiler_params=_params("arbitrary"),
        name="final_norm",
    )(x2, moe, g_final)


def _route(logits, n_tok):
    group_logits = logits[:, :N_GROUPS]
    group_prob = jax.nn.softmax(group_logits, axis=-1)
    g_sel = jnp.argmax(group_logits, axis=-1)
    g_w = jnp.take_along_axis(group_prob, g_sel[:, None], axis=-1)
    exp_logits = logits[:, N_GROUPS:N_GROUPS + N_EXPERTS].reshape(n_tok, N_GROUPS, EXPERTS_PER_GROUP)
    sel_logits = jnp.take_along_axis(exp_logits, g_sel[:, None, None], axis=1)[:, 0]
    top_vals, top_idx = lax.top_k(sel_logits, TOP_K)
    weights = jax.nn.softmax(top_vals, axis=-1) * g_w
    expert_ids = g_sel[:, None].astype(jnp.int32) * EXPERTS_PER_GROUP + top_idx.astype(jnp.int32)

    n_assign = n_tok * TOP_K
    flat_e = expert_ids.reshape(n_assign)
    flat_w = weights.reshape(n_assign)
    flat_tok = jnp.arange(n_assign, dtype=jnp.int32) // TOP_K
    onehot = (flat_e[:, None] == jnp.arange(N_EXPERTS, dtype=jnp.int32)[None, :]).astype(jnp.int32)
    rank = jnp.take_along_axis(jnp.cumsum(onehot, axis=0), flat_e[:, None], axis=1)[:, 0] - 1
    counts = jnp.sum(onehot, axis=0)
    padded = (counts + MOE_BLOCK - 1) // MOE_BLOCK * MOE_BLOCK
    pad_end = jnp.cumsum(padded)
    pad_start = pad_end - padded
    dest = pad_start[flat_e] + rank
    n_blocks = -(-n_assign // MOE_BLOCK) + N_EXPERTS
    n_rows = n_blocks * MOE_BLOCK
    row_tok = jnp.full((n_rows,), n_tok, jnp.int32).at[dest].set(flat_tok)
    row_w = jnp.zeros((n_rows,), F32).at[dest].set(flat_w)
    block_start = jnp.arange(n_blocks, dtype=jnp.int32) * MOE_BLOCK
    block_expert = jnp.minimum(jnp.searchsorted(pad_end, block_start, side='right'),
                               N_EXPERTS - 1).astype(jnp.int32)
    n_used = (pad_end[-1] // MOE_BLOCK).astype(jnp.int32).reshape(1)
    return row_tok, row_w, block_expert, n_used


def kernel(x, mem, positions, g_mix, w_in, b_fgate, w_branch_a, w_branch_b, w_out, lambda_q1, lambda_k1, lambda_q2, lambda_k2, g_diff_sub, g_cross, g_mem, w_cq, w_ckv, w_co, g_ffn, w_group, b_group, w_expert, b_expert, w_exp_gate, w_exp_up, w_exp_down, g_final):
    n_batch, seq, d = x.shape
    mem_len = mem.shape[1]
    depth = g_mix.shape[0]
    n_tok = n_batch * seq
    fox_w = FOX_HEADS * HEAD_DIM
    diff_w = DIFF_HEADS * 2 * HEAD_DIM

    half = HEAD_DIM // 2
    inv_freq = 10000.0 ** (-jnp.arange(half, dtype=F32) * 2.0 / HEAD_DIM)
    ang = positions.astype(F32).reshape(n_tok, 1) * inv_freq
    cos_t = jnp.tile(jnp.cos(ang), (1, LANES // half))
    sin_t = jnp.tile(jnp.concatenate([-jnp.sin(ang), jnp.sin(ang)], axis=1), (1, LANES // HEAD_DIM))

    x2d = x.reshape(n_tok, d)
    mem2d = mem.reshape(n_batch * mem_len, d)
    for l in range(depth):
        lambda_init = 0.8 - 0.6 * math.exp(-0.3 * l)
        o_f = 3 * fox_w
        w_main = jnp.concatenate([w_in[l][:, :o_f], w_in[l][:, o_f + FOX_HEADS:]], axis=1).astype(BF16)
        w_f = jnp.pad(w_in[l][:, o_f:o_f + FOX_HEADS].T, ((0, 16 - FOX_HEADS), (0, 0))).astype(BF16)
        b_col = jnp.pad(b_fgate[l], (0, 16 - FOX_HEADS)).reshape(16, 1)

        kc, vc = _mem_kv(mem2d, g_mem[l][None], w_ckv[l].astype(BF16), n_batch, mem_len)
        fq, fk, fv, dq, dk, dv, ga, gb, fl_t = _in_proj(
            x2d, g_mix[l][None], w_main, w_f, cos_t, sin_t, tm=512)
        bias = _fox_prep(fl_t, b_col, n_batch, seq, tc=512)
        y_a = _attention(fq, fk, fv, (bias,), n_batch, seq, blk=256, fox=True)
        lam_vec = jnp.stack([lambda_q1[l], lambda_k1[l], lambda_q2[l], lambda_k2[l]])
        y_b = _attention(dq, dk, dv, (lam_vec, g_diff_sub[l][None]), n_batch, seq, blk=256,
                         fox=False, lambda_init=lambda_init)

        w_router = jnp.pad(jnp.concatenate([w_group[l], w_expert[l]], axis=1),
                           ((0, 0), (0, LANES - N_GROUPS - N_EXPERTS)))
        b_router = jnp.pad(jnp.concatenate([b_group[l], b_expert[l]]),
                           (0, LANES - N_GROUPS - N_EXPERTS))[None]
        x2d, hm, logits = _post_mixer(
            y_a, y_b, ga, gb, x2d, w_branch_a[l].astype(BF16), w_branch_b[l].astype(BF16),
            w_out[l].astype(BF16), g_cross[l][None], w_cq[l].astype(BF16), kc, vc,
            w_co[l].astype(BF16), g_ffn[l][None], w_router, b_router,
            n_batch, seq, mem_len, tm=512)

        row_tok, row_w, block_expert, n_used = _route(logits, n_tok)
        hm_pad = jnp.concatenate([hm, jnp.zeros((1, d), BF16)], axis=0)
        xs = hm_pad[row_tok]
        ys = _experts(block_expert, n_used, xs, row_w[:, None], w_exp_gate[l].astype(BF16),
                      w_exp_up[l].astype(BF16), w_exp_down[l].astype(BF16))
        moe = jnp.zeros((n_tok + 1, d), F32).at[row_tok].add(ys)[:n_tok]
        if l + 1 < depth:
            x2d = x2d + moe
    return _final(x2d, moe, g_final[None], tm=512).reshape(n_batch, seq, d)
```

```python
import functools
import math

import numpy as np
import jax
import jax.numpy as jnp
from jax import lax
from jax.experimental import pallas as pl
from jax.experimental.pallas import tpu as pltpu

HEAD_DIM = 64
LANES = 128
FOX_HEADS = 8
DIFF_HEADS = 4
CROSS_HEADS = 4
N_GROUPS = 4
EXPERTS_PER_GROUP = 8
N_EXPERTS = N_GROUPS * EXPERTS_PER_GROUP
TOP_K = 2
MOE_BLOCK = 256
ATTN_Q_BLOCK = 512
ATTN_K_BLOCK = 256
NORM_EPS = 1e-6
SUBLN_EPS = 1e-5
QK_SCALE = HEAD_DIM ** -0.5
LOG2E = math.log2(math.e)
Q_PRESCALE = QK_SCALE * LOG2E
NEG_BIG = -1e30
VMEM_LIMIT = 56 * 2**20

BF16 = jnp.bfloat16
F32 = jnp.float32


def _rms(t, g, eps):
    return t * lax.rsqrt(jnp.mean(t * t, axis=-1, keepdims=True) + eps) * g


def _dot(a, b):
    return jnp.dot(a, b, preferred_element_type=F32)


def _dot_nt(a, b):
    return lax.dot_general(a, b, (((1,), (1,)), ((), ())), preferred_element_type=F32)


def _params(*sem):
    return pltpu.CompilerParams(dimension_semantics=sem, vmem_limit_bytes=VMEM_LIMIT)


def _mem_kv_kernel(mem_ref, g_ref, w_ref, k_ref, v_ref):
    h = _rms(mem_ref[...], g_ref[...], NORM_EPS).astype(BF16)
    kv = _dot(h, w_ref[...])
    width = k_ref.shape[1]
    k_ref[...] = kv[:, :width].astype(BF16)
    v_ref[...] = kv[:, width:].astype(BF16)


def _mem_kv(mem2, g_mem, w_ckv, n_batch, mem_len):
    d = mem2.shape[1]
    cw = w_ckv.shape[1] // 2
    out = jax.ShapeDtypeStruct((n_batch * mem_len, cw), BF16)
    return pl.pallas_call(
        _mem_kv_kernel,
        grid=(n_batch,),
        in_specs=[pl.BlockSpec((mem_len, d), lambda b: (b, 0)),
                  pl.BlockSpec((1, d), lambda b: (0, 0)),
                  pl.BlockSpec((d, 2 * cw), lambda b: (0, 0))],
        out_specs=[pl.BlockSpec((mem_len, cw), lambda b: (b, 0))] * 2,
        out_shape=[out, out],
        compiler_params=_params("arbitrary"),
        name="mem_kv",
    )(mem2, g_mem, w_ckv)


def _rope(t, cos, sin_signed, first_half):
    fwd = pltpu.roll(t, LANES - HEAD_DIM // 2, axis=1)
    bwd = pltpu.roll(t, HEAD_DIM // 2, axis=1)
    return t * cos + jnp.where(first_half, fwd, bwd) * sin_signed


def _in_proj_kernel(x_ref, g_ref, w_ref, wvt_ref, wf_ref, cos_ref, sin_ref,
                    fq_ref, fk_ref, dq_ref, dk_ref, ga_ref, gb_ref, fvt_ref, dvt_ref, fl_ref):
    h = _rms(x_ref[...], g_ref[...], NORM_EPS).astype(BF16)
    tm = h.shape[0]
    cw = fq_ref.shape[1]

    def proj(chunk):
        return _dot(h, w_ref[:, chunk * cw:(chunk + 1) * cw])

    fq_ref[...] = (proj(0) * Q_PRESCALE).astype(BF16)
    fk_ref[...] = proj(1).astype(BF16)

    cos = cos_ref[...]
    sin = sin_ref[...]
    lane = lax.broadcasted_iota(jnp.int32, (tm, LANES), 1)
    first_half = (lane % HEAD_DIM) < HEAD_DIM // 2
    for out_ref, chunk, scale in ((dq_ref, 2, Q_PRESCALE), (dk_ref, 3, 1.0)):
        t = proj(chunk)
        for c in range(cw // LANES):
            blk = _rope(t[:, c * LANES:(c + 1) * LANES], cos, sin, first_half)
            out_ref[:, c * LANES:(c + 1) * LANES] = (blk * scale).astype(BF16)

    for out_ref, chunk in ((ga_ref, 4), (gb_ref, 6)):
        for c in range(2):
            out_ref[:, c * cw:(c + 1) * cw] = proj(chunk + c).astype(BF16)
    fvt_ref[...] = _dot_nt(wvt_ref[0:cw, :], h).astype(BF16)
    dvt_ref[...] = _dot_nt(wvt_ref[cw:2 * cw, :], h).astype(BF16)
    fl_ref[...] = _dot(h, wf_ref[...])


def _in_proj(x2, g_mix, w_main, w_vt, w_f, cos_t, sin_t, tm):
    n_tok, d = x2.shape
    cw = 512
    tok = lambda i: (i, 0)
    tok_t = lambda i: (0, i)
    const = lambda i: (0, 0)
    o512 = jax.ShapeDtypeStruct((n_tok, cw), BF16)
    o1024 = jax.ShapeDtypeStruct((n_tok, 2 * cw), BF16)
    o512_t = jax.ShapeDtypeStruct((cw, n_tok), BF16)
    return pl.pallas_call(
        _in_proj_kernel,
        grid=(n_tok // tm,),
        in_specs=[pl.BlockSpec((tm, d), tok),
                  pl.BlockSpec((1, d), const),
                  pl.BlockSpec(w_main.shape, const, pipeline_mode=pl.Buffered(1)),
                  pl.BlockSpec(w_vt.shape, const, pipeline_mode=pl.Buffered(1)),
                  pl.BlockSpec(w_f.shape, const),
                  pl.BlockSpec((tm, LANES), tok),
                  pl.BlockSpec((tm, LANES), tok)],
        out_specs=[pl.BlockSpec((tm, cw), tok)] * 4
                  + [pl.BlockSpec((tm, 2 * cw), tok)] * 2
                  + [pl.BlockSpec((cw, tm), tok_t)] * 2
                  + [pl.BlockSpec((tm, LANES), tok)],
        out_shape=[o512] * 4 + [o1024] * 2 + [o512_t] * 2
                  + [jax.ShapeDtypeStruct((n_tok, LANES), F32)],
        compiler_params=_params("arbitrary"),
        name="in_proj",
    )(x2, g_mix, w_main, w_vt, w_f, cos_t, sin_t)


def _bias_lane_selectors():
    sel = np.zeros((3, LANES, FOX_HEADS // 2 * LANES), np.float32)
    for h in range(FOX_HEADS):
        base = (h // 2) * LANES + (HEAD_DIM if h % 2 == 0 else 0)
        for piece in range(3):
            sel[piece, h, base + piece] = 1.0
    return jnp.asarray(sel, BF16)


def _fox_prep_kernel(fl_ref, b_ref, sel_ref, out_ref, carry_ref):
    @pl.when(pl.program_id(1) == 0)
    def _():
        carry_ref[...] = jnp.zeros_like(carry_ref)

    z = fl_ref[...] + b_ref[...]
    log_f = jnp.minimum(z, 0.0) - jnp.log(1.0 + jnp.exp(-jnp.abs(z)))
    tc = z.shape[0]
    row = lax.broadcasted_iota(jnp.int32, (tc, tc), 0)
    col = lax.broadcasted_iota(jnp.int32, (tc, tc), 1)
    lower = (row >= col).astype(F32)
    c = jnp.dot(lower, log_f, precision=lax.Precision.HIGHEST,
                preferred_element_type=F32) + carry_ref[...]
    carry_ref[...] = c[tc - 1:tc, :]
    rest = c * (-LOG2E)
    acc = jnp.zeros(out_ref.shape, F32)
    for piece in range(3):
        part = rest.astype(BF16)
        rest = rest - part.astype(F32)
        acc = acc + _dot(part, sel_ref[piece])
    out_ref[...] = acc.astype(BF16)


def _fox_prep(fl, b_row, sel, n_batch, seq, tc):
    n_tok = fl.shape[0]
    nblk = seq // tc
    width = sel.shape[2]
    return pl.pallas_call(
        _fox_prep_kernel,
        grid=(n_batch, nblk),
        in_specs=[pl.BlockSpec((tc, LANES), lambda b, j: (b * nblk + j, 0)),
                  pl.BlockSpec((1, LANES), lambda b, j: (0, 0)),
                  pl.BlockSpec(sel.shape, lambda b, j: (0, 0, 0))],
        out_specs=pl.BlockSpec((tc, width), lambda b, j: (b * nblk + j, 0)),
        out_shape=jax.ShapeDtypeStruct((n_tok, width), BF16),
        scratch_shapes=[pltpu.VMEM((1, LANES), F32)],
        compiler_params=_params("arbitrary", "arbitrary"),
        name="fox_prep",
    )(fl, b_row, sel)


def _attn_kernel(*refs, bq, bk, fox, lambda_init):
    if fox:
        q_ref, k_ref, vt_ref, bl_ref, out_ref, qs_ref, st_ref, p_ref, acc_ref, stat_ref, ka_ref, kb_ref = refs
    else:
        q_ref, k_ref, vt_ref, lam_ref, gsub_ref, out_ref, qs_ref, st_ref, p_ref, acc_ref, stat_ref = refs
    qi = pl.program_id(2)

    q = q_ref[...].astype(F32)
    lane = lax.broadcasted_iota(jnp.int32, q.shape, 1)
    if fox:
        @pl.when(qi == 0)
        def _():
            k = k_ref[...].astype(F32)
            bl = bl_ref[...].astype(F32)
            klane = lax.broadcasted_iota(jnp.int32, k.shape, 1)
            ka_ref[...] = jnp.where(klane < HEAD_DIM, k, bl).astype(BF16)
            kb_ref[...] = jnp.where(klane >= HEAD_DIM, k, bl).astype(BF16)

        q_streams = (jnp.where(lane < HEAD_DIM, q, jnp.where(lane < HEAD_DIM + 3, 1.0, 0.0)),
                     jnp.where(lane >= HEAD_DIM, q, jnp.where(lane < 3, 1.0, 0.0)))
        k_srcs = (ka_ref, kb_ref)
        v_rows = ((0, HEAD_DIM), (HEAD_DIM, 2 * HEAD_DIM))
    else:
        q_streams = (jnp.where(lane < HEAD_DIM, q, 0.0), jnp.where(lane >= HEAD_DIM, q, 0.0))
        k_srcs = (k_ref, k_ref)
        v_rows = ((0, 2 * HEAD_DIM), (0, 2 * HEAD_DIM))
    n_v = v_rows[0][1] - v_rows[0][0]
    for idx in range(2):
        qs_ref[idx] = q_streams[idx].astype(BF16)
        p_ref[idx] = jnp.zeros((bk, bq), BF16)
        acc_ref[idx] = jnp.zeros((n_v + 16, bq), F32)
    ones_rows = jnp.ones((16, bk), BF16)
    diag_blocks = bq // bk
    first_diag = qi * diag_blocks

    def qk_stage(kj, idx):
        k0 = pl.multiple_of(kj * bk, bk)
        st_ref[idx] = _dot_nt(k_srcs[idx][pl.ds(k0, bk), :], qs_ref[idx])

    M_ROW, A_ROW = 0, 8

    def pv_stage(kj, idx):
        k0 = pl.multiple_of(kj * bk, bk)
        r0, r1 = v_rows[idx]
        v_aug = jnp.concatenate([vt_ref[r0:r1, pl.ds(k0, bk)], ones_rows], axis=0)
        alpha = stat_ref[idx, A_ROW:A_ROW + 1, :]
        acc_ref[idx] = alpha * acc_ref[idx] + _dot(v_aug, p_ref[idx])

    def softmax_stage(idx, diag):
        st = st_ref[idx]
        if diag is not None:
            key_idx = lax.broadcasted_iota(jnp.int32, (bk, bq), 0) + diag * bk
            qry_idx = lax.broadcasted_iota(jnp.int32, (bk, bq), 1)
            st = jnp.where(key_idx <= qry_idx, st, NEG_BIG)
        m = stat_ref[idx, M_ROW:M_ROW + 1, :]
        m_new = jnp.maximum(m, jnp.max(st, axis=0, keepdims=True))
        p_ref[idx] = jnp.exp2(st - m_new).astype(BF16)
        stat_ref[idx, M_ROW:M_ROW + 1, :] = m_new
        stat_ref[idx, A_ROW:A_ROW + 1, :] = jnp.exp2(m - m_new)

    for idx in range(2):
        stat_ref[idx, M_ROW:M_ROW + 8, :] = jnp.full((8, bq), NEG_BIG, F32)
        stat_ref[idx, A_ROW:A_ROW + 8, :] = jnp.ones((8, bq), F32)
        qk_stage(0, idx)

    def body(j, carry):
        for idx in range(2):
            pv_stage(jnp.maximum(j - 1, 0), idx)
            softmax_stage(idx, None)
            qk_stage(j + 1, idx)
        return carry

    lax.fori_loop(0, first_diag, body, 0)
    for diag in range(diag_blocks):
        for idx in range(2):
            pv_stage(jnp.maximum(first_diag + diag - 1, 0), idx)
            softmax_stage(idx, diag)
            if diag + 1 < diag_blocks:
                qk_stage(first_diag + diag + 1, idx)
    outs = []
    for idx in range(2):
        pv_stage(first_diag + diag_blocks - 1, idx)
        acc = acc_ref[idx]
        outs.append(acc[0:n_v] * (1.0 / acc[n_v:n_v + 1]))
    o_a, o_b = outs
    if fox:
        y = jnp.concatenate([o_a, o_b], axis=0)
    else:
        lam_vec = lam_ref[...]
        lam = (jnp.exp(jnp.sum(lam_vec[0:1] * lam_vec[1:2], axis=1, keepdims=True))
               - jnp.exp(jnp.sum(lam_vec[2:3] * lam_vec[3:4], axis=1, keepdims=True))
               + lambda_init)
        y = o_a - lam * o_b
        y = y * lax.rsqrt(jnp.mean(y * y, axis=0, keepdims=True) + SUBLN_EPS)
        y = y * (gsub_ref[...] * (1.0 - lambda_init))
    out_ref[...] = y.T.astype(out_ref.dtype)


def _attention(q, k, vt, extra, n_batch, seq, fox, lambda_init=0.0):
    n_tok, width = q.shape
    n_pairs = width // LANES
    bq, bk = ATTN_Q_BLOCK, ATTN_K_BLOCK
    nq = seq // bq
    n_v = HEAD_DIM if fox else 2 * HEAD_DIM
    q_spec = pl.BlockSpec((bq, LANES), lambda b, hp, qi: (b * nq + qi, hp))
    k_spec = pl.BlockSpec((seq, LANES), lambda b, hp, qi: (b, hp))
    vt_spec = pl.BlockSpec((LANES, seq), lambda b, hp, qi: (hp, b))
    scratch = [pltpu.VMEM((2, bq, LANES), BF16),
               pltpu.VMEM((2, bk, bq), F32),
               pltpu.VMEM((2, bk, bq), BF16),
               pltpu.VMEM((2, n_v + 16, bq), F32),
               pltpu.VMEM((2, 16, bq), F32)]
    if fox:
        extra_specs = [k_spec]
        scratch += [pltpu.VMEM((seq, LANES), BF16)] * 2
    else:
        lam_vec, g_sub = extra
        extra_specs = [pl.BlockSpec(lam_vec.shape, lambda b, hp, qi: (0, 0)),
                       pl.BlockSpec(g_sub.shape, lambda b, hp, qi: (0, 0))]
    return pl.pallas_call(
        functools.partial(_attn_kernel, bq=bq, bk=bk, fox=fox, lambda_init=lambda_init),
        grid=(n_batch, n_pairs, nq),
        in_specs=[q_spec, k_spec, vt_spec] + extra_specs,
        out_specs=q_spec,
        out_shape=jax.ShapeDtypeStruct((n_tok, width), BF16),
        scratch_shapes=scratch,
        compiler_params=_params("arbitrary", "arbitrary", "arbitrary"),
        name="fox_attn" if fox else "diff_attn",
    )(q, k, vt, *extra)


def _post_mixer_kernel(ya_ref, yb_ref, ga_ref, gb_ref, x_ref, wa_ref, wb_ref, wo_ref,
                       gc_ref, wcq_ref, kc_ref, vc_ref, wco_ref, gf_ref, wr_ref, br_ref,
                       x2_ref, hm_ref, lg_ref):
    def sigmoid(t):
        return 1.0 / (1.0 + jnp.exp(-t))

    merged = (sigmoid(ga_ref[...].astype(F32)) * _dot(ya_ref[...], wa_ref[...])
              + sigmoid(gb_ref[...].astype(F32)) * _dot(yb_ref[...], wb_ref[...]))
    x1 = x_ref[...] + _dot(merged.astype(BF16), wo_ref[...])

    hx = _rms(x1, gc_ref[...], NORM_EPS).astype(BF16)
    qc = (_dot(hx, wcq_ref[...]) * QK_SCALE).astype(BF16)
    kc = kc_ref[...]
    vc = vc_ref[...]
    lane = lax.broadcasted_iota(jnp.int32, qc.shape, 1)
    zero = jnp.zeros_like(qc)
    o = jnp.zeros(qc.shape, F32)
    for h in range(CROSS_HEADS):
        in_head = (lane >= h * HEAD_DIM) & (lane < (h + 1) * HEAD_DIM)
        s = _dot_nt(jnp.where(in_head, qc, zero), kc)
        p = jnp.exp(s - jnp.max(s, axis=1, keepdims=True))
        p = p * (1.0 / jnp.sum(p, axis=1, keepdims=True))
        o = jnp.where(in_head, _dot(p.astype(BF16), vc), o)
    x2 = x1 + _dot(o.astype(BF16), wco_ref[...])
    x2_ref[...] = x2

    hm = _rms(x2, gf_ref[...], NORM_EPS)
    hm_ref[...] = hm.astype(BF16)
    lg_ref[...] = jnp.dot(hm, wr_ref[...], precision=lax.Precision.HIGHEST,
                          preferred_element_type=F32) + br_ref[...]


def _post_mixer(ya, yb, ga, gb, x2d, wa, wb, wo, g_cross, wcq, kc, vc, wco, g_ffn, wr, br,
                n_batch, seq, mem_len, tm):
    n_tok, d = x2d.shape
    nblk = seq // tm
    tok = lambda b, j: (b * nblk + j, 0)
    const = lambda b, j: (0, 0)
    full = lambda a: pl.BlockSpec(a.shape, const)
    return pl.pallas_call(
        _post_mixer_kernel,
        grid=(n_batch, nblk),
        in_specs=[pl.BlockSpec((tm, ya.shape[1]), tok), pl.BlockSpec((tm, yb.shape[1]), tok),
                  pl.BlockSpec((tm, d), tok), pl.BlockSpec((tm, d), tok), pl.BlockSpec((tm, d), tok),
                  full(wa), full(wb), full(wo), full(g_cross), full(wcq),
                  pl.BlockSpec((mem_len, kc.shape[1]), lambda b, j: (b, 0)),
                  pl.BlockSpec((mem_len, vc.shape[1]), lambda b, j: (b, 0)),
                  full(wco), full(g_ffn), full(wr), full(br)],
        out_specs=[pl.BlockSpec((tm, d), tok), pl.BlockSpec((tm, d), tok),
                   pl.BlockSpec((tm, LANES), tok)],
        out_shape=[jax.ShapeDtypeStruct((n_tok, d), F32),
                   jax.ShapeDtypeStruct((n_tok, d), BF16),
                   jax.ShapeDtypeStruct((n_tok, LANES), F32)],
        compiler_params=_params("arbitrary", "arbitrary"),
        name="post_mixer",
    )(ya, yb, ga, gb, x2d, wa, wb, wo, g_cross, wcq, kc, vc, wco, g_ffn, wr, br)


def _expert_kernel(be_ref, nused_ref, xs_ref, rw_ref, wg_ref, wu_ref, wd_ref, ys_ref):
    i = pl.program_id(0)

    @pl.when(i < nused_ref[0])
    def _():
        xb = xs_ref[...]
        g = _dot(xb, wg_ref[0])
        u = _dot(xb, wu_ref[0])
        hb = (g * (1.0 / (1.0 + jnp.exp(-g))) * u).astype(BF16)
        ys_ref[...] = _dot(hb, wd_ref[0]) * rw_ref[...]

    @pl.when(i >= nused_ref[0])
    def _():
        ys_ref[...] = jnp.zeros_like(ys_ref)


def _experts(block_expert, n_used, xs, row_w, wg, wu, wd):
    n_rows, d = xs.shape
    n_blocks = n_rows // MOE_BLOCK
    ff = wg.shape[2]
    grid_spec = pltpu.PrefetchScalarGridSpec(
        num_scalar_prefetch=2,
        grid=(n_blocks,),
        in_specs=[pl.BlockSpec((MOE_BLOCK, d), lambda i, be, nu: (i, 0)),
                  pl.BlockSpec((MOE_BLOCK, 1), lambda i, be, nu: (i, 0)),
                  pl.BlockSpec((1, d, ff), lambda i, be, nu: (be[i], 0, 0)),
                  pl.BlockSpec((1, d, ff), lambda i, be, nu: (be[i], 0, 0)),
                  pl.BlockSpec((1, ff, d), lambda i, be, nu: (be[i], 0, 0))],
        out_specs=pl.BlockSpec((MOE_BLOCK, d), lambda i, be, nu: (i, 0)),
    )
    return pl.pallas_call(
        _expert_kernel,
        grid_spec=grid_spec,
        out_shape=jax.ShapeDtypeStruct((n_rows, d), F32),
        compiler_params=_params("arbitrary"),
        name="experts",
    )(block_expert, n_used, xs, row_w, wg, wu, wd)


def _final_kernel(x_ref, moe_ref, g_ref, out_ref):
    out_ref[...] = _rms(x_ref[...] + moe_ref[...], g_ref[...], NORM_EPS)


def _final(x2, moe, g_final, tm):
    n_tok, d = x2.shape
    tok = lambda i: (i, 0)
    return pl.pallas_call(
        _final_kernel,
        grid=(n_tok // tm,),
        in_specs=[pl.BlockSpec((tm, d), tok), pl.BlockSpec((tm, d), tok),
                  pl.BlockSpec((1, d), lambda i: (0, 0))],
        out_specs=pl.BlockSpec((tm, d), tok),
        out_shape=jax.ShapeDtypeStruct((n_tok, d), F32),
        compiler_params=_params("arbitrary"),
        name="final_norm",
    )(x2, moe, g_final)


def _route(logits, n_tok):
    group_logits = logits[:, :N_GROUPS]
    group_prob = jax.nn.softmax(group_logits, axis=-1)
    g_sel = jnp.argmax(group_logits, axis=-1)
    g_w = jnp.take_along_axis(group_prob, g_sel[:, None], axis=-1)
    exp_logits = logits[:, N_GROUPS:N_GROUPS + N_EXPERTS].reshape(n_tok, N_GROUPS, EXPERTS_PER_GROUP)
    sel_logits = jnp.take_along_axis(exp_logits, g_sel[:, None, None], axis=1)[:, 0]
    top_vals, top_idx = lax.top_k(sel_logits, TOP_K)
    weights = jax.nn.softmax(top_vals, axis=-1) * g_w
    expert_ids = g_sel[:, None].astype(jnp.int32) * EXPERTS_PER_GROUP + top_idx.astype(jnp.int32)

    n_assign = n_tok * TOP_K
    flat_e = expert_ids.reshape(n_assign)
    flat_w = weights.reshape(n_assign)
    flat_tok = jnp.arange(n_assign, dtype=jnp.int32) // TOP_K
    onehot = (flat_e[:, None] == jnp.arange(N_EXPERTS, dtype=jnp.int32)[None, :]).astype(jnp.int32)
    rank = jnp.take_along_axis(jnp.cumsum(onehot, axis=0), flat_e[:, None], axis=1)[:, 0] - 1
    counts = jnp.sum(onehot, axis=0)
    padded = (counts + MOE_BLOCK - 1) // MOE_BLOCK * MOE_BLOCK
    pad_end = jnp.cumsum(padded)
    pad_start = pad_end - padded
    dest = pad_start[flat_e] + rank
    n_blocks = -(-n_assign // MOE_BLOCK) + N_EXPERTS
    n_rows = n_blocks * MOE_BLOCK
    row_tok = jnp.full((n_rows,), n_tok, jnp.int32).at[dest].set(flat_tok)
    row_w = jnp.zeros((n_rows,), F32).at[dest].set(flat_w)
    block_start = jnp.arange(n_blocks, dtype=jnp.int32) * MOE_BLOCK
    block_expert = jnp.minimum(jnp.searchsorted(pad_end, block_start, side='right'),
                               N_EXPERTS - 1).astype(jnp.int32)
    n_used = (pad_end[-1] // MOE_BLOCK).astype(jnp.int32).reshape(1)
    return row_tok, row_w, block_expert, n_used


def kernel(x, mem, positions, g_mix, w_in, b_fgate, w_branch_a, w_branch_b, w_out, lambda_q1, lambda_k1, lambda_q2, lambda_k2, g_diff_sub, g_cross, g_mem, w_cq, w_ckv, w_co, g_ffn, w_group, b_group, w_expert, b_expert, w_exp_gate, w_exp_up, w_exp_down, g_final):
    n_batch, seq, d = x.shape
    mem_len = mem.shape[1]
    depth = g_mix.shape[0]
    n_tok = n_batch * seq
    fox_w = FOX_HEADS * HEAD_DIM
    diff_w = DIFF_HEADS * 2 * HEAD_DIM

    half = HEAD_DIM // 2
    inv_freq = 10000.0 ** (-jnp.arange(half, dtype=F32) * 2.0 / HEAD_DIM)
    ang = positions.astype(F32).reshape(n_tok, 1) * inv_freq
    cos_t = jnp.tile(jnp.cos(ang), (1, LANES // half))
    sin_t = jnp.tile(jnp.concatenate([-jnp.sin(ang), jnp.sin(ang)], axis=1), (1, LANES // HEAD_DIM))
    sel = _bias_lane_selectors()

    x2d = x.reshape(n_tok, d)
    mem2d = mem.reshape(n_batch * mem_len, d)
    for l in range(depth):
        lambda_init = 0.8 - 0.6 * math.exp(-0.3 * l)
        o_fv = 2 * fox_w
        o_fl = o_fv + fox_w
        o_dq = o_fl + FOX_HEADS
        o_dv = o_dq + 2 * diff_w
        o_ga = o_dv + diff_w
        wl = w_in[l]
        w_main = jnp.concatenate([wl[:, :o_fv], wl[:, o_dq:o_dv], wl[:, o_ga:]], axis=1).astype(BF16)
        w_vt = jnp.concatenate([wl[:, o_fv:o_fl], wl[:, o_dv:o_ga]], axis=1).T.astype(BF16)
        w_f = jnp.pad(wl[:, o_fl:o_dq], ((0, 0), (0, LANES - FOX_HEADS))).astype(BF16)
        b_row = jnp.pad(b_fgate[l], (0, LANES - FOX_HEADS))[None]

        kc, vc = _mem_kv(mem2d, g_mem[l][None], w_ckv[l].astype(BF16), n_batch, mem_len)
        fq, fk, dq, dk, ga, gb, fvt, dvt, fl = _in_proj(
            x2d, g_mix[l][None], w_main, w_vt, w_f, cos_t, sin_t, tm=512)
        bias_lanes = _fox_prep(fl, b_row, sel, n_batch, seq, tc=512)
        y_a = _attention(fq, fk, fvt, (bias_lanes,), n_batch, seq, fox=True)
        lam_vec = jnp.stack([lambda_q1[l], lambda_k1[l], lambda_q2[l], lambda_k2[l]])
        y_b = _attention(dq, dk, dvt, (lam_vec, g_diff_sub[l][:, None]), n_batch, seq,
                         fox=False, lambda_init=lambda_init)

        w_router = jnp.pad(jnp.concatenate([w_group[l], w_expert[l]], axis=1),
                           ((0, 0), (0, LANES - N_GROUPS - N_EXPERTS)))
        b_router = jnp.pad(jnp.concatenate([b_group[l], b_expert[l]]),
                           (0, LANES - N_GROUPS - N_EXPERTS))[None]
        x2d, hm, logits = _post_mixer(
            y_a, y_b, ga, gb, x2d, w_branch_a[l].astype(BF16), w_branch_b[l].astype(BF16),
            w_out[l].astype(BF16), g_cross[l][None], w_cq[l].astype(BF16), kc, vc,
            w_co[l].astype(BF16), g_ffn[l][None], w_router, b_router,
            n_batch, seq, mem_len, tm=512)

        row_tok, row_w, block_expert, n_used = _route(logits, n_tok)
        hm_pad = jnp.concatenate([hm, jnp.zeros((1, d), BF16)], axis=0)
        xs = hm_pad[row_tok]
        ys = _experts(block_expert, n_used, xs, row_w[:, None], w_exp_gate[l].astype(BF16),
                      w_exp_up[l].astype(BF16), w_exp_down[l].astype(BF16))
        moe = jnp.zeros((n_tok + 1, d), F32).at[row_tok].add(ys)[:n_tok]
        if l + 1 < depth:
            x2d = x2d + moe
    return _final(x2d, moe, g_final[None], tm=512).reshape(n_batch, seq, d)
```

```python
import functools
import math

import numpy as np
import jax
import jax.numpy as jnp
from jax import lax
from jax.experimental import pallas as pl
from jax.experimental.pallas import tpu as pltpu

HEAD_DIM = 64
LANES = 128
FOX_HEADS = 8
DIFF_HEADS = 4
CROSS_HEADS = 4
N_GROUPS = 4
EXPERTS_PER_GROUP = 8
N_EXPERTS = N_GROUPS * EXPERTS_PER_GROUP
TOP_K = 2
MOE_BLOCK = 256
ATTN_Q_BLOCK = 512
ATTN_K_BLOCK = 256
NORM_EPS = 1e-6
SUBLN_EPS = 1e-5
QK_SCALE = HEAD_DIM ** -0.5
LOG2E = math.log2(math.e)
Q_PRESCALE = QK_SCALE * LOG2E
NEG_BIG = -1e30
VMEM_LIMIT = 56 * 2**20

BF16 = jnp.bfloat16
F32 = jnp.float32


def _rms(t, g, eps):
    return t * lax.rsqrt(jnp.mean(t * t, axis=-1, keepdims=True) + eps) * g


def _dot(a, b):
    return jnp.dot(a, b, preferred_element_type=F32)


def _dot_nt(a, b):
    return lax.dot_general(a, b, (((1,), (1,)), ((), ())), preferred_element_type=F32)


def _params(*sem):
    return pltpu.CompilerParams(dimension_semantics=sem, vmem_limit_bytes=VMEM_LIMIT)


def _mem_kv_kernel(mem_ref, g_ref, w_ref, k_ref, v_ref):
    h = _rms(mem_ref[...], g_ref[...], NORM_EPS).astype(BF16)
    kv = _dot(h, w_ref[...])
    width = k_ref.shape[1]
    k_ref[...] = kv[:, :width].astype(BF16)
    v_ref[...] = kv[:, width:].astype(BF16)


def _mem_kv(mem2, g_mem, w_ckv, n_batch, mem_len):
    d = mem2.shape[1]
    cw = w_ckv.shape[1] // 2
    out = jax.ShapeDtypeStruct((n_batch * mem_len, cw), BF16)
    return pl.pallas_call(
        _mem_kv_kernel,
        grid=(n_batch,),
        in_specs=[pl.BlockSpec((mem_len, d), lambda b: (b, 0)),
                  pl.BlockSpec((1, d), lambda b: (0, 0)),
                  pl.BlockSpec((d, 2 * cw), lambda b: (0, 0))],
        out_specs=[pl.BlockSpec((mem_len, cw), lambda b: (b, 0))] * 2,
        out_shape=[out, out],
        compiler_params=_params("arbitrary"),
        name="mem_kv",
    )(mem2, g_mem, w_ckv)


def _rope(t, cos, sin_signed, first_half):
    fwd = pltpu.roll(t, LANES - HEAD_DIM // 2, axis=1)
    bwd = pltpu.roll(t, HEAD_DIM // 2, axis=1)
    return t * cos + jnp.where(first_half, fwd, bwd) * sin_signed


def _in_proj_kernel(x_ref, g_ref, w_ref, wvt_ref, wf_ref, cos_ref, sin_ref,
                    fq_ref, fk_ref, dq_ref, dk_ref, ga_ref, gb_ref, fvt_ref, dvt_ref, fl_ref):
    h = _rms(x_ref[...], g_ref[...], NORM_EPS).astype(BF16)
    tm = h.shape[0]
    cw = fq_ref.shape[1]

    def proj(chunk):
        return _dot(h, w_ref[:, chunk * cw:(chunk + 1) * cw])

    fq_ref[...] = (proj(0) * Q_PRESCALE).astype(BF16)
    fk_ref[...] = proj(1).astype(BF16)

    cos = cos_ref[...]
    sin = sin_ref[...]
    lane = lax.broadcasted_iota(jnp.int32, (tm, LANES), 1)
    first_half = (lane % HEAD_DIM) < HEAD_DIM // 2
    for out_ref, chunk, scale in ((dq_ref, 2, Q_PRESCALE), (dk_ref, 3, 1.0)):
        t = proj(chunk)
        for c in range(cw // LANES):
            blk = _rope(t[:, c * LANES:(c + 1) * LANES], cos, sin, first_half)
            out_ref[:, c * LANES:(c + 1) * LANES] = (blk * scale).astype(BF16)

    for out_ref, chunk in ((ga_ref, 4), (gb_ref, 6)):
        for c in range(2):
            out_ref[:, c * cw:(c + 1) * cw] = proj(chunk + c).astype(BF16)
    fvt_ref[...] = _dot_nt(wvt_ref[0:cw, :], h).astype(BF16)
    dvt_ref[...] = _dot_nt(wvt_ref[cw:2 * cw, :], h).astype(BF16)
    fl_ref[...] = _dot(h, wf_ref[...])


def _in_proj(x2, g_mix, w_main, w_vt, w_f, cos_t, sin_t, tm):
    n_tok, d = x2.shape
    cw = 512
    tok = lambda i: (i, 0)
    tok_t = lambda i: (0, i)
    const = lambda i: (0, 0)
    o512 = jax.ShapeDtypeStruct((n_tok, cw), BF16)
    o1024 = jax.ShapeDtypeStruct((n_tok, 2 * cw), BF16)
    o512_t = jax.ShapeDtypeStruct((cw, n_tok), BF16)
    return pl.pallas_call(
        _in_proj_kernel,
        grid=(n_tok // tm,),
        in_specs=[pl.BlockSpec((tm, d), tok),
                  pl.BlockSpec((1, d), const),
                  pl.BlockSpec(w_main.shape, const, pipeline_mode=pl.Buffered(1)),
                  pl.BlockSpec(w_vt.shape, const, pipeline_mode=pl.Buffered(1)),
                  pl.BlockSpec(w_f.shape, const),
                  pl.BlockSpec((tm, LANES), tok),
                  pl.BlockSpec((tm, LANES), tok)],
        out_specs=[pl.BlockSpec((tm, cw), tok)] * 4
                  + [pl.BlockSpec((tm, 2 * cw), tok)] * 2
                  + [pl.BlockSpec((cw, tm), tok_t)] * 2
                  + [pl.BlockSpec((tm, LANES), tok)],
        out_shape=[o512] * 4 + [o1024] * 2 + [o512_t] * 2
                  + [jax.ShapeDtypeStruct((n_tok, LANES), F32)],
        compiler_params=_params("arbitrary"),
        name="in_proj",
    )(x2, g_mix, w_main, w_vt, w_f, cos_t, sin_t)


def _bias_lane_selectors():
    sel = np.zeros((3, LANES, FOX_HEADS // 2 * LANES), np.float32)
    for h in range(FOX_HEADS):
        base = (h // 2) * LANES + (HEAD_DIM if h % 2 == 0 else 0)
        for piece in range(3):
            sel[piece, h, base + piece] = 1.0
    return jnp.asarray(sel, BF16)


def _split3(t):
    pieces = []
    for _ in range(3):
        part = t.astype(BF16)
        t = t - part.astype(F32)
        pieces.append(part)
    return pieces


def _fox_prep_kernel(fl_ref, b_ref, tri_ref, sel_ref, out_ref, carry_ref):
    @pl.when(pl.program_id(1) == 0)
    def _():
        carry_ref[...] = jnp.zeros_like(carry_ref)

    z = fl_ref[...] + b_ref[...]
    log_f = jnp.minimum(z, 0.0) - jnp.log(1.0 + jnp.exp(-jnp.abs(z)))
    tc = z.shape[0]
    c = carry_ref[...]
    for part in _split3(log_f):
        c = c + _dot(tri_ref[...], part)
    carry_ref[...] = c[tc - 1:tc, :]
    acc = jnp.zeros(out_ref.shape, F32)
    for piece, part in enumerate(_split3(c * (-LOG2E))):
        acc = acc + _dot(part, sel_ref[piece])
    out_ref[...] = acc.astype(BF16)


def _fox_prep(fl, b_row, sel, n_batch, seq, tc):
    n_tok = fl.shape[0]
    nblk = seq // tc
    width = sel.shape[2]
    tri = jnp.tril(jnp.ones((tc, tc), BF16))
    return pl.pallas_call(
        _fox_prep_kernel,
        grid=(n_batch, nblk),
        in_specs=[pl.BlockSpec((tc, LANES), lambda b, j: (b * nblk + j, 0)),
                  pl.BlockSpec((1, LANES), lambda b, j: (0, 0)),
                  pl.BlockSpec((tc, tc), lambda b, j: (0, 0)),
                  pl.BlockSpec(sel.shape, lambda b, j: (0, 0, 0))],
        out_specs=pl.BlockSpec((tc, width), lambda b, j: (b * nblk + j, 0)),
        out_shape=jax.ShapeDtypeStruct((n_tok, width), BF16),
        scratch_shapes=[pltpu.VMEM((1, LANES), F32)],
        compiler_params=_params("arbitrary", "arbitrary"),
        name="fox_prep",
    )(fl, b_row, tri, sel)


def _attn_kernel(*refs, bq, bk, fox, lambda_init):
    if fox:
        q_ref, k_ref, vt_ref, bl_ref, out_ref, qs_ref, st_ref, p_ref, acc_ref, stat_ref, ka_ref, kb_ref = refs
    else:
        q_ref, k_ref, vt_ref, lam_ref, gsub_ref, out_ref, qs_ref, st_ref, p_ref, acc_ref, stat_ref = refs
    qi = pl.program_id(2)

    q = q_ref[...].astype(F32)
    lane = lax.broadcasted_iota(jnp.int32, q.shape, 1)
    if fox:
        @pl.when(qi == 0)
        def _():
            k = k_ref[...].astype(F32)
            bl = bl_ref[...].astype(F32)
            klane = lax.broadcasted_iota(jnp.int32, k.shape, 1)
            ka_ref[...] = jnp.where(klane < HEAD_DIM, k, bl).astype(BF16)
            kb_ref[...] = jnp.where(klane >= HEAD_DIM, k, bl).astype(BF16)

        q_streams = (jnp.where(lane < HEAD_DIM, q, jnp.where(lane < HEAD_DIM + 3, 1.0, 0.0)),
                     jnp.where(lane >= HEAD_DIM, q, jnp.where(lane < 3, 1.0, 0.0)))
        k_srcs = (ka_ref, kb_ref)
        v_rows = ((0, HEAD_DIM), (HEAD_DIM, 2 * HEAD_DIM))
    else:
        q_streams = (jnp.where(lane < HEAD_DIM, q, 0.0), jnp.where(lane >= HEAD_DIM, q, 0.0))
        k_srcs = (k_ref, k_ref)
        v_rows = ((0, 2 * HEAD_DIM), (0, 2 * HEAD_DIM))
    n_v = v_rows[0][1] - v_rows[0][0]
    for idx in range(2):
        qs_ref[idx] = q_streams[idx].astype(BF16)
        p_ref[idx] = jnp.zeros((bk, bq), BF16)
        acc_ref[idx] = jnp.zeros((n_v + 16, bq), F32)
    ones_rows = jnp.ones((16, bk), BF16)
    diag_blocks = bq // bk
    first_diag = qi * diag_blocks

    def qk_stage(kj, idx):
        k0 = pl.multiple_of(kj * bk, bk)
        st_ref[idx] = _dot_nt(k_srcs[idx][pl.ds(k0, bk), :], qs_ref[idx])

    M_ROW, A_ROW = 0, 8

    def pv_stage(kj, idx):
        k0 = pl.multiple_of(kj * bk, bk)
        r0, r1 = v_rows[idx]
        v_aug = jnp.concatenate([vt_ref[r0:r1, pl.ds(k0, bk)], ones_rows], axis=0)
        alpha = stat_ref[idx, A_ROW:A_ROW + 1, :]
        acc_ref[idx] = alpha * acc_ref[idx] + _dot(v_aug, p_ref[idx])

    def softmax_stage(idx, diag):
        st = st_ref[idx]
        if diag is not None:
            key_idx = lax.broadcasted_iota(jnp.int32, (bk, bq), 0) + diag * bk
            qry_idx = lax.broadcasted_iota(jnp.int32, (bk, bq), 1)
            st = jnp.where(key_idx <= qry_idx, st, NEG_BIG)
        m = stat_ref[idx, M_ROW:M_ROW + 1, :]
        m_new = jnp.maximum(m, jnp.max(st, axis=0, keepdims=True))
        p_ref[idx] = jnp.exp2(st - m_new).astype(BF16)
        stat_ref[idx, M_ROW:M_ROW + 1, :] = m_new
        stat_ref[idx, A_ROW:A_ROW + 1, :] = jnp.exp2(m - m_new)

    for idx in range(2):
        stat_ref[idx, M_ROW:M_ROW + 8, :] = jnp.full((8, bq), NEG_BIG, F32)
        stat_ref[idx, A_ROW:A_ROW + 8, :] = jnp.ones((8, bq), F32)
        qk_stage(0, idx)

    def body(j, carry):
        for idx in range(2):
            pv_stage(jnp.maximum(j - 1, 0), idx)
            softmax_stage(idx, None)
            qk_stage(j + 1, idx)
        return carry

    lax.fori_loop(0, first_diag, body, 0)
    for diag in range(diag_blocks):
        for idx in range(2):
            pv_stage(jnp.maximum(first_diag + diag - 1, 0), idx)
            softmax_stage(idx, diag)
            if diag + 1 < diag_blocks:
                qk_stage(first_diag + diag + 1, idx)
    outs = []
    for idx in range(2):
        pv_stage(first_diag + diag_blocks - 1, idx)
        acc = acc_ref[idx]
        outs.append(acc[0:n_v] * (1.0 / acc[n_v:n_v + 1]))
    o_a, o_b = outs
    if fox:
        y = jnp.concatenate([o_a, o_b], axis=0)
    else:
        lam_vec = lam_ref[...]
        lam = (jnp.exp(jnp.sum(lam_vec[0:1] * lam_vec[1:2], axis=1, keepdims=True))
               - jnp.exp(jnp.sum(lam_vec[2:3] * lam_vec[3:4], axis=1, keepdims=True))
               + lambda_init)
        y = o_a - lam * o_b
        y = y * lax.rsqrt(jnp.mean(y * y, axis=0, keepdims=True) + SUBLN_EPS)
        y = y * (gsub_ref[...] * (1.0 - lambda_init))
    out_ref[...] = y.T.astype(out_ref.dtype)


def _attention(q, k, vt, extra, n_batch, seq, fox, lambda_init=0.0):
    n_tok, width = q.shape
    n_pairs = width // LANES
    bq, bk = ATTN_Q_BLOCK, ATTN_K_BLOCK
    nq = seq // bq
    n_v = HEAD_DIM if fox else 2 * HEAD_DIM
    q_spec = pl.BlockSpec((bq, LANES), lambda b, hp, qi: (b * nq + qi, hp))
    k_spec = pl.BlockSpec((seq, LANES), lambda b, hp, qi: (b, hp))
    vt_spec = pl.BlockSpec((LANES, seq), lambda b, hp, qi: (hp, b))
    scratch = [pltpu.VMEM((2, bq, LANES), BF16),
               pltpu.VMEM((2, bk, bq), F32),
               pltpu.VMEM((2, bk, bq), BF16),
               pltpu.VMEM((2, n_v + 16, bq), F32),
               pltpu.VMEM((2, 16, bq), F32)]
    if fox:
        extra_specs = [k_spec]
        scratch += [pltpu.VMEM((seq, LANES), BF16)] * 2
    else:
        lam_vec, g_sub = extra
        extra_specs = [pl.BlockSpec(lam_vec.shape, lambda b, hp, qi: (0, 0)),
                       pl.BlockSpec(g_sub.shape, lambda b, hp, qi: (0, 0))]
    return pl.pallas_call(
        functools.partial(_attn_kernel, bq=bq, bk=bk, fox=fox, lambda_init=lambda_init),
        grid=(n_batch, n_pairs, nq),
        in_specs=[q_spec, k_spec, vt_spec] + extra_specs,
        out_specs=q_spec,
        out_shape=jax.ShapeDtypeStruct((n_tok, width), BF16),
        scratch_shapes=scratch,
        compiler_params=_params("arbitrary", "arbitrary", "arbitrary"),
        name="fox_attn" if fox else "diff_attn",
    )(q, k, vt, *extra)


def _post_mixer_kernel(ya_ref, yb_ref, ga_ref, gb_ref, x_ref, wa_ref, wb_ref, wo_ref,
                       gc_ref, wcq_ref, kc_ref, vc_ref, wco_ref, gf_ref, wr_ref, br_ref,
                       x2_ref, hm_ref, lg_ref):
    def sigmoid(t):
        return 1.0 / (1.0 + jnp.exp(-t))

    merged = (sigmoid(ga_ref[...].astype(F32)) * _dot(ya_ref[...], wa_ref[...])
              + sigmoid(gb_ref[...].astype(F32)) * _dot(yb_ref[...], wb_ref[...]))
    x1 = x_ref[...] + _dot(merged.astype(BF16), wo_ref[...])

    hx = _rms(x1, gc_ref[...], NORM_EPS).astype(BF16)
    qc = (_dot(hx, wcq_ref[...]) * QK_SCALE).astype(BF16)
    kc = kc_ref[...]
    vc = vc_ref[...]
    lane = lax.broadcasted_iota(jnp.int32, qc.shape, 1)
    zero = jnp.zeros_like(qc)
    o = jnp.zeros(qc.shape, F32)
    for h in range(CROSS_HEADS):
        in_head = (lane >= h * HEAD_DIM) & (lane < (h + 1) * HEAD_DIM)
        s = _dot_nt(jnp.where(in_head, qc, zero), kc)
        p = jnp.exp(s - jnp.max(s, axis=1, keepdims=True))
        p = p * (1.0 / jnp.sum(p, axis=1, keepdims=True))
        o = jnp.where(in_head, _dot(p.astype(BF16), vc), o)
    x2 = x1 + _dot(o.astype(BF16), wco_ref[...])
    x2_ref[...] = x2

    hm = _rms(x2, gf_ref[...], NORM_EPS)
    hm_ref[...] = hm.astype(BF16)
    hm_hi = hm.astype(BF16)
    hm_lo = (hm - hm_hi.astype(F32)).astype(BF16)
    lg_ref[...] = (_dot(hm_hi, wr_ref[0]) + _dot(hm_hi, wr_ref[1]) + _dot(hm_lo, wr_ref[0])
                   + br_ref[...])


def _post_mixer(ya, yb, ga, gb, x2d, wa, wb, wo, g_cross, wcq, kc, vc, wco, g_ffn, wr, br,
                n_batch, seq, mem_len, tm):
    n_tok, d = x2d.shape
    nblk = seq // tm
    tok = lambda b, j: (b * nblk + j, 0)
    const = lambda b, j: (0, 0)
    full = lambda a: pl.BlockSpec(a.shape, const)
    return pl.pallas_call(
        _post_mixer_kernel,
        grid=(n_batch, nblk),
        in_specs=[pl.BlockSpec((tm, ya.shape[1]), tok), pl.BlockSpec((tm, yb.shape[1]), tok),
                  pl.BlockSpec((tm, d), tok), pl.BlockSpec((tm, d), tok), pl.BlockSpec((tm, d), tok),
                  full(wa), full(wb), full(wo), full(g_cross), full(wcq),
                  pl.BlockSpec((mem_len, kc.shape[1]), lambda b, j: (b, 0)),
                  pl.BlockSpec((mem_len, vc.shape[1]), lambda b, j: (b, 0)),
                  full(wco), full(g_ffn), pl.BlockSpec(wr.shape, lambda b, j: (0, 0, 0)), full(br)],
        out_specs=[pl.BlockSpec((tm, d), tok), pl.BlockSpec((tm, d), tok),
                   pl.BlockSpec((tm, LANES), tok)],
        out_shape=[jax.ShapeDtypeStruct((n_tok, d), F32),
                   jax.ShapeDtypeStruct((n_tok, d), BF16),
                   jax.ShapeDtypeStruct((n_tok, LANES), F32)],
        compiler_params=_params("arbitrary", "arbitrary"),
        name="post_mixer",
    )(ya, yb, ga, gb, x2d, wa, wb, wo, g_cross, wcq, kc, vc, wco, g_ffn, wr, br)


def _expert_kernel(be_ref, nused_ref, xs_ref, wg_ref, wu_ref, wd_ref, ys_ref):
    i = pl.program_id(0)

    @pl.when(i < nused_ref[0])
    def _():
        xb = xs_ref[...]
        g = _dot(xb, wg_ref[0])
        u = _dot(xb, wu_ref[0])
        hb = (g * (1.0 / (1.0 + jnp.exp(-g))) * u).astype(BF16)
        ys_ref[...] = _dot(hb, wd_ref[0]).astype(ys_ref.dtype)

    @pl.when(i >= nused_ref[0])
    def _():
        ys_ref[...] = jnp.zeros_like(ys_ref)


def _experts(block_expert, n_used, xs, wg, wu, wd):
    n_rows, d = xs.shape
    n_blocks = n_rows // MOE_BLOCK
    ff = wg.shape[2]
    grid_spec = pltpu.PrefetchScalarGridSpec(
        num_scalar_prefetch=2,
        grid=(n_blocks,),
        in_specs=[pl.BlockSpec((MOE_BLOCK, d), lambda i, be, nu: (i, 0)),
                  pl.BlockSpec((1, d, ff), lambda i, be, nu: (be[i], 0, 0)),
                  pl.BlockSpec((1, d, ff), lambda i, be, nu: (be[i], 0, 0)),
                  pl.BlockSpec((1, ff, d), lambda i, be, nu: (be[i], 0, 0))],
        out_specs=pl.BlockSpec((MOE_BLOCK, d), lambda i, be, nu: (i, 0)),
    )
    return pl.pallas_call(
        _expert_kernel,
        grid_spec=grid_spec,
        out_shape=jax.ShapeDtypeStruct((n_rows, d), BF16),
        compiler_params=_params("arbitrary"),
        name="experts",
    )(block_expert, n_used, xs, wg, wu, wd)


def _final_kernel(x_ref, y_ref, w_ref, g_ref, out_ref, *, normalize):
    d = x_ref.shape[1]
    w = w_ref[...]
    moe = w[:, 0:1] * y_ref[:, 0:d].astype(F32) + w[:, 1:2] * y_ref[:, d:2 * d].astype(F32)
    out = x_ref[...] + moe
    out_ref[...] = _rms(out, g_ref[...], NORM_EPS) if normalize else out


def _final(x2, y_pairs, weights, g_final, tm, normalize):
    n_tok, d = x2.shape
    tok = lambda i: (i, 0)
    return pl.pallas_call(
        functools.partial(_final_kernel, normalize=normalize),
        grid=(n_tok // tm,),
        in_specs=[pl.BlockSpec((tm, d), tok), pl.BlockSpec((tm, TOP_K * d), tok),
                  pl.BlockSpec((tm, TOP_K), tok), pl.BlockSpec((1, d), lambda i: (0, 0))],
        out_specs=pl.BlockSpec((tm, d), tok),
        out_shape=jax.ShapeDtypeStruct((n_tok, d), F32),
        compiler_params=_params("arbitrary"),
        name="final_norm",
    )(x2, y_pairs, weights, g_final)


def _route(logits, n_tok):
    group_logits = logits[:, :N_GROUPS]
    group_prob = jax.nn.softmax(group_logits, axis=-1)
    g_sel = jnp.argmax(group_logits, axis=-1)
    g_w = jnp.take_along_axis(group_prob, g_sel[:, None], axis=-1)
    exp_logits = logits[:, N_GROUPS:N_GROUPS + N_EXPERTS].reshape(n_tok, N_GROUPS, EXPERTS_PER_GROUP)
    sel_logits = jnp.take_along_axis(exp_logits, g_sel[:, None, None], axis=1)[:, 0]
    top_vals, top_idx = lax.top_k(sel_logits, TOP_K)
    weights = jax.nn.softmax(top_vals, axis=-1) * g_w
    expert_ids = g_sel[:, None].astype(jnp.int32) * EXPERTS_PER_GROUP + top_idx.astype(jnp.int32)

    n_assign = n_tok * TOP_K
    flat_e = expert_ids.reshape(n_assign)
    experts = jnp.arange(N_EXPERTS, dtype=jnp.int32)

    def lookup(table, idx):
        return jnp.sum(jnp.where(idx[:, None] == experts[None, :], table[None, :], 0), axis=1)

    order = jnp.argsort(flat_e).astype(jnp.int32)
    sorted_pos = jnp.argsort(order).astype(jnp.int32)
    seg_end = jnp.searchsorted(flat_e[order], experts, side='right').astype(jnp.int32)
    counts = seg_end - jnp.concatenate([jnp.zeros((1,), jnp.int32), seg_end[:-1]])
    seg_start = seg_end - counts
    padded = (counts + MOE_BLOCK - 1) // MOE_BLOCK * MOE_BLOCK
    pad_end = jnp.cumsum(padded)
    pad_start = pad_end - padded
    dest = sorted_pos + lookup(pad_start - seg_start, flat_e)

    n_blocks = -(-n_assign // MOE_BLOCK) + N_EXPERTS
    n_rows = n_blocks * MOE_BLOCK
    block_start = jnp.arange(n_blocks, dtype=jnp.int32) * MOE_BLOCK
    block_expert = jnp.minimum(jnp.searchsorted(pad_end, block_start, side='right'),
                               N_EXPERTS - 1).astype(jnp.int32)
    row_e = jnp.repeat(block_expert, MOE_BLOCK)
    within = jnp.arange(n_rows, dtype=jnp.int32) - lookup(pad_start, row_e)
    valid = within < lookup(counts, row_e)
    src = jnp.clip(lookup(seg_start, row_e) + within, 0, n_assign - 1)
    row_tok = jnp.where(valid, order[src] // TOP_K, n_tok)
    n_used = (pad_end[-1] // MOE_BLOCK).astype(jnp.int32).reshape(1)
    return row_tok, dest, weights, block_expert, n_used


def kernel(x, mem, positions, g_mix, w_in, b_fgate, w_branch_a, w_branch_b, w_out, lambda_q1, lambda_k1, lambda_q2, lambda_k2, g_diff_sub, g_cross, g_mem, w_cq, w_ckv, w_co, g_ffn, w_group, b_group, w_expert, b_expert, w_exp_gate, w_exp_up, w_exp_down, g_final):
    n_batch, seq, d = x.shape
    mem_len = mem.shape[1]
    depth = g_mix.shape[0]
    n_tok = n_batch * seq
    fox_w = FOX_HEADS * HEAD_DIM
    diff_w = DIFF_HEADS * 2 * HEAD_DIM

    half = HEAD_DIM // 2
    inv_freq = 10000.0 ** (-jnp.arange(half, dtype=F32) * 2.0 / HEAD_DIM)
    ang = positions.astype(F32).reshape(n_tok, 1) * inv_freq
    cos_t = jnp.tile(jnp.cos(ang), (1, LANES // half))
    sin_t = jnp.tile(jnp.concatenate([-jnp.sin(ang), jnp.sin(ang)], axis=1), (1, LANES // HEAD_DIM))
    sel = _bias_lane_selectors()

    x2d = x.reshape(n_tok, d)
    mem2d = mem.reshape(n_batch * mem_len, d)
    for l in range(depth):
        lambda_init = 0.8 - 0.6 * math.exp(-0.3 * l)
        o_fv = 2 * fox_w
        o_fl = o_fv + fox_w
        o_dq = o_fl + FOX_HEADS
        o_dv = o_dq + 2 * diff_w
        o_ga = o_dv + diff_w
        wl = w_in[l]
        w_main = jnp.concatenate([wl[:, :o_fv], wl[:, o_dq:o_dv], wl[:, o_ga:]], axis=1).astype(BF16)
        w_vt = jnp.concatenate([wl[:, o_fv:o_fl], wl[:, o_dv:o_ga]], axis=1).T.astype(BF16)
        w_f = jnp.pad(wl[:, o_fl:o_dq], ((0, 0), (0, LANES - FOX_HEADS))).astype(BF16)
        b_row = jnp.pad(b_fgate[l], (0, LANES - FOX_HEADS))[None]

        kc, vc = _mem_kv(mem2d, g_mem[l][None], w_ckv[l].astype(BF16), n_batch, mem_len)
        fq, fk, dq, dk, ga, gb, fvt, dvt, fl = _in_proj(
            x2d, g_mix[l][None], w_main, w_vt, w_f, cos_t, sin_t, tm=512)
        bias_lanes = _fox_prep(fl, b_row, sel, n_batch, seq, tc=512)
        y_a = _attention(fq, fk, fvt, (bias_lanes,), n_batch, seq, fox=True)
        lam_vec = jnp.stack([lambda_q1[l], lambda_k1[l], lambda_q2[l], lambda_k2[l]])
        y_b = _attention(dq, dk, dvt, (lam_vec, g_diff_sub[l][:, None]), n_batch, seq,
                         fox=False, lambda_init=lambda_init)

        w_router = jnp.pad(jnp.concatenate([w_group[l], w_expert[l]], axis=1),
                           ((0, 0), (0, LANES - N_GROUPS - N_EXPERTS)))
        w_router_hi = w_router.astype(BF16)
        w_router = jnp.stack([w_router_hi, (w_router - w_router_hi.astype(F32)).astype(BF16)])
        b_router = jnp.pad(jnp.concatenate([b_group[l], b_expert[l]]),
                           (0, LANES - N_GROUPS - N_EXPERTS))[None]
        x2d, hm, logits = _post_mixer(
            y_a, y_b, ga, gb, x2d, w_branch_a[l].astype(BF16), w_branch_b[l].astype(BF16),
            w_out[l].astype(BF16), g_cross[l][None], w_cq[l].astype(BF16), kc, vc,
            w_co[l].astype(BF16), g_ffn[l][None], w_router, b_router,
            n_batch, seq, mem_len, tm=512)

        row_tok, dest, weights, block_expert, n_used = _route(logits, n_tok)
        hm_pad = jnp.concatenate([hm, jnp.zeros((1, d), BF16)], axis=0)
        xs = hm_pad[row_tok]
        ys = _experts(block_expert, n_used, xs, w_exp_gate[l].astype(BF16),
                      w_exp_up[l].astype(BF16), w_exp_down[l].astype(BF16))
        y_pairs = ys[dest].reshape(n_tok, TOP_K * d)
        last = l + 1 == depth
        x2d = _final(x2d, y_pairs, weights, g_final[None], tm=512, normalize=last)
    return x2d.reshape(n_batch, seq, d)
```

```python
import functools
import math

import numpy as np
import jax
import jax.numpy as jnp
from jax import lax
from jax.experimental import pallas as pl
from jax.experimental.pallas import tpu as pltpu

HEAD_DIM = 64
LANES = 128
FOX_HEADS = 8
DIFF_HEADS = 4
CROSS_HEADS = 4
N_GROUPS = 4
EXPERTS_PER_GROUP = 8
N_EXPERTS = N_GROUPS * EXPERTS_PER_GROUP
TOP_K = 2
MOE_BLOCK = 256
ATTN_Q_BLOCK = 512
ATTN_K_BLOCK = 256
NORM_EPS = 1e-6
SUBLN_EPS = 1e-5
QK_SCALE = HEAD_DIM ** -0.5
LOG2E = math.log2(math.e)
Q_PRESCALE = QK_SCALE * LOG2E
NEG_BIG = -1e30
VMEM_LIMIT = 56 * 2**20

BF16 = jnp.bfloat16
F32 = jnp.float32


def _rms(t, g, eps):
    return t * lax.rsqrt(jnp.mean(t * t, axis=-1, keepdims=True) + eps) * g


def _dot(a, b):
    return jnp.dot(a, b, preferred_element_type=F32)


def _dot_nt(a, b):
    return lax.dot_general(a, b, (((1,), (1,)), ((), ())), preferred_element_type=F32)


def _params(*sem):
    return pltpu.CompilerParams(dimension_semantics=sem, vmem_limit_bytes=VMEM_LIMIT)


def _mem_kv_kernel(mem_ref, g_ref, w_ref, k_ref, v_ref):
    h = _rms(mem_ref[...], g_ref[...], NORM_EPS).astype(BF16)
    kv = _dot(h, w_ref[...])
    width = k_ref.shape[1]
    k_ref[...] = kv[:, :width].astype(BF16)
    v_ref[...] = kv[:, width:].astype(BF16)


def _mem_kv(mem2, g_mem, w_ckv, n_batch, mem_len):
    d = mem2.shape[1]
    cw = w_ckv.shape[1] // 2
    out = jax.ShapeDtypeStruct((n_batch * mem_len, cw), BF16)
    return pl.pallas_call(
        _mem_kv_kernel,
        grid=(n_batch,),
        in_specs=[pl.BlockSpec((mem_len, d), lambda b: (b, 0)),
                  pl.BlockSpec((1, d), lambda b: (0, 0)),
                  pl.BlockSpec((d, 2 * cw), lambda b: (0, 0))],
        out_specs=[pl.BlockSpec((mem_len, cw), lambda b: (b, 0))] * 2,
        out_shape=[out, out],
        compiler_params=_params("arbitrary"),
        name="mem_kv",
    )(mem2, g_mem, w_ckv)


def _rope(t, cos, sin_signed, first_half):
    fwd = pltpu.roll(t, LANES - HEAD_DIM // 2, axis=1)
    bwd = pltpu.roll(t, HEAD_DIM // 2, axis=1)
    return t * cos + jnp.where(first_half, fwd, bwd) * sin_signed


def _in_proj_kernel(x_ref, g_ref, w_ref, wvt_ref, wf_ref, cos_ref, sin_ref,
                    fq_ref, fk_ref, dq_ref, dk_ref, ga_ref, gb_ref, fvt_ref, dvt_ref, fl_ref):
    h = _rms(x_ref[...], g_ref[...], NORM_EPS).astype(BF16)
    tm = h.shape[0]
    cw = fq_ref.shape[1]

    def proj(chunk):
        return _dot(h, w_ref[:, chunk * cw:(chunk + 1) * cw])

    fq_ref[...] = (proj(0) * Q_PRESCALE).astype(BF16)
    fk_ref[...] = proj(1).astype(BF16)

    cos = cos_ref[...]
    sin = sin_ref[...]
    lane = lax.broadcasted_iota(jnp.int32, (tm, LANES), 1)
    first_half = (lane % HEAD_DIM) < HEAD_DIM // 2
    for out_ref, chunk, scale in ((dq_ref, 2, Q_PRESCALE), (dk_ref, 3, 1.0)):
        t = proj(chunk)
        for c in range(cw // LANES):
            blk = _rope(t[:, c * LANES:(c + 1) * LANES], cos, sin, first_half)
            out_ref[:, c * LANES:(c + 1) * LANES] = (blk * scale).astype(BF16)

    for out_ref, chunk in ((ga_ref, 4), (gb_ref, 6)):
        for c in range(2):
            out_ref[:, c * cw:(c + 1) * cw] = proj(chunk + c).astype(BF16)
    fvt_ref[...] = _dot_nt(wvt_ref[0:cw, :], h).astype(BF16)
    dvt_ref[...] = _dot_nt(wvt_ref[cw:2 * cw, :], h).astype(BF16)
    fl_ref[...] = _dot(h, wf_ref[...])


def _in_proj(x2, g_mix, w_main, w_vt, w_f, cos_t, sin_t, tm):
    n_tok, d = x2.shape
    cw = 512
    tok = lambda i: (i, 0)
    tok_t = lambda i: (0, i)
    const = lambda i: (0, 0)
    o512 = jax.ShapeDtypeStruct((n_tok, cw), BF16)
    o1024 = jax.ShapeDtypeStruct((n_tok, 2 * cw), BF16)
    o512_t = jax.ShapeDtypeStruct((cw, n_tok), BF16)
    return pl.pallas_call(
        _in_proj_kernel,
        grid=(n_tok // tm,),
        in_specs=[pl.BlockSpec((tm, d), tok),
                  pl.BlockSpec((1, d), const),
                  pl.BlockSpec(w_main.shape, const, pipeline_mode=pl.Buffered(1)),
                  pl.BlockSpec(w_vt.shape, const, pipeline_mode=pl.Buffered(1)),
                  pl.BlockSpec(w_f.shape, const),
                  pl.BlockSpec((tm, LANES), tok),
                  pl.BlockSpec((tm, LANES), tok)],
        out_specs=[pl.BlockSpec((tm, cw), tok)] * 4
                  + [pl.BlockSpec((tm, 2 * cw), tok)] * 2
                  + [pl.BlockSpec((cw, tm), tok_t)] * 2
                  + [pl.BlockSpec((tm, LANES), tok)],
        out_shape=[o512] * 4 + [o1024] * 2 + [o512_t] * 2
                  + [jax.ShapeDtypeStruct((n_tok, LANES), F32)],
        compiler_params=_params("arbitrary"),
        name="in_proj",
    )(x2, g_mix, w_main, w_vt, w_f, cos_t, sin_t)


def _bias_lane_selectors():
    sel = np.zeros((3, LANES, FOX_HEADS // 2 * LANES), np.float32)
    for h in range(FOX_HEADS):
        base = (h // 2) * LANES + (HEAD_DIM if h % 2 == 0 else 0)
        for piece in range(3):
            sel[piece, h, base + piece] = 1.0
    return jnp.asarray(sel, BF16)


def _split3(t):
    pieces = []
    for _ in range(3):
        part = t.astype(BF16)
        t = t - part.astype(F32)
        pieces.append(part)
    return pieces


def _fox_prep_kernel(fl_ref, b_ref, tri_ref, sel_ref, out_ref, carry_ref):
    @pl.when(pl.program_id(1) == 0)
    def _():
        carry_ref[...] = jnp.zeros_like(carry_ref)

    z = fl_ref[...] + b_ref[...]
    log_f = jnp.minimum(z, 0.0) - jnp.log(1.0 + jnp.exp(-jnp.abs(z)))
    tc = z.shape[0]
    c = carry_ref[...]
    for part in _split3(log_f):
        c = c + _dot(tri_ref[...], part)
    carry_ref[...] = c[tc - 1:tc, :]
    acc = jnp.zeros(out_ref.shape, F32)
    for piece, part in enumerate(_split3(c * (-LOG2E))):
        acc = acc + _dot(part, sel_ref[piece])
    out_ref[...] = acc.astype(BF16)


def _fox_prep(fl, b_row, sel, n_batch, seq, tc):
    n_tok = fl.shape[0]
    nblk = seq // tc
    width = sel.shape[2]
    tri = jnp.tril(jnp.ones((tc, tc), BF16))
    return pl.pallas_call(
        _fox_prep_kernel,
        grid=(n_batch, nblk),
        in_specs=[pl.BlockSpec((tc, LANES), lambda b, j: (b * nblk + j, 0)),
                  pl.BlockSpec((1, LANES), lambda b, j: (0, 0)),
                  pl.BlockSpec((tc, tc), lambda b, j: (0, 0)),
                  pl.BlockSpec(sel.shape, lambda b, j: (0, 0, 0))],
        out_specs=pl.BlockSpec((tc, width), lambda b, j: (b * nblk + j, 0)),
        out_shape=jax.ShapeDtypeStruct((n_tok, width), BF16),
        scratch_shapes=[pltpu.VMEM((1, LANES), F32)],
        compiler_params=_params("arbitrary", "arbitrary"),
        name="fox_prep",
    )(fl, b_row, tri, sel)


def _attn_kernel(*refs, bq, bk, fox, lambda_init):
    if fox:
        q_ref, k_ref, vt_ref, bl_ref, out_ref, qs_ref, st_ref, p_ref, acc_ref, stat_ref, ka_ref, kb_ref = refs
    else:
        q_ref, k_ref, vt_ref, lam_ref, gsub_ref, out_ref, qs_ref, st_ref, p_ref, acc_ref, stat_ref = refs
    qi = pl.program_id(2)

    q = q_ref[...].astype(F32)
    lane = lax.broadcasted_iota(jnp.int32, q.shape, 1)
    if fox:
        @pl.when(qi == 0)
        def _():
            k = k_ref[...].astype(F32)
            bl = bl_ref[...].astype(F32)
            klane = lax.broadcasted_iota(jnp.int32, k.shape, 1)
            ka_ref[...] = jnp.where(klane < HEAD_DIM, k, bl).astype(BF16)
            kb_ref[...] = jnp.where(klane >= HEAD_DIM, k, bl).astype(BF16)

        q_streams = (jnp.where(lane < HEAD_DIM, q, jnp.where(lane < HEAD_DIM + 3, 1.0, 0.0)),
                     jnp.where(lane >= HEAD_DIM, q, jnp.where(lane < 3, 1.0, 0.0)))
        k_srcs = (ka_ref, kb_ref)
        v_rows = ((0, HEAD_DIM), (HEAD_DIM, 2 * HEAD_DIM))
    else:
        q_streams = (jnp.where(lane < HEAD_DIM, q, 0.0), jnp.where(lane >= HEAD_DIM, q, 0.0))
        k_srcs = (k_ref, k_ref)
        v_rows = ((0, 2 * HEAD_DIM), (0, 2 * HEAD_DIM))
    n_v = v_rows[0][1] - v_rows[0][0]
    assert bq == 2 * bk
    for idx in range(2):
        qs_ref[idx] = q_streams[idx].astype(BF16)
        p_ref[idx, 1] = jnp.zeros((bk, bq), BF16)
        acc_ref[idx] = jnp.zeros((n_v + 16, bq), F32)
    ones_rows = jnp.ones((16, bk), BF16)
    first_diag = 2 * qi

    def qk_stage(kj, idx, slot):
        k0 = pl.multiple_of(kj * bk, bk)
        st_ref[idx, slot] = _dot_nt(k_srcs[idx][pl.ds(k0, bk), :], qs_ref[idx])

    M_ROW, A_ROW = 0, 8

    def pv_stage(kj, idx, slot):
        k0 = pl.multiple_of(kj * bk, bk)
        r0, r1 = v_rows[idx]
        v_aug = jnp.concatenate([vt_ref[r0:r1, pl.ds(k0, bk)], ones_rows], axis=0)
        alpha = stat_ref[idx, A_ROW + 8 * slot:A_ROW + 8 * slot + 1, :]
        acc_ref[idx] = alpha * acc_ref[idx] + _dot(v_aug, p_ref[idx, slot])

    def softmax_stage(idx, slot, diag):
        st = st_ref[idx, slot]
        if diag is not None:
            key_idx = lax.broadcasted_iota(jnp.int32, (bk, bq), 0) + diag * bk
            qry_idx = lax.broadcasted_iota(jnp.int32, (bk, bq), 1)
            st = jnp.where(key_idx <= qry_idx, st, NEG_BIG)
        m = stat_ref[idx, M_ROW:M_ROW + 1, :]
        m_new = jnp.maximum(m, jnp.max(st, axis=0, keepdims=True))
        p_ref[idx, slot] = jnp.exp2(st - m_new).astype(BF16)
        stat_ref[idx, M_ROW:M_ROW + 1, :] = m_new
        stat_ref[idx, A_ROW + 8 * slot:A_ROW + 8 * slot + 1, :] = jnp.exp2(m - m_new)

    for idx in range(2):
        stat_ref[idx, M_ROW:M_ROW + 8, :] = jnp.full((8, bq), NEG_BIG, F32)
        stat_ref[idx, A_ROW:A_ROW + 16, :] = jnp.ones((16, bq), F32)
        qk_stage(0, idx, 0)

    def block_pair(j, diag_pair):
        for slot in range(2):
            for idx in range(2):
                pv_stage(jnp.maximum(j + slot - 1, 0), idx, 1 - slot)
                softmax_stage(idx, slot, slot if diag_pair else None)
                if not (diag_pair and slot == 1):
                    qk_stage(j + slot + 1, idx, 1 - slot)

    def body(i, carry):
        block_pair(2 * i, False)
        return carry

    lax.fori_loop(0, qi, body, 0)
    block_pair(first_diag, True)
    outs = []
    for idx in range(2):
        pv_stage(first_diag + 1, idx, 1)
        acc = acc_ref[idx]
        outs.append(acc[0:n_v] * (1.0 / acc[n_v:n_v + 1]))
    o_a, o_b = outs
    if fox:
        y = jnp.concatenate([o_a, o_b], axis=0)
    else:
        lam_vec = lam_ref[...]
        lam = (jnp.exp(jnp.sum(lam_vec[0:1] * lam_vec[1:2], axis=1, keepdims=True))
               - jnp.exp(jnp.sum(lam_vec[2:3] * lam_vec[3:4], axis=1, keepdims=True))
               + lambda_init)
        y = o_a - lam * o_b
        y = y * lax.rsqrt(jnp.mean(y * y, axis=0, keepdims=True) + SUBLN_EPS)
        y = y * (gsub_ref[...] * (1.0 - lambda_init))
    out_ref[...] = y.T.astype(out_ref.dtype)


def _attention(q, k, vt, extra, n_batch, seq, fox, lambda_init=0.0):
    n_tok, width = q.shape
    n_pairs = width // LANES
    bq, bk = ATTN_Q_BLOCK, ATTN_K_BLOCK
    nq = seq // bq
    n_v = HEAD_DIM if fox else 2 * HEAD_DIM
    q_spec = pl.BlockSpec((bq, LANES), lambda b, hp, qi: (b * nq + qi, hp))
    k_spec = pl.BlockSpec((seq, LANES), lambda b, hp, qi: (b, hp))
    vt_spec = pl.BlockSpec((LANES, seq), lambda b, hp, qi: (hp, b))
    scratch = [pltpu.VMEM((2, bq, LANES), BF16),
               pltpu.VMEM((2, 2, bk, bq), F32),
               pltpu.VMEM((2, 2, bk, bq), BF16),
               pltpu.VMEM((2, n_v + 16, bq), F32),
               pltpu.VMEM((2, 24, bq), F32)]
    if fox:
        extra_specs = [k_spec]
        scratch += [pltpu.VMEM((seq, LANES), BF16)] * 2
    else:
        lam_vec, g_sub = extra
        extra_specs = [pl.BlockSpec(lam_vec.shape, lambda b, hp, qi: (0, 0)),
                       pl.BlockSpec(g_sub.shape, lambda b, hp, qi: (0, 0))]
    return pl.pallas_call(
        functools.partial(_attn_kernel, bq=bq, bk=bk, fox=fox, lambda_init=lambda_init),
        grid=(n_batch, n_pairs, nq),
        in_specs=[q_spec, k_spec, vt_spec] + extra_specs,
        out_specs=q_spec,
        out_shape=jax.ShapeDtypeStruct((n_tok, width), BF16),
        scratch_shapes=scratch,
        compiler_params=_params("arbitrary", "arbitrary", "arbitrary"),
        name="fox_attn" if fox else "diff_attn",
    )(q, k, vt, *extra)


def _post_mixer_kernel(ya_ref, yb_ref, ga_ref, gb_ref, x_ref, wa_ref, wb_ref, wo_ref,
                       gc_ref, wcq_ref, kc_ref, vc_ref, wco_ref, gf_ref, wr_ref, br_ref,
                       x2_ref, hm_ref, lg_ref):
    def sigmoid(t):
        return 1.0 / (1.0 + jnp.exp(-t))

    merged = (sigmoid(ga_ref[...].astype(F32)) * _dot(ya_ref[...], wa_ref[...])
              + sigmoid(gb_ref[...].astype(F32)) * _dot(yb_ref[...], wb_ref[...]))
    x1 = x_ref[...] + _dot(merged.astype(BF16), wo_ref[...])

    hx = _rms(x1, gc_ref[...], NORM_EPS).astype(BF16)
    qc = (_dot(hx, wcq_ref[...]) * QK_SCALE).astype(BF16)
    kc = kc_ref[...]
    vc = vc_ref[...]
    lane = lax.broadcasted_iota(jnp.int32, qc.shape, 1)
    zero = jnp.zeros_like(qc)
    o = jnp.zeros(qc.shape, F32)
    for h in range(CROSS_HEADS):
        in_head = (lane >= h * HEAD_DIM) & (lane < (h + 1) * HEAD_DIM)
        s = _dot_nt(jnp.where(in_head, qc, zero), kc)
        p = jnp.exp(s - jnp.max(s, axis=1, keepdims=True))
        p = p * (1.0 / jnp.sum(p, axis=1, keepdims=True))
        o = jnp.where(in_head, _dot(p.astype(BF16), vc), o)
    x2 = x1 + _dot(o.astype(BF16), wco_ref[...])
    x2_ref[...] = x2

    hm = _rms(x2, gf_ref[...], NORM_EPS)
    hm_ref[...] = hm.astype(BF16)
    hm_hi = hm.astype(BF16)
    hm_lo = (hm - hm_hi.astype(F32)).astype(BF16)
    lg_ref[...] = (_dot(hm_hi, wr_ref[0]) + _dot(hm_hi, wr_ref[1]) + _dot(hm_lo, wr_ref[0])
                   + br_ref[...])


def _post_mixer(ya, yb, ga, gb, x2d, wa, wb, wo, g_cross, wcq, kc, vc, wco, g_ffn, wr, br,
                n_batch, seq, mem_len, tm):
    n_tok, d = x2d.shape
    nblk = seq // tm
    tok = lambda b, j: (b * nblk + j, 0)
    const = lambda b, j: (0, 0)
    full = lambda a: pl.BlockSpec(a.shape, const)
    return pl.pallas_call(
        _post_mixer_kernel,
        grid=(n_batch, nblk),
        in_specs=[pl.BlockSpec((tm, ya.shape[1]), tok), pl.BlockSpec((tm, yb.shape[1]), tok),
                  pl.BlockSpec((tm, d), tok), pl.BlockSpec((tm, d), tok), pl.BlockSpec((tm, d), tok),
                  full(wa), full(wb), full(wo), full(g_cross), full(wcq),
                  pl.BlockSpec((mem_len, kc.shape[1]), lambda b, j: (b, 0)),
                  pl.BlockSpec((mem_len, vc.shape[1]), lambda b, j: (b, 0)),
                  full(wco), full(g_ffn), pl.BlockSpec(wr.shape, lambda b, j: (0, 0, 0)), full(br)],
        out_specs=[pl.BlockSpec((tm, d), tok), pl.BlockSpec((tm, d), tok),
                   pl.BlockSpec((tm, LANES), tok)],
        out_shape=[jax.ShapeDtypeStruct((n_tok, d), F32),
                   jax.ShapeDtypeStruct((n_tok, d), BF16),
                   jax.ShapeDtypeStruct((n_tok, LANES), F32)],
        compiler_params=_params("arbitrary", "arbitrary"),
        name="post_mixer",
    )(ya, yb, ga, gb, x2d, wa, wb, wo, g_cross, wcq, kc, vc, wco, g_ffn, wr, br)


def _expert_kernel(be_ref, nused_ref, xs_ref, wg_ref, wu_ref, wd_ref, ys_ref):
    i = pl.program_id(0)

    @pl.when(i < nused_ref[0])
    def _():
        xb = xs_ref[...]
        g = _dot(xb, wg_ref[0])
        u = _dot(xb, wu_ref[0])
        hb = (g * (1.0 / (1.0 + jnp.exp(-g))) * u).astype(BF16)
        ys_ref[...] = _dot(hb, wd_ref[0]).astype(ys_ref.dtype)

    @pl.when(i >= nused_ref[0])
    def _():
        ys_ref[...] = jnp.zeros_like(ys_ref)


def _experts(block_expert, n_used, xs, wg, wu, wd):
    n_rows, d = xs.shape
    n_blocks = n_rows // MOE_BLOCK
    ff = wg.shape[2]
    grid_spec = pltpu.PrefetchScalarGridSpec(
        num_scalar_prefetch=2,
        grid=(n_blocks,),
        in_specs=[pl.BlockSpec((MOE_BLOCK, d), lambda i, be, nu: (i, 0)),
                  pl.BlockSpec((1, d, ff), lambda i, be, nu: (be[i], 0, 0)),
                  pl.BlockSpec((1, d, ff), lambda i, be, nu: (be[i], 0, 0)),
                  pl.BlockSpec((1, ff, d), lambda i, be, nu: (be[i], 0, 0))],
        out_specs=pl.BlockSpec((MOE_BLOCK, d), lambda i, be, nu: (i, 0)),
    )
    return pl.pallas_call(
        _expert_kernel,
        grid_spec=grid_spec,
        out_shape=jax.ShapeDtypeStruct((n_rows, d), BF16),
        compiler_params=_params("arbitrary"),
        name="experts",
    )(block_expert, n_used, xs, wg, wu, wd)


def _final_kernel(x_ref, y0_ref, y1_ref, w_ref, g_ref, out_ref, *, normalize):
    w = w_ref[...]
    out = (x_ref[...] + w[:, 0:1] * y0_ref[...].astype(F32) + w[:, 1:2] * y1_ref[...].astype(F32))
    out_ref[...] = _rms(out, g_ref[...], NORM_EPS) if normalize else out


def _final(x2, y0, y1, weights, g_final, tm, normalize):
    n_tok, d = x2.shape
    tok = lambda i: (i, 0)
    return pl.pallas_call(
        functools.partial(_final_kernel, normalize=normalize),
        grid=(n_tok // tm,),
        in_specs=[pl.BlockSpec((tm, d), tok), pl.BlockSpec((tm, d), tok), pl.BlockSpec((tm, d), tok),
                  pl.BlockSpec((tm, TOP_K), tok), pl.BlockSpec((1, d), lambda i: (0, 0))],
        out_specs=pl.BlockSpec((tm, d), tok),
        out_shape=jax.ShapeDtypeStruct((n_tok, d), F32),
        compiler_params=_params("arbitrary"),
        name="final_norm",
    )(x2, y0, y1, weights, g_final)


def _route(logits, n_tok):
    group_logits = logits[:, :N_GROUPS]
    group_prob = jax.nn.softmax(group_logits, axis=-1)
    g_sel = jnp.argmax(group_logits, axis=-1)
    g_w = jnp.take_along_axis(group_prob, g_sel[:, None], axis=-1)
    exp_logits = logits[:, N_GROUPS:N_GROUPS + N_EXPERTS].reshape(n_tok, N_GROUPS, EXPERTS_PER_GROUP)
    sel_logits = jnp.take_along_axis(exp_logits, g_sel[:, None, None], axis=1)[:, 0]
    top_vals, top_idx = lax.top_k(sel_logits, TOP_K)
    weights = jax.nn.softmax(top_vals, axis=-1) * g_w
    expert_ids = g_sel[:, None].astype(jnp.int32) * EXPERTS_PER_GROUP + top_idx.astype(jnp.int32)

    n_assign = n_tok * TOP_K
    flat_e = expert_ids.reshape(n_assign)
    experts = jnp.arange(N_EXPERTS, dtype=jnp.int32)

    def lookup(table, idx):
        return jnp.sum(jnp.where(idx[:, None] == experts[None, :], table[None, :], 0), axis=1)

    order = jnp.argsort(flat_e).astype(jnp.int32)
    sorted_pos = jnp.argsort(order).astype(jnp.int32)
    seg_end = jnp.searchsorted(flat_e[order], experts, side='right').astype(jnp.int32)
    counts = seg_end - jnp.concatenate([jnp.zeros((1,), jnp.int32), seg_end[:-1]])
    seg_start = seg_end - counts
    padded = (counts + MOE_BLOCK - 1) // MOE_BLOCK * MOE_BLOCK
    pad_end = jnp.cumsum(padded)
    pad_start = pad_end - padded
    dest = sorted_pos + lookup(pad_start - seg_start, flat_e)

    n_blocks = -(-n_assign // MOE_BLOCK) + N_EXPERTS
    n_rows = n_blocks * MOE_BLOCK
    block_start = jnp.arange(n_blocks, dtype=jnp.int32) * MOE_BLOCK
    block_expert = jnp.minimum(jnp.searchsorted(pad_end, block_start, side='right'),
                               N_EXPERTS - 1).astype(jnp.int32)
    row_e = jnp.repeat(block_expert, MOE_BLOCK)
    within = jnp.arange(n_rows, dtype=jnp.int32) - lookup(pad_start, row_e)
    valid = within < lookup(counts, row_e)
    src = jnp.clip(lookup(seg_start, row_e) + within, 0, n_assign - 1)
    row_tok = jnp.where(valid, order[src] // TOP_K, 0)
    n_used = (pad_end[-1] // MOE_BLOCK).astype(jnp.int32).reshape(1)
    return row_tok, dest, weights, block_expert, n_used


def kernel(x, mem, positions, g_mix, w_in, b_fgate, w_branch_a, w_branch_b, w_out, lambda_q1, lambda_k1, lambda_q2, lambda_k2, g_diff_sub, g_cross, g_mem, w_cq, w_ckv, w_co, g_ffn, w_group, b_group, w_expert, b_expert, w_exp_gate, w_exp_up, w_exp_down, g_final):
    n_batch, seq, d = x.shape
    mem_len = mem.shape[1]
    depth = g_mix.shape[0]
    n_tok = n_batch * seq
    fox_w = FOX_HEADS * HEAD_DIM
    diff_w = DIFF_HEADS * 2 * HEAD_DIM

    half = HEAD_DIM // 2
    inv_freq = 10000.0 ** (-jnp.arange(half, dtype=F32) * 2.0 / HEAD_DIM)
    ang = positions.astype(F32).reshape(n_tok, 1) * inv_freq
    cos_t = jnp.tile(jnp.cos(ang), (1, LANES // half))
    sin_t = jnp.tile(jnp.concatenate([-jnp.sin(ang), jnp.sin(ang)], axis=1), (1, LANES // HEAD_DIM))
    sel = _bias_lane_selectors()

    x2d = x.reshape(n_tok, d)
    mem2d = mem.reshape(n_batch * mem_len, d)
    for l in range(depth):
        lambda_init = 0.8 - 0.6 * math.exp(-0.3 * l)
        o_fv = 2 * fox_w
        o_fl = o_fv + fox_w
        o_dq = o_fl + FOX_HEADS
        o_dv = o_dq + 2 * diff_w
        o_ga = o_dv + diff_w
        wl = w_in[l]
        w_main = jnp.concatenate([wl[:, :o_fv], wl[:, o_dq:o_dv], wl[:, o_ga:]], axis=1).astype(BF16)
        w_vt = jnp.concatenate([wl[:, o_fv:o_fl], wl[:, o_dv:o_ga]], axis=1).T.astype(BF16)
        w_f = jnp.pad(wl[:, o_fl:o_dq], ((0, 0), (0, LANES - FOX_HEADS))).astype(BF16)
        b_row = jnp.pad(b_fgate[l], (0, LANES - FOX_HEADS))[None]

        kc, vc = _mem_kv(mem2d, g_mem[l][None], w_ckv[l].astype(BF16), n_batch, mem_len)
        fq, fk, dq, dk, ga, gb, fvt, dvt, fl = _in_proj(
            x2d, g_mix[l][None], w_main, w_vt, w_f, cos_t, sin_t, tm=512)
        bias_lanes = _fox_prep(fl, b_row, sel, n_batch, seq, tc=512)
        y_a = _attention(fq, fk, fvt, (bias_lanes,), n_batch, seq, fox=True)
        lam_vec = jnp.stack([lambda_q1[l], lambda_k1[l], lambda_q2[l], lambda_k2[l]])
        y_b = _attention(dq, dk, dvt, (lam_vec, g_diff_sub[l][:, None]), n_batch, seq,
                         fox=False, lambda_init=lambda_init)

        w_router = jnp.pad(jnp.concatenate([w_group[l], w_expert[l]], axis=1),
                           ((0, 0), (0, LANES - N_GROUPS - N_EXPERTS)))
        w_router_hi = w_router.astype(BF16)
        w_router = jnp.stack([w_router_hi, (w_router - w_router_hi.astype(F32)).astype(BF16)])
        b_router = jnp.pad(jnp.concatenate([b_group[l], b_expert[l]]),
                           (0, LANES - N_GROUPS - N_EXPERTS))[None]
        x2d, hm, logits = _post_mixer(
            y_a, y_b, ga, gb, x2d, w_branch_a[l].astype(BF16), w_branch_b[l].astype(BF16),
            w_out[l].astype(BF16), g_cross[l][None], w_cq[l].astype(BF16), kc, vc,
            w_co[l].astype(BF16), g_ffn[l][None], w_router, b_router,
            n_batch, seq, mem_len, tm=512)

        row_tok, dest, weights, block_expert, n_used = _route(logits, n_tok)
        xs = hm[row_tok]
        ys = _experts(block_expert, n_used, xs, w_exp_gate[l].astype(BF16),
                      w_exp_up[l].astype(BF16), w_exp_down[l].astype(BF16))
        dest = dest.reshape(n_tok, TOP_K)
        last = l + 1 == depth
        x2d = _final(x2d, ys[dest[:, 0]], ys[dest[:, 1]], weights, g_final[None], tm=512,
                     normalize=last)
    return x2d.reshape(n_batch, seq, d)
```

```python
import functools
import math

import numpy as np
import jax
import jax.numpy as jnp
from jax import lax
from jax.experimental import pallas as pl
from jax.experimental.pallas import tpu as pltpu

HEAD_DIM = 64
LANES = 128
FOX_HEADS = 8
DIFF_HEADS = 4
CROSS_HEADS = 4
N_GROUPS = 4
EXPERTS_PER_GROUP = 8
N_EXPERTS = N_GROUPS * EXPERTS_PER_GROUP
TOP_K = 2
MOE_BLOCK = 512
ATTN_Q_BLOCK = 512
ATTN_K_BLOCK = 256
NORM_EPS = 1e-6
SUBLN_EPS = 1e-5
QK_SCALE = HEAD_DIM ** -0.5
LOG2E = math.log2(math.e)
Q_PRESCALE = QK_SCALE * LOG2E
NEG_BIG = -1e30
VMEM_LIMIT = 56 * 2**20

BF16 = jnp.bfloat16
F32 = jnp.float32


def _rms(t, g, eps):
    return t * lax.rsqrt(jnp.mean(t * t, axis=-1, keepdims=True) + eps) * g


def _dot(a, b):
    return jnp.dot(a, b, preferred_element_type=F32)


def _dot_nt(a, b):
    return lax.dot_general(a, b, (((1,), (1,)), ((), ())), preferred_element_type=F32)


def _params(*sem):
    return pltpu.CompilerParams(dimension_semantics=sem, vmem_limit_bytes=VMEM_LIMIT)


def _mem_kv_kernel(mem_ref, g_ref, w_ref, k_ref, v_ref):
    h = _rms(mem_ref[...], g_ref[...], NORM_EPS).astype(BF16)
    kv = _dot(h, w_ref[...])
    width = k_ref.shape[1]
    k_ref[...] = kv[:, :width].astype(BF16)
    v_ref[...] = kv[:, width:].astype(BF16)


def _mem_kv(mem2, g_mem, w_ckv, n_batch, mem_len):
    d = mem2.shape[1]
    cw = w_ckv.shape[1] // 2
    out = jax.ShapeDtypeStruct((n_batch * mem_len, cw), BF16)
    return pl.pallas_call(
        _mem_kv_kernel,
        grid=(n_batch,),
        in_specs=[pl.BlockSpec((mem_len, d), lambda b: (b, 0)),
                  pl.BlockSpec((1, d), lambda b: (0, 0)),
                  pl.BlockSpec((d, 2 * cw), lambda b: (0, 0))],
        out_specs=[pl.BlockSpec((mem_len, cw), lambda b: (b, 0))] * 2,
        out_shape=[out, out],
        compiler_params=_params("arbitrary"),
        name="mem_kv",
    )(mem2, g_mem, w_ckv)


def _rope(t, cos, sin_signed, first_half):
    fwd = pltpu.roll(t, LANES - HEAD_DIM // 2, axis=1)
    bwd = pltpu.roll(t, HEAD_DIM // 2, axis=1)
    return t * cos + jnp.where(first_half, fwd, bwd) * sin_signed


def _in_proj_kernel(x_ref, g_ref, w_ref, wvt_ref, wf_ref, cos_ref, sin_ref,
                    fq_ref, fk_ref, dq_ref, dk_ref, ga_ref, gb_ref, fvt_ref, dvt_ref, fl_ref):
    h = _rms(x_ref[...], g_ref[...], NORM_EPS).astype(BF16)
    tm = h.shape[0]
    cw = fq_ref.shape[1]

    def proj(chunk):
        return _dot(h, w_ref[:, chunk * cw:(chunk + 1) * cw])

    fq_ref[...] = (proj(0) * Q_PRESCALE).astype(BF16)
    fk_ref[...] = proj(1).astype(BF16)

    cos = cos_ref[...]
    sin = sin_ref[...]
    lane = lax.broadcasted_iota(jnp.int32, (tm, LANES), 1)
    first_half = (lane % HEAD_DIM) < HEAD_DIM // 2
    for out_ref, chunk, scale in ((dq_ref, 2, Q_PRESCALE), (dk_ref, 3, 1.0)):
        t = proj(chunk)
        for c in range(cw // LANES):
            blk = _rope(t[:, c * LANES:(c + 1) * LANES], cos, sin, first_half)
            out_ref[:, c * LANES:(c + 1) * LANES] = (blk * scale).astype(BF16)

    for out_ref, chunk in ((ga_ref, 4), (gb_ref, 6)):
        for c in range(2):
            out_ref[:, c * cw:(c + 1) * cw] = proj(chunk + c).astype(BF16)
    fvt_ref[...] = _dot_nt(wvt_ref[0:cw, :], h).astype(BF16)
    dvt_ref[...] = _dot_nt(wvt_ref[cw:2 * cw, :], h).astype(BF16)
    fl_ref[...] = _dot(h, wf_ref[...])


def _in_proj(x2, g_mix, w_main, w_vt, w_f, cos_t, sin_t, tm):
    n_tok, d = x2.shape
    cw = 512
    tok = lambda i: (i, 0)
    tok_t = lambda i: (0, i)
    const = lambda i: (0, 0)
    o512 = jax.ShapeDtypeStruct((n_tok, cw), BF16)
    o1024 = jax.ShapeDtypeStruct((n_tok, 2 * cw), BF16)
    o512_t = jax.ShapeDtypeStruct((cw, n_tok), BF16)
    return pl.pallas_call(
        _in_proj_kernel,
        grid=(n_tok // tm,),
        in_specs=[pl.BlockSpec((tm, d), tok),
                  pl.BlockSpec((1, d), const),
                  pl.BlockSpec(w_main.shape, const, pipeline_mode=pl.Buffered(1)),
                  pl.BlockSpec(w_vt.shape, const, pipeline_mode=pl.Buffered(1)),
                  pl.BlockSpec(w_f.shape, const),
                  pl.BlockSpec((tm, LANES), tok),
                  pl.BlockSpec((tm, LANES), tok)],
        out_specs=[pl.BlockSpec((tm, cw), tok)] * 4
                  + [pl.BlockSpec((tm, 2 * cw), tok)] * 2
                  + [pl.BlockSpec((cw, tm), tok_t)] * 2
                  + [pl.BlockSpec((tm, LANES), tok)],
        out_shape=[o512] * 4 + [o1024] * 2 + [o512_t] * 2
                  + [jax.ShapeDtypeStruct((n_tok, LANES), F32)],
        compiler_params=_params("arbitrary"),
        name="in_proj",
    )(x2, g_mix, w_main, w_vt, w_f, cos_t, sin_t)


def _bias_lane_selectors():
    sel = np.zeros((3, LANES, FOX_HEADS // 2 * LANES), np.float32)
    for h in range(FOX_HEADS):
        base = (h // 2) * LANES + (HEAD_DIM if h % 2 == 0 else 0)
        for piece in range(3):
            sel[piece, h, base + piece] = 1.0
    return jnp.asarray(sel, BF16)


def _split3(t):
    pieces = []
    for _ in range(3):
        part = t.astype(BF16)
        t = t - part.astype(F32)
        pieces.append(part)
    return pieces


def _fox_prep_kernel(fl_ref, b_ref, tri_ref, sel_ref, out_ref, carry_ref):
    @pl.when(pl.program_id(1) == 0)
    def _():
        carry_ref[...] = jnp.zeros_like(carry_ref)

    z = fl_ref[...] + b_ref[...]
    log_f = jnp.minimum(z, 0.0) - jnp.log(1.0 + jnp.exp(-jnp.abs(z)))
    tc = z.shape[0]
    c = carry_ref[...]
    for part in _split3(log_f):
        c = c + _dot(tri_ref[...], part)
    carry_ref[...] = c[tc - 1:tc, :]
    acc = jnp.zeros(out_ref.shape, F32)
    for piece, part in enumerate(_split3(c * (-LOG2E))):
        acc = acc + _dot(part, sel_ref[piece])
    out_ref[...] = acc.astype(BF16)


def _fox_prep(fl, b_row, sel, n_batch, seq, tc):
    n_tok = fl.shape[0]
    nblk = seq // tc
    width = sel.shape[2]
    tri = jnp.tril(jnp.ones((tc, tc), BF16))
    return pl.pallas_call(
        _fox_prep_kernel,
        grid=(n_batch, nblk),
        in_specs=[pl.BlockSpec((tc, LANES), lambda b, j: (b * nblk + j, 0)),
                  pl.BlockSpec((1, LANES), lambda b, j: (0, 0)),
                  pl.BlockSpec((tc, tc), lambda b, j: (0, 0)),
                  pl.BlockSpec(sel.shape, lambda b, j: (0, 0, 0))],
        out_specs=pl.BlockSpec((tc, width), lambda b, j: (b * nblk + j, 0)),
        out_shape=jax.ShapeDtypeStruct((n_tok, width), BF16),
        scratch_shapes=[pltpu.VMEM((1, LANES), F32)],
        compiler_params=_params("arbitrary", "arbitrary"),
        name="fox_prep",
    )(fl, b_row, tri, sel)


def _attn_kernel(*refs, bq, bk, fox, lambda_init):
    if fox:
        q_ref, k_ref, vt_ref, bl_ref, out_ref, qs_ref, st_ref, p_ref, acc_ref, stat_ref, ka_ref, kb_ref = refs
    else:
        q_ref, k_ref, vt_ref, lam_ref, gsub_ref, out_ref, qs_ref, st_ref, p_ref, acc_ref, stat_ref = refs
    qi = pl.program_id(2)

    q = q_ref[...].astype(F32)
    lane = lax.broadcasted_iota(jnp.int32, q.shape, 1)
    if fox:
        @pl.when(qi == 0)
        def _():
            k = k_ref[...].astype(F32)
            bl = bl_ref[...].astype(F32)
            klane = lax.broadcasted_iota(jnp.int32, k.shape, 1)
            ka_ref[...] = jnp.where(klane < HEAD_DIM, k, bl).astype(BF16)
            kb_ref[...] = jnp.where(klane >= HEAD_DIM, k, bl).astype(BF16)

        q_streams = (jnp.where(lane < HEAD_DIM, q, jnp.where(lane < HEAD_DIM + 3, 1.0, 0.0)),
                     jnp.where(lane >= HEAD_DIM, q, jnp.where(lane < 3, 1.0, 0.0)))
        k_srcs = (ka_ref, kb_ref)
        v_rows = ((0, HEAD_DIM), (HEAD_DIM, 2 * HEAD_DIM))
    else:
        q_streams = (jnp.where(lane < HEAD_DIM, q, 0.0), jnp.where(lane >= HEAD_DIM, q, 0.0))
        k_srcs = (k_ref, k_ref)
        v_rows = ((0, 2 * HEAD_DIM), (0, 2 * HEAD_DIM))
    n_v = v_rows[0][1] - v_rows[0][0]
    assert bq == 2 * bk
    for idx in range(2):
        qs_ref[idx] = q_streams[idx].astype(BF16)
        p_ref[idx, 1] = jnp.zeros((bk, bq), BF16)
        acc_ref[idx] = jnp.zeros((n_v + 16, bq), F32)
    ones_rows = jnp.ones((16, bk), BF16)
    first_diag = 2 * qi

    def qk_stage(kj, idx, slot):
        k0 = pl.multiple_of(kj * bk, bk)
        st_ref[idx, slot] = _dot_nt(k_srcs[idx][pl.ds(k0, bk), :], qs_ref[idx])

    M_ROW, A_ROW = 0, 8

    def pv_stage(kj, idx, slot):
        k0 = pl.multiple_of(kj * bk, bk)
        r0, r1 = v_rows[idx]
        v_aug = jnp.concatenate([vt_ref[r0:r1, pl.ds(k0, bk)], ones_rows], axis=0)
        alpha = stat_ref[idx, A_ROW + 8 * slot:A_ROW + 8 * slot + 1, :]
        acc_ref[idx] = alpha * acc_ref[idx] + _dot(v_aug, p_ref[idx, slot])

    def softmax_stage(idx, slot, diag):
        st = st_ref[idx, slot]
        if diag is not None:
            key_idx = lax.broadcasted_iota(jnp.int32, (bk, bq), 0) + diag * bk
            qry_idx = lax.broadcasted_iota(jnp.int32, (bk, bq), 1)
            st = jnp.where(key_idx <= qry_idx, st, NEG_BIG)
        m = stat_ref[idx, M_ROW:M_ROW + 1, :]
        m_new = jnp.maximum(m, jnp.max(st, axis=0, keepdims=True))
        p_ref[idx, slot] = jnp.exp2(st - m_new).astype(BF16)
        stat_ref[idx, M_ROW:M_ROW + 1, :] = m_new
        stat_ref[idx, A_ROW + 8 * slot:A_ROW + 8 * slot + 1, :] = jnp.exp2(m - m_new)

    for idx in range(2):
        stat_ref[idx, M_ROW:M_ROW + 8, :] = jnp.full((8, bq), NEG_BIG, F32)
        stat_ref[idx, A_ROW:A_ROW + 16, :] = jnp.ones((16, bq), F32)
        qk_stage(0, idx, 0)

    def block_pair(j, diag_pair):
        for slot in range(2):
            for idx in range(2):
                pv_stage(jnp.maximum(j + slot - 1, 0), idx, 1 - slot)
                softmax_stage(idx, slot, slot if diag_pair else None)
                if not (diag_pair and slot == 1):
                    qk_stage(j + slot + 1, idx, 1 - slot)

    def body(i, carry):
        block_pair(2 * i, False)
        return carry

    lax.fori_loop(0, qi, body, 0)
    block_pair(first_diag, True)
    outs = []
    for idx in range(2):
        pv_stage(first_diag + 1, idx, 1)
        acc = acc_ref[idx]
        outs.append(acc[0:n_v] * (1.0 / acc[n_v:n_v + 1]))
    o_a, o_b = outs
    if fox:
        y = jnp.concatenate([o_a, o_b], axis=0)
    else:
        lam_vec = lam_ref[...]
        lam = (jnp.exp(jnp.sum(lam_vec[0:1] * lam_vec[1:2], axis=1, keepdims=True))
               - jnp.exp(jnp.sum(lam_vec[2:3] * lam_vec[3:4], axis=1, keepdims=True))
               + lambda_init)
        y = o_a - lam * o_b
        y = y * lax.rsqrt(jnp.mean(y * y, axis=0, keepdims=True) + SUBLN_EPS)
        y = y * (gsub_ref[...] * (1.0 - lambda_init))
    out_ref[...] = y.T.astype(out_ref.dtype)


def _attention(q, k, vt, extra, n_batch, seq, fox, lambda_init=0.0):
    n_tok, width = q.shape
    n_pairs = width // LANES
    bq, bk = ATTN_Q_BLOCK, ATTN_K_BLOCK
    nq = seq // bq
    n_v = HEAD_DIM if fox else 2 * HEAD_DIM
    q_spec = pl.BlockSpec((bq, LANES), lambda b, hp, qi: (b * nq + qi, hp))
    k_spec = pl.BlockSpec((seq, LANES), lambda b, hp, qi: (b, hp))
    vt_spec = pl.BlockSpec((LANES, seq), lambda b, hp, qi: (hp, b))
    scratch = [pltpu.VMEM((2, bq, LANES), BF16),
               pltpu.VMEM((2, 2, bk, bq), F32),
               pltpu.VMEM((2, 2, bk, bq), BF16),
               pltpu.VMEM((2, n_v + 16, bq), F32),
               pltpu.VMEM((2, 24, bq), F32)]
    if fox:
        extra_specs = [k_spec]
        scratch += [pltpu.VMEM((seq, LANES), BF16)] * 2
    else:
        lam_vec, g_sub = extra
        extra_specs = [pl.BlockSpec(lam_vec.shape, lambda b, hp, qi: (0, 0)),
                       pl.BlockSpec(g_sub.shape, lambda b, hp, qi: (0, 0))]
    return pl.pallas_call(
        functools.partial(_attn_kernel, bq=bq, bk=bk, fox=fox, lambda_init=lambda_init),
        grid=(n_batch, n_pairs, nq),
        in_specs=[q_spec, k_spec, vt_spec] + extra_specs,
        out_specs=q_spec,
        out_shape=jax.ShapeDtypeStruct((n_tok, width), BF16),
        scratch_shapes=scratch,
        compiler_params=_params("arbitrary", "arbitrary", "arbitrary"),
        name="fox_attn" if fox else "diff_attn",
    )(q, k, vt, *extra)


def _post_mixer_kernel(ya_ref, yb_ref, ga_ref, gb_ref, x_ref, wa_ref, wb_ref, wo_ref,
                       gc_ref, wcq_ref, kc_ref, vc_ref, wco_ref, gf_ref, wr_ref, br_ref,
                       x2_ref, hm_ref, lg_ref):
    def sigmoid(t):
        return 1.0 / (1.0 + jnp.exp(-t))

    merged = (sigmoid(ga_ref[...].astype(F32)) * _dot(ya_ref[...], wa_ref[...])
              + sigmoid(gb_ref[...].astype(F32)) * _dot(yb_ref[...], wb_ref[...]))
    x1 = x_ref[...] + _dot(merged.astype(BF16), wo_ref[...])

    hx = _rms(x1, gc_ref[...], NORM_EPS).astype(BF16)
    qc = (_dot(hx, wcq_ref[...]) * QK_SCALE).astype(BF16)
    kc = kc_ref[...]
    vc = vc_ref[...]
    lane = lax.broadcasted_iota(jnp.int32, qc.shape, 1)
    zero = jnp.zeros_like(qc)
    o = jnp.zeros(qc.shape, F32)
    for h in range(CROSS_HEADS):
        in_head = (lane >= h * HEAD_DIM) & (lane < (h + 1) * HEAD_DIM)
        s = _dot_nt(jnp.where(in_head, qc, zero), kc)
        p = jnp.exp(s - jnp.max(s, axis=1, keepdims=True))
        p = p * (1.0 / jnp.sum(p, axis=1, keepdims=True))
        o = jnp.where(in_head, _dot(p.astype(BF16), vc), o)
    x2 = x1 + _dot(o.astype(BF16), wco_ref[...])
    x2_ref[...] = x2

    hm = _rms(x2, gf_ref[...], NORM_EPS)
    hm_ref[...] = hm.astype(BF16)
    hm_hi = hm.astype(BF16)
    hm_lo = (hm - hm_hi.astype(F32)).astype(BF16)
    lg_ref[...] = (_dot(hm_hi, wr_ref[0]) + _dot(hm_hi, wr_ref[1]) + _dot(hm_lo, wr_ref[0])
                   + br_ref[...])


def _post_mixer(ya, yb, ga, gb, x2d, wa, wb, wo, g_cross, wcq, kc, vc, wco, g_ffn, wr, br,
                n_batch, seq, mem_len, tm):
    n_tok, d = x2d.shape
    nblk = seq // tm
    tok = lambda b, j: (b * nblk + j, 0)
    const = lambda b, j: (0, 0)
    full = lambda a: pl.BlockSpec(a.shape, const)
    return pl.pallas_call(
        _post_mixer_kernel,
        grid=(n_batch, nblk),
        in_specs=[pl.BlockSpec((tm, ya.shape[1]), tok), pl.BlockSpec((tm, yb.shape[1]), tok),
                  pl.BlockSpec((tm, d), tok), pl.BlockSpec((tm, d), tok), pl.BlockSpec((tm, d), tok),
                  full(wa), full(wb), full(wo), full(g_cross), full(wcq),
                  pl.BlockSpec((mem_len, kc.shape[1]), lambda b, j: (b, 0)),
                  pl.BlockSpec((mem_len, vc.shape[1]), lambda b, j: (b, 0)),
                  full(wco), full(g_ffn), pl.BlockSpec(wr.shape, lambda b, j: (0, 0, 0)), full(br)],
        out_specs=[pl.BlockSpec((tm, d), tok), pl.BlockSpec((tm, d), tok),
                   pl.BlockSpec((tm, LANES), tok)],
        out_shape=[jax.ShapeDtypeStruct((n_tok, d), F32),
                   jax.ShapeDtypeStruct((n_tok, d), BF16),
                   jax.ShapeDtypeStruct((n_tok, LANES), F32)],
        compiler_params=_params("arbitrary", "arbitrary"),
        name="post_mixer",
    )(ya, yb, ga, gb, x2d, wa, wb, wo, g_cross, wcq, kc, vc, wco, g_ffn, wr, br)


def _expert_kernel(be_ref, nused_ref, xs_ref, wg_ref, wu_ref, wd_ref, ys_ref, wg_s, wu_s, wd_s):
    i = pl.program_id(0)

    @pl.when((i == 0) | (be_ref[i] != be_ref[jnp.maximum(i - 1, 0)]))
    def _():
        wg_s[...] = wg_ref[0].astype(BF16)
        wu_s[...] = wu_ref[0].astype(BF16)
        wd_s[...] = wd_ref[0].astype(BF16)

    @pl.when(i < nused_ref[0])
    def _():
        xb = xs_ref[...]
        g = _dot(xb, wg_s[...])
        u = _dot(xb, wu_s[...])
        hb = (g * (1.0 / (1.0 + jnp.exp(-g))) * u).astype(BF16)
        ys_ref[...] = _dot(hb, wd_s[...]).astype(ys_ref.dtype)

    @pl.when(i >= nused_ref[0])
    def _():
        ys_ref[...] = jnp.zeros_like(ys_ref)


def _experts(block_expert, n_used, xs, wg, wu, wd):
    n_rows, d = xs.shape
    n_blocks = n_rows // MOE_BLOCK
    ff = wg.shape[2]
    grid_spec = pltpu.PrefetchScalarGridSpec(
        num_scalar_prefetch=2,
        grid=(n_blocks,),
        in_specs=[pl.BlockSpec((MOE_BLOCK, d), lambda i, be, nu: (i, 0)),
                  pl.BlockSpec((1, d, ff), lambda i, be, nu: (be[i], 0, 0)),
                  pl.BlockSpec((1, d, ff), lambda i, be, nu: (be[i], 0, 0)),
                  pl.BlockSpec((1, ff, d), lambda i, be, nu: (be[i], 0, 0))],
        out_specs=pl.BlockSpec((MOE_BLOCK, d), lambda i, be, nu: (i, 0)),
        scratch_shapes=[pltpu.VMEM((d, ff), BF16), pltpu.VMEM((d, ff), BF16),
                        pltpu.VMEM((ff, d), BF16)],
    )
    return pl.pallas_call(
        _expert_kernel,
        grid_spec=grid_spec,
        out_shape=jax.ShapeDtypeStruct((n_rows, d), BF16),
        compiler_params=_params("arbitrary"),
        name="experts",
    )(block_expert, n_used, xs, wg, wu, wd)


def _final_kernel(x_ref, y0_ref, y1_ref, w_ref, g_ref, out_ref, *, normalize):
    w = w_ref[...]
    out = (x_ref[...] + w[:, 0:1] * y0_ref[...].astype(F32) + w[:, 1:2] * y1_ref[...].astype(F32))
    out_ref[...] = _rms(out, g_ref[...], NORM_EPS) if normalize else out


def _final(x2, y0, y1, weights, g_final, tm, normalize):
    n_tok, d = x2.shape
    tok = lambda i: (i, 0)
    return pl.pallas_call(
        functools.partial(_final_kernel, normalize=normalize),
        grid=(n_tok // tm,),
        in_specs=[pl.BlockSpec((tm, d), tok), pl.BlockSpec((tm, d), tok), pl.BlockSpec((tm, d), tok),
                  pl.BlockSpec((tm, TOP_K), tok), pl.BlockSpec((1, d), lambda i: (0, 0))],
        out_specs=pl.BlockSpec((tm, d), tok),
        out_shape=jax.ShapeDtypeStruct((n_tok, d), F32),
        compiler_params=_params("arbitrary"),
        name="final_norm",
    )(x2, y0, y1, weights, g_final)


def _route(logits, n_tok):
    group_logits = logits[:, :N_GROUPS]
    group_prob = jax.nn.softmax(group_logits, axis=-1)
    g_sel = jnp.argmax(group_logits, axis=-1)
    g_w = jnp.take_along_axis(group_prob, g_sel[:, None], axis=-1)
    exp_logits = logits[:, N_GROUPS:N_GROUPS + N_EXPERTS].reshape(n_tok, N_GROUPS, EXPERTS_PER_GROUP)
    sel_logits = jnp.take_along_axis(exp_logits, g_sel[:, None, None], axis=1)[:, 0]
    top_vals, top_idx = lax.top_k(sel_logits, TOP_K)
    weights = jax.nn.softmax(top_vals, axis=-1) * g_w
    expert_ids = g_sel[:, None].astype(jnp.int32) * EXPERTS_PER_GROUP + top_idx.astype(jnp.int32)

    n_assign = n_tok * TOP_K
    flat_e = expert_ids.reshape(n_assign)
    experts = jnp.arange(N_EXPERTS, dtype=jnp.int32)

    def lookup(table, idx):
        return jnp.sum(jnp.where(idx[:, None] == experts[None, :], table[None, :], 0), axis=1)

    order = jnp.argsort(flat_e).astype(jnp.int32)
    sorted_pos = jnp.argsort(order).astype(jnp.int32)
    seg_end = jnp.searchsorted(flat_e[order], experts, side='right').astype(jnp.int32)
    counts = seg_end - jnp.concatenate([jnp.zeros((1,), jnp.int32), seg_end[:-1]])
    seg_start = seg_end - counts
    padded = (counts + MOE_BLOCK - 1) // MOE_BLOCK * MOE_BLOCK
    pad_end = jnp.cumsum(padded)
    pad_start = pad_end - padded
    dest = sorted_pos + lookup(pad_start - seg_start, flat_e)

    n_blocks = -(-n_assign // MOE_BLOCK) + N_EXPERTS
    n_rows = n_blocks * MOE_BLOCK
    block_start = jnp.arange(n_blocks, dtype=jnp.int32) * MOE_BLOCK
    block_expert = jnp.minimum(jnp.searchsorted(pad_end, block_start, side='right'),
                               N_EXPERTS - 1).astype(jnp.int32)
    row_e = jnp.repeat(block_expert, MOE_BLOCK)
    within = jnp.arange(n_rows, dtype=jnp.int32) - lookup(pad_start, row_e)
    valid = within < lookup(counts, row_e)
    src = jnp.clip(lookup(seg_start, row_e) + within, 0, n_assign - 1)
    row_tok = jnp.where(valid, order[src] // TOP_K, 0)
    n_used = (pad_end[-1] // MOE_BLOCK).astype(jnp.int32).reshape(1)
    return row_tok, dest, weights, block_expert, n_used


def kernel(x, mem, positions, g_mix, w_in, b_fgate, w_branch_a, w_branch_b, w_out, lambda_q1, lambda_k1, lambda_q2, lambda_k2, g_diff_sub, g_cross, g_mem, w_cq, w_ckv, w_co, g_ffn, w_group, b_group, w_expert, b_expert, w_exp_gate, w_exp_up, w_exp_down, g_final):
    n_batch, seq, d = x.shape
    mem_len = mem.shape[1]
    depth = g_mix.shape[0]
    n_tok = n_batch * seq
    fox_w = FOX_HEADS * HEAD_DIM
    diff_w = DIFF_HEADS * 2 * HEAD_DIM

    half = HEAD_DIM // 2
    inv_freq = 10000.0 ** (-jnp.arange(half, dtype=F32) * 2.0 / HEAD_DIM)
    ang = positions.astype(F32).reshape(n_tok, 1) * inv_freq
    cos_t = jnp.tile(jnp.cos(ang), (1, LANES // half))
    sin_t = jnp.tile(jnp.concatenate([-jnp.sin(ang), jnp.sin(ang)], axis=1), (1, LANES // HEAD_DIM))
    sel = _bias_lane_selectors()

    x2d = x.reshape(n_tok, d)
    mem2d = mem.reshape(n_batch * mem_len, d)
    for l in range(depth):
        lambda_init = 0.8 - 0.6 * math.exp(-0.3 * l)
        o_fv = 2 * fox_w
        o_fl = o_fv + fox_w
        o_dq = o_fl + FOX_HEADS
        o_dv = o_dq + 2 * diff_w
        o_ga = o_dv + diff_w
        wl = w_in[l]
        w_main = jnp.concatenate([wl[:, :o_fv], wl[:, o_dq:o_dv], wl[:, o_ga:]], axis=1).astype(BF16)
        w_vt = jnp.concatenate([wl[:, o_fv:o_fl], wl[:, o_dv:o_ga]], axis=1).T.astype(BF16)
        w_f = jnp.pad(wl[:, o_fl:o_dq], ((0, 0), (0, LANES - FOX_HEADS))).astype(BF16)
        b_row = jnp.pad(b_fgate[l], (0, LANES - FOX_HEADS))[None]

        kc, vc = _mem_kv(mem2d, g_mem[l][None], w_ckv[l].astype(BF16), n_batch, mem_len)
        fq, fk, dq, dk, ga, gb, fvt, dvt, fl = _in_proj(
            x2d, g_mix[l][None], w_main, w_vt, w_f, cos_t, sin_t, tm=512)
        bias_lanes = _fox_prep(fl, b_row, sel, n_batch, seq, tc=512)
        y_a = _attention(fq, fk, fvt, (bias_lanes,), n_batch, seq, fox=True)
        lam_vec = jnp.stack([lambda_q1[l], lambda_k1[l], lambda_q2[l], lambda_k2[l]])
        y_b = _attention(dq, dk, dvt, (lam_vec, g_diff_sub[l][:, None]), n_batch, seq,
                         fox=False, lambda_init=lambda_init)

        w_router = jnp.pad(jnp.concatenate([w_group[l], w_expert[l]], axis=1),
                           ((0, 0), (0, LANES - N_GROUPS - N_EXPERTS)))
        w_router_hi = w_router.astype(BF16)
        w_router = jnp.stack([w_router_hi, (w_router - w_router_hi.astype(F32)).astype(BF16)])
        b_router = jnp.pad(jnp.concatenate([b_group[l], b_expert[l]]),
                           (0, LANES - N_GROUPS - N_EXPERTS))[None]
        x2d, hm, logits = _post_mixer(
            y_a, y_b, ga, gb, x2d, w_branch_a[l].astype(BF16), w_branch_b[l].astype(BF16),
            w_out[l].astype(BF16), g_cross[l][None], w_cq[l].astype(BF16), kc, vc,
            w_co[l].astype(BF16), g_ffn[l][None], w_router, b_router,
            n_batch, seq, mem_len, tm=512)

        row_tok, dest, weights, block_expert, n_used = _route(logits, n_tok)
        xs = hm[row_tok]
        ys = _experts(block_expert, n_used, xs, w_exp_gate[l], w_exp_up[l], w_exp_down[l])
        dest = dest.reshape(n_tok, TOP_K)
        last = l + 1 == depth
        x2d = _final(x2d, ys[dest[:, 0]], ys[dest[:, 1]], weights, g_final[None], tm=512,
                     normalize=last)
    return x2d.reshape(n_batch, seq, d)
```

```python
import functools
import math

import numpy as np
import jax
import jax.numpy as jnp
from jax import lax
from jax.experimental import pallas as pl
from jax.experimental.pallas import tpu as pltpu

HEAD_DIM = 64
LANES = 128
FOX_HEADS = 8
DIFF_HEADS = 4
CROSS_HEADS = 4
N_GROUPS = 4
EXPERTS_PER_GROUP = 8
N_EXPERTS = N_GROUPS * EXPERTS_PER_GROUP
TOP_K = 2
MOE_BLOCK = 512
ATTN_Q_BLOCK = 512
ATTN_K_BLOCK = 256
NORM_EPS = 1e-6
SUBLN_EPS = 1e-5
QK_SCALE = HEAD_DIM ** -0.5
LOG2E = math.log2(math.e)
Q_PRESCALE = QK_SCALE * LOG2E
NEG_BIG = -1e30
VMEM_LIMIT = 56 * 2**20

BF16 = jnp.bfloat16
F32 = jnp.float32


def _rms(t, g, eps):
    return t * lax.rsqrt(jnp.mean(t * t, axis=-1, keepdims=True) + eps) * g


def _dot(a, b):
    return jnp.dot(a, b, preferred_element_type=F32)


def _dot_nt(a, b):
    return lax.dot_general(a, b, (((1,), (1,)), ((), ())), preferred_element_type=F32)


def _params(*sem):
    return pltpu.CompilerParams(dimension_semantics=sem, vmem_limit_bytes=VMEM_LIMIT)


def _mem_kv_kernel(mem_ref, g_ref, w_ref, k_ref, v_ref):
    h = _rms(mem_ref[...], g_ref[...], NORM_EPS).astype(BF16)
    kv = _dot(h, w_ref[...])
    width = k_ref.shape[1]
    k_ref[...] = kv[:, :width].astype(BF16)
    v_ref[...] = kv[:, width:].astype(BF16)


def _mem_kv(mem2, g_mem, w_ckv, n_batch, mem_len):
    d = mem2.shape[1]
    cw = w_ckv.shape[1] // 2
    out = jax.ShapeDtypeStruct((n_batch * mem_len, cw), BF16)
    return pl.pallas_call(
        _mem_kv_kernel,
        grid=(n_batch,),
        in_specs=[pl.BlockSpec((mem_len, d), lambda b: (b, 0)),
                  pl.BlockSpec((1, d), lambda b: (0, 0)),
                  pl.BlockSpec((d, 2 * cw), lambda b: (0, 0))],
        out_specs=[pl.BlockSpec((mem_len, cw), lambda b: (b, 0))] * 2,
        out_shape=[out, out],
        compiler_params=_params("arbitrary"),
        name="mem_kv",
    )(mem2, g_mem, w_ckv)


def _rope(t, cos, sin_signed, first_half):
    fwd = pltpu.roll(t, LANES - HEAD_DIM // 2, axis=1)
    bwd = pltpu.roll(t, HEAD_DIM // 2, axis=1)
    return t * cos + jnp.where(first_half, fwd, bwd) * sin_signed


def _in_proj_kernel(x_ref, g_ref, w_ref, wvt_ref, wf_ref, cos_ref, sin_ref,
                    fq_ref, fk_ref, dq_ref, dk_ref, ga_ref, gb_ref, fvt_ref, dvt_ref, fl_ref):
    h = _rms(x_ref[...], g_ref[...], NORM_EPS).astype(BF16)
    tm = h.shape[0]
    cw = fq_ref.shape[1]

    def proj(chunk):
        return _dot(h, w_ref[:, chunk * cw:(chunk + 1) * cw])

    fq_ref[...] = (proj(0) * Q_PRESCALE).astype(BF16)
    fk_ref[...] = proj(1).astype(BF16)

    cos = cos_ref[...]
    sin = sin_ref[...]
    lane = lax.broadcasted_iota(jnp.int32, (tm, LANES), 1)
    first_half = (lane % HEAD_DIM) < HEAD_DIM // 2
    for out_ref, chunk, scale in ((dq_ref, 2, Q_PRESCALE), (dk_ref, 3, 1.0)):
        t = proj(chunk)
        for c in range(cw // LANES):
            blk = _rope(t[:, c * LANES:(c + 1) * LANES], cos, sin, first_half)
            out_ref[:, c * LANES:(c + 1) * LANES] = (blk * scale).astype(BF16)

    for out_ref, chunk in ((ga_ref, 4), (gb_ref, 6)):
        for c in range(2):
            out_ref[:, c * cw:(c + 1) * cw] = proj(chunk + c).astype(BF16)
    fvt_ref[...] = _dot_nt(wvt_ref[0:cw, :], h).astype(BF16)
    dvt_ref[...] = _dot_nt(wvt_ref[cw:2 * cw, :], h).astype(BF16)
    fl_ref[...] = _dot(h, wf_ref[...])


def _in_proj(x2, g_mix, w_main, w_vt, w_f, cos_t, sin_t, tm):
    n_tok, d = x2.shape
    cw = 512
    tok = lambda i: (i, 0)
    tok_t = lambda i: (0, i)
    const = lambda i: (0, 0)
    o512 = jax.ShapeDtypeStruct((n_tok, cw), BF16)
    o1024 = jax.ShapeDtypeStruct((n_tok, 2 * cw), BF16)
    o512_t = jax.ShapeDtypeStruct((cw, n_tok), BF16)
    return pl.pallas_call(
        _in_proj_kernel,
        grid=(n_tok // tm,),
        in_specs=[pl.BlockSpec((tm, d), tok),
                  pl.BlockSpec((1, d), const),
                  pl.BlockSpec(w_main.shape, const, pipeline_mode=pl.Buffered(1)),
                  pl.BlockSpec(w_vt.shape, const, pipeline_mode=pl.Buffered(1)),
                  pl.BlockSpec(w_f.shape, const),
                  pl.BlockSpec((tm, LANES), tok),
                  pl.BlockSpec((tm, LANES), tok)],
        out_specs=[pl.BlockSpec((tm, cw), tok)] * 4
                  + [pl.BlockSpec((tm, 2 * cw), tok)] * 2
                  + [pl.BlockSpec((cw, tm), tok_t)] * 2
                  + [pl.BlockSpec((tm, LANES), tok)],
        out_shape=[o512] * 4 + [o1024] * 2 + [o512_t] * 2
                  + [jax.ShapeDtypeStruct((n_tok, LANES), F32)],
        compiler_params=_params("arbitrary"),
        name="in_proj",
    )(x2, g_mix, w_main, w_vt, w_f, cos_t, sin_t)


def _bias_lane_selectors():
    sel = np.zeros((3, LANES, FOX_HEADS // 2 * LANES), np.float32)
    for h in range(FOX_HEADS):
        base = (h // 2) * LANES + (HEAD_DIM if h % 2 == 0 else 0)
        for piece in range(3):
            sel[piece, h, base + piece] = 1.0
    return jnp.asarray(sel, BF16)


def _split3(t):
    pieces = []
    for _ in range(3):
        part = t.astype(BF16)
        t = t - part.astype(F32)
        pieces.append(part)
    return pieces


def _fox_prep_kernel(fl_ref, b_ref, tri_ref, sel_ref, out_ref, carry_ref):
    @pl.when(pl.program_id(1) == 0)
    def _():
        carry_ref[...] = jnp.zeros_like(carry_ref)

    z = fl_ref[...] + b_ref[...]
    log_f = jnp.minimum(z, 0.0) - jnp.log(1.0 + jnp.exp(-jnp.abs(z)))
    tc = z.shape[0]
    c = carry_ref[...]
    for part in _split3(log_f):
        c = c + _dot(tri_ref[...], part)
    carry_ref[...] = c[tc - 1:tc, :]
    acc = jnp.zeros(out_ref.shape, F32)
    for piece, part in enumerate(_split3(c * (-LOG2E))):
        acc = acc + _dot(part, sel_ref[piece])
    out_ref[...] = acc.astype(BF16)


def _fox_prep(fl, b_row, sel, n_batch, seq, tc):
    n_tok = fl.shape[0]
    nblk = seq // tc
    width = sel.shape[2]
    tri = jnp.tril(jnp.ones((tc, tc), BF16))
    return pl.pallas_call(
        _fox_prep_kernel,
        grid=(n_batch, nblk),
        in_specs=[pl.BlockSpec((tc, LANES), lambda b, j: (b * nblk + j, 0)),
                  pl.BlockSpec((1, LANES), lambda b, j: (0, 0)),
                  pl.BlockSpec((tc, tc), lambda b, j: (0, 0)),
                  pl.BlockSpec(sel.shape, lambda b, j: (0, 0, 0))],
        out_specs=pl.BlockSpec((tc, width), lambda b, j: (b * nblk + j, 0)),
        out_shape=jax.ShapeDtypeStruct((n_tok, width), BF16),
        scratch_shapes=[pltpu.VMEM((1, LANES), F32)],
        compiler_params=_params("arbitrary", "arbitrary"),
        name="fox_prep",
    )(fl, b_row, tri, sel)


def _attn_kernel(*refs, bq, bk, fox, lambda_init):
    if fox:
        q_ref, k_ref, vt_ref, bl_ref, out_ref, qs_ref, st_ref, p_ref, acc_ref, stat_ref, ka_ref, kb_ref = refs
    else:
        q_ref, k_ref, vt_ref, lam_ref, gsub_ref, out_ref, qs_ref, st_ref, p_ref, acc_ref, stat_ref = refs
    qi = pl.program_id(2)

    q = q_ref[...].astype(F32)
    lane = lax.broadcasted_iota(jnp.int32, q.shape, 1)
    if fox:
        @pl.when(qi == 0)
        def _():
            k = k_ref[...].astype(F32)
            bl = bl_ref[...].astype(F32)
            klane = lax.broadcasted_iota(jnp.int32, k.shape, 1)
            ka_ref[...] = jnp.where(klane < HEAD_DIM, k, bl).astype(BF16)
            kb_ref[...] = jnp.where(klane >= HEAD_DIM, k, bl).astype(BF16)

        q_streams = (jnp.where(lane < HEAD_DIM, q, jnp.where(lane < HEAD_DIM + 3, 1.0, 0.0)),
                     jnp.where(lane >= HEAD_DIM, q, jnp.where(lane < 3, 1.0, 0.0)))
        k_srcs = (ka_ref, kb_ref)
        v_rows = ((0, HEAD_DIM), (HEAD_DIM, 2 * HEAD_DIM))
    else:
        q_streams = (jnp.where(lane < HEAD_DIM, q, 0.0), jnp.where(lane >= HEAD_DIM, q, 0.0))
        k_srcs = (k_ref, k_ref)
        v_rows = ((0, 2 * HEAD_DIM), (0, 2 * HEAD_DIM))
    n_v = v_rows[0][1] - v_rows[0][0]
    assert bq == 2 * bk
    for idx in range(2):
        qs_ref[idx] = q_streams[idx].astype(BF16)
        p_ref[idx, 1] = jnp.zeros((bk, bq), BF16)
        acc_ref[idx] = jnp.zeros((n_v + 16, bq), F32)
    ones_rows = jnp.ones((16, bk), BF16)
    first_diag = 2 * qi

    def qk_stage(kj, idx, slot):
        k0 = pl.multiple_of(kj * bk, bk)
        st_ref[idx, slot] = _dot_nt(k_srcs[idx][pl.ds(k0, bk), :], qs_ref[idx])

    M_ROW, A_ROW = 0, 8

    def pv_stage(kj, idx, slot):
        k0 = pl.multiple_of(kj * bk, bk)
        r0, r1 = v_rows[idx]
        v_aug = jnp.concatenate([vt_ref[r0:r1, pl.ds(k0, bk)], ones_rows], axis=0)
        alpha = stat_ref[idx, A_ROW + 8 * slot:A_ROW + 8 * slot + 1, :]
        acc_ref[idx] = alpha * acc_ref[idx] + _dot(v_aug, p_ref[idx, slot])

    def softmax_stage(idx, slot, diag):
        st = st_ref[idx, slot]
        if diag is not None:
            key_idx = lax.broadcasted_iota(jnp.int32, (bk, bq), 0) + diag * bk
            qry_idx = lax.broadcasted_iota(jnp.int32, (bk, bq), 1)
            st = jnp.where(key_idx <= qry_idx, st, NEG_BIG)
        m = stat_ref[idx, M_ROW:M_ROW + 1, :]
        m_new = jnp.maximum(m, jnp.max(st, axis=0, keepdims=True))
        p_ref[idx, slot] = jnp.exp2(st - m_new).astype(BF16)
        stat_ref[idx, M_ROW:M_ROW + 1, :] = m_new
        stat_ref[idx, A_ROW + 8 * slot:A_ROW + 8 * slot + 1, :] = jnp.exp2(m - m_new)

    for idx in range(2):
        stat_ref[idx, M_ROW:M_ROW + 8, :] = jnp.full((8, bq), NEG_BIG, F32)
        stat_ref[idx, A_ROW:A_ROW + 16, :] = jnp.ones((16, bq), F32)
        qk_stage(0, idx, 0)

    def block_pair(j, diag_pair):
        for slot in range(2):
            for idx in range(2):
                pv_stage(jnp.maximum(j + slot - 1, 0), idx, 1 - slot)
                softmax_stage(idx, slot, slot if diag_pair else None)
                if not (diag_pair and slot == 1):
                    qk_stage(j + slot + 1, idx, 1 - slot)

    def body(i, carry):
        block_pair(2 * i, False)
        return carry

    lax.fori_loop(0, qi, body, 0)
    block_pair(first_diag, True)
    outs = []
    for idx in range(2):
        pv_stage(first_diag + 1, idx, 1)
        acc = acc_ref[idx]
        outs.append(acc[0:n_v] * (1.0 / acc[n_v:n_v + 1]))
    o_a, o_b = outs
    if fox:
        y = jnp.concatenate([o_a, o_b], axis=0)
    else:
        lam_vec = lam_ref[...]
        lam = (jnp.exp(jnp.sum(lam_vec[0:1] * lam_vec[1:2], axis=1, keepdims=True))
               - jnp.exp(jnp.sum(lam_vec[2:3] * lam_vec[3:4], axis=1, keepdims=True))
               + lambda_init)
        y = o_a - lam * o_b
        y = y * lax.rsqrt(jnp.mean(y * y, axis=0, keepdims=True) + SUBLN_EPS)
        y = y * (gsub_ref[...] * (1.0 - lambda_init))
    out_ref[...] = y.T.astype(out_ref.dtype)


def _attention(q, k, vt, extra, n_batch, seq, fox, lambda_init=0.0):
    n_tok, width = q.shape
    n_pairs = width // LANES
    bq, bk = ATTN_Q_BLOCK, ATTN_K_BLOCK
    nq = seq // bq
    n_v = HEAD_DIM if fox else 2 * HEAD_DIM
    q_spec = pl.BlockSpec((bq, LANES), lambda b, hp, qi: (b * nq + qi, hp))
    k_spec = pl.BlockSpec((seq, LANES), lambda b, hp, qi: (b, hp))
    vt_spec = pl.BlockSpec((LANES, seq), lambda b, hp, qi: (hp, b))
    scratch = [pltpu.VMEM((2, bq, LANES), BF16),
               pltpu.VMEM((2, 2, bk, bq), F32),
               pltpu.VMEM((2, 2, bk, bq), BF16),
               pltpu.VMEM((2, n_v + 16, bq), F32),
               pltpu.VMEM((2, 24, bq), F32)]
    if fox:
        extra_specs = [k_spec]
        scratch += [pltpu.VMEM((seq, LANES), BF16)] * 2
    else:
        lam_vec, g_sub = extra
        extra_specs = [pl.BlockSpec(lam_vec.shape, lambda b, hp, qi: (0, 0)),
                       pl.BlockSpec(g_sub.shape, lambda b, hp, qi: (0, 0))]
    return pl.pallas_call(
        functools.partial(_attn_kernel, bq=bq, bk=bk, fox=fox, lambda_init=lambda_init),
        grid=(n_batch, n_pairs, nq),
        in_specs=[q_spec, k_spec, vt_spec] + extra_specs,
        out_specs=q_spec,
        out_shape=jax.ShapeDtypeStruct((n_tok, width), BF16),
        scratch_shapes=scratch,
        compiler_params=_params("arbitrary", "arbitrary", "arbitrary"),
        name="fox_attn" if fox else "diff_attn",
    )(q, k, vt, *extra)


def _post_mixer_kernel(ya_ref, yb_ref, ga_ref, gb_ref, x_ref, wa_ref, wb_ref, wo_ref,
                       gc_ref, wcq_ref, kc_ref, vc_ref, wco_ref, gf_ref, wr_ref, br_ref,
                       x2_ref, hm_ref, lg_ref):
    def sigmoid(t):
        return 1.0 / (1.0 + jnp.exp(-t))

    merged = (sigmoid(ga_ref[...].astype(F32)) * _dot(ya_ref[...], wa_ref[...])
              + sigmoid(gb_ref[...].astype(F32)) * _dot(yb_ref[...], wb_ref[...]))
    x1 = x_ref[...] + _dot(merged.astype(BF16), wo_ref[...])

    hx = _rms(x1, gc_ref[...], NORM_EPS).astype(BF16)
    qc = (_dot(hx, wcq_ref[...]) * QK_SCALE).astype(BF16)
    kc = kc_ref[...]
    vc = vc_ref[...]
    lane = lax.broadcasted_iota(jnp.int32, qc.shape, 1)
    zero = jnp.zeros_like(qc)
    o = jnp.zeros(qc.shape, F32)
    for h in range(CROSS_HEADS):
        in_head = (lane >= h * HEAD_DIM) & (lane < (h + 1) * HEAD_DIM)
        s = _dot_nt(jnp.where(in_head, qc, zero), kc)
        p = jnp.exp(s - jnp.max(s, axis=1, keepdims=True))
        p = p * (1.0 / jnp.sum(p, axis=1, keepdims=True))
        o = jnp.where(in_head, _dot(p.astype(BF16), vc), o)
    x2 = x1 + _dot(o.astype(BF16), wco_ref[...])
    x2_ref[...] = x2

    hm = _rms(x2, gf_ref[...], NORM_EPS)
    hm_ref[...] = hm.astype(BF16)
    hm_hi = hm.astype(BF16)
    hm_lo = (hm - hm_hi.astype(F32)).astype(BF16)
    lg_ref[...] = (_dot(hm_hi, wr_ref[0]) + _dot(hm_hi, wr_ref[1]) + _dot(hm_lo, wr_ref[0])
                   + br_ref[...])


def _post_mixer(ya, yb, ga, gb, x2d, wa, wb, wo, g_cross, wcq, kc, vc, wco, g_ffn, wr, br,
                n_batch, seq, mem_len, tm):
    n_tok, d = x2d.shape
    nblk = seq // tm
    tok = lambda b, j: (b * nblk + j, 0)
    const = lambda b, j: (0, 0)
    full = lambda a: pl.BlockSpec(a.shape, const)
    return pl.pallas_call(
        _post_mixer_kernel,
        grid=(n_batch, nblk),
        in_specs=[pl.BlockSpec((tm, ya.shape[1]), tok), pl.BlockSpec((tm, yb.shape[1]), tok),
                  pl.BlockSpec((tm, d), tok), pl.BlockSpec((tm, d), tok), pl.BlockSpec((tm, d), tok),
                  full(wa), full(wb), full(wo), full(g_cross), full(wcq),
                  pl.BlockSpec((mem_len, kc.shape[1]), lambda b, j: (b, 0)),
                  pl.BlockSpec((mem_len, vc.shape[1]), lambda b, j: (b, 0)),
                  full(wco), full(g_ffn), pl.BlockSpec(wr.shape, lambda b, j: (0, 0, 0)), full(br)],
        out_specs=[pl.BlockSpec((tm, d), tok), pl.BlockSpec((tm, d), tok),
                   pl.BlockSpec((tm, LANES), tok)],
        out_shape=[jax.ShapeDtypeStruct((n_tok, d), F32),
                   jax.ShapeDtypeStruct((n_tok, d), BF16),
                   jax.ShapeDtypeStruct((n_tok, LANES), F32)],
        compiler_params=_params("arbitrary", "arbitrary"),
        name="post_mixer",
    )(ya, yb, ga, gb, x2d, wa, wb, wo, g_cross, wcq, kc, vc, wco, g_ffn, wr, br)


def _expert_kernel(be_ref, nused_ref, xs_ref, wg_ref, wu_ref, wd_ref, ys_ref, wg_s, wu_s, wd_s):
    i = pl.program_id(0)

    @pl.when((i == 0) | (be_ref[i] != be_ref[jnp.maximum(i - 1, 0)]))
    def _():
        wg_s[...] = wg_ref[0].astype(BF16)
        wu_s[...] = wu_ref[0].astype(BF16)
        wd_s[...] = wd_ref[0].astype(BF16)

    @pl.when(i < nused_ref[0])
    def _():
        xb = xs_ref[...]
        g = _dot(xb, wg_s[...])
        u = _dot(xb, wu_s[...])
        hb = (g * (1.0 / (1.0 + jnp.exp(-g))) * u).astype(BF16)
        ys_ref[...] = _dot(hb, wd_s[...]).astype(ys_ref.dtype)

    @pl.when(i >= nused_ref[0])
    def _():
        ys_ref[...] = jnp.zeros_like(ys_ref)


def _experts(block_expert, n_used, xs, wg, wu, wd):
    n_rows, d = xs.shape
    n_blocks = n_rows // MOE_BLOCK
    ff = wg.shape[2]
    grid_spec = pltpu.PrefetchScalarGridSpec(
        num_scalar_prefetch=2,
        grid=(n_blocks,),
        in_specs=[pl.BlockSpec((MOE_BLOCK, d), lambda i, be, nu: (i, 0)),
                  pl.BlockSpec((1, d, ff), lambda i, be, nu: (be[i], 0, 0)),
                  pl.BlockSpec((1, d, ff), lambda i, be, nu: (be[i], 0, 0)),
                  pl.BlockSpec((1, ff, d), lambda i, be, nu: (be[i], 0, 0))],
        out_specs=pl.BlockSpec((MOE_BLOCK, d), lambda i, be, nu: (i, 0)),
        scratch_shapes=[pltpu.VMEM((d, ff), BF16), pltpu.VMEM((d, ff), BF16),
                        pltpu.VMEM((ff, d), BF16)],
    )
    return pl.pallas_call(
        _expert_kernel,
        grid_spec=grid_spec,
        out_shape=jax.ShapeDtypeStruct((n_rows, d), BF16),
        compiler_params=_params("arbitrary"),
        name="experts",
    )(block_expert, n_used, xs, wg, wu, wd)


def _final_kernel(x_ref, y0_ref, y1_ref, w_ref, g_ref, out_ref, *, normalize):
    w = w_ref[...]
    out = (x_ref[...] + w[:, 0:1] * y0_ref[...].astype(F32) + w[:, 1:2] * y1_ref[...].astype(F32))
    out_ref[...] = _rms(out, g_ref[...], NORM_EPS) if normalize else out


def _final(x2, y0, y1, weights, g_final, tm, normalize):
    n_tok, d = x2.shape
    tok = lambda i: (i, 0)
    return pl.pallas_call(
        functools.partial(_final_kernel, normalize=normalize),
        grid=(n_tok // tm,),
        in_specs=[pl.BlockSpec((tm, d), tok), pl.BlockSpec((tm, d), tok), pl.BlockSpec((tm, d), tok),
                  pl.BlockSpec((tm, TOP_K), tok), pl.BlockSpec((1, d), lambda i: (0, 0))],
        out_specs=pl.BlockSpec((tm, d), tok),
        out_shape=jax.ShapeDtypeStruct((n_tok, d), F32),
        compiler_params=_params("arbitrary"),
        name="final_norm",
    )(x2, y0, y1, weights, g_final)


def _route(logits, n_tok):
    group_logits = logits[:, :N_GROUPS]
    group_prob = jax.nn.softmax(group_logits, axis=-1)
    g_sel = jnp.argmax(group_logits, axis=-1)
    g_w = jnp.take_along_axis(group_prob, g_sel[:, None], axis=-1)
    exp_logits = logits[:, N_GROUPS:N_GROUPS + N_EXPERTS].reshape(n_tok, N_GROUPS, EXPERTS_PER_GROUP)
    sel_logits = jnp.take_along_axis(exp_logits, g_sel[:, None, None], axis=1)[:, 0]
    top_vals, top_idx = lax.top_k(sel_logits, TOP_K)
    weights = jax.nn.softmax(top_vals, axis=-1) * g_w
    expert_ids = g_sel[:, None].astype(jnp.int32) * EXPERTS_PER_GROUP + top_idx.astype(jnp.int32)

    n_assign = n_tok * TOP_K
    flat_e = expert_ids.reshape(n_assign)
    experts = jnp.arange(N_EXPERTS, dtype=jnp.int32)

    def lookup(table, idx):
        return jnp.sum(jnp.where(idx[:, None] == experts[None, :], table[None, :], 0), axis=1)

    order = jnp.argsort(flat_e).astype(jnp.int32)
    sorted_pos = jnp.argsort(order).astype(jnp.int32)
    seg_end = jnp.searchsorted(flat_e[order], experts, side='right').astype(jnp.int32)
    counts = seg_end - jnp.concatenate([jnp.zeros((1,), jnp.int32), seg_end[:-1]])
    seg_start = seg_end - counts
    padded = (counts + MOE_BLOCK - 1) // MOE_BLOCK * MOE_BLOCK
    pad_end = jnp.cumsum(padded)
    pad_start = pad_end - padded
    dest = sorted_pos + lookup(pad_start - seg_start, flat_e)

    n_blocks = -(-n_assign // MOE_BLOCK) + N_EXPERTS
    n_rows = n_blocks * MOE_BLOCK
    block_start = jnp.arange(n_blocks, dtype=jnp.int32) * MOE_BLOCK
    block_expert = jnp.minimum(jnp.searchsorted(pad_end, block_start, side='right'),
                               N_EXPERTS - 1).astype(jnp.int32)
    row_e = jnp.repeat(block_expert, MOE_BLOCK)
    within = jnp.arange(n_rows, dtype=jnp.int32) - lookup(pad_start, row_e)
    valid = within < lookup(counts, row_e)
    src = jnp.clip(lookup(seg_start, row_e) + within, 0, n_assign - 1)
    rows = jnp.arange(n_rows, dtype=jnp.int32)
    row_tok = jnp.where(valid, order[src] // TOP_K, rows % n_tok)
    n_used = (pad_end[-1] // MOE_BLOCK).astype(jnp.int32).reshape(1)
    return row_tok, dest, weights, block_expert, n_used


def kernel(x, mem, positions, g_mix, w_in, b_fgate, w_branch_a, w_branch_b, w_out, lambda_q1, lambda_k1, lambda_q2, lambda_k2, g_diff_sub, g_cross, g_mem, w_cq, w_ckv, w_co, g_ffn, w_group, b_group, w_expert, b_expert, w_exp_gate, w_exp_up, w_exp_down, g_final):
    n_batch, seq, d = x.shape
    mem_len = mem.shape[1]
    depth = g_mix.shape[0]
    n_tok = n_batch * seq
    fox_w = FOX_HEADS * HEAD_DIM
    diff_w = DIFF_HEADS * 2 * HEAD_DIM

    half = HEAD_DIM // 2
    inv_freq = 10000.0 ** (-jnp.arange(half, dtype=F32) * 2.0 / HEAD_DIM)
    ang = positions.astype(F32).reshape(n_tok, 1) * inv_freq
    cos_t = jnp.tile(jnp.cos(ang), (1, LANES // half))
    sin_t = jnp.tile(jnp.concatenate([-jnp.sin(ang), jnp.sin(ang)], axis=1), (1, LANES // HEAD_DIM))
    sel = _bias_lane_selectors()

    x2d = x.reshape(n_tok, d)
    mem2d = mem.reshape(n_batch * mem_len, d)
    for l in range(depth):
        lambda_init = 0.8 - 0.6 * math.exp(-0.3 * l)
        o_fv = 2 * fox_w
        o_fl = o_fv + fox_w
        o_dq = o_fl + FOX_HEADS
        o_dv = o_dq + 2 * diff_w
        o_ga = o_dv + diff_w
        wl = w_in[l]
        w_main = jnp.concatenate([wl[:, :o_fv], wl[:, o_dq:o_dv], wl[:, o_ga:]], axis=1).astype(BF16)
        w_vt = jnp.concatenate([wl[:, o_fv:o_fl], wl[:, o_dv:o_ga]], axis=1).T.astype(BF16)
        w_f = jnp.pad(wl[:, o_fl:o_dq], ((0, 0), (0, LANES - FOX_HEADS))).astype(BF16)
        b_row = jnp.pad(b_fgate[l], (0, LANES - FOX_HEADS))[None]

        kc, vc = _mem_kv(mem2d, g_mem[l][None], w_ckv[l].astype(BF16), n_batch, mem_len)
        fq, fk, dq, dk, ga, gb, fvt, dvt, fl = _in_proj(
            x2d, g_mix[l][None], w_main, w_vt, w_f, cos_t, sin_t, tm=512)
        bias_lanes = _fox_prep(fl, b_row, sel, n_batch, seq, tc=512)
        y_a = _attention(fq, fk, fvt, (bias_lanes,), n_batch, seq, fox=True)
        lam_vec = jnp.stack([lambda_q1[l], lambda_k1[l], lambda_q2[l], lambda_k2[l]])
        y_b = _attention(dq, dk, dvt, (lam_vec, g_diff_sub[l][:, None]), n_batch, seq,
                         fox=False, lambda_init=lambda_init)

        w_router = jnp.pad(jnp.concatenate([w_group[l], w_expert[l]], axis=1),
                           ((0, 0), (0, LANES - N_GROUPS - N_EXPERTS)))
        w_router_hi = w_router.astype(BF16)
        w_router = jnp.stack([w_router_hi, (w_router - w_router_hi.astype(F32)).astype(BF16)])
        b_router = jnp.pad(jnp.concatenate([b_group[l], b_expert[l]]),
                           (0, LANES - N_GROUPS - N_EXPERTS))[None]
        x2d, hm, logits = _post_mixer(
            y_a, y_b, ga, gb, x2d, w_branch_a[l].astype(BF16), w_branch_b[l].astype(BF16),
            w_out[l].astype(BF16), g_cross[l][None], w_cq[l].astype(BF16), kc, vc,
            w_co[l].astype(BF16), g_ffn[l][None], w_router, b_router,
            n_batch, seq, mem_len, tm=512)

        row_tok, dest, weights, block_expert, n_used = _route(logits, n_tok)
        xs = hm[row_tok]
        ys = _experts(block_expert, n_used, xs, w_exp_gate[l], w_exp_up[l], w_exp_down[l])
        dest = dest.reshape(n_tok, TOP_K)
        last = l + 1 == depth
        x2d = _final(x2d, ys[dest[:, 0]], ys[dest[:, 1]], weights, g_final[None], tm=512,
                     normalize=last)
    return x2d.reshape(n_batch, seq, d)
```

```python
import functools
import math

import numpy as np
import jax
import jax.numpy as jnp
from jax import lax
from jax.experimental import pallas as pl
from jax.experimental.pallas import tpu as pltpu

HEAD_DIM = 64
LANES = 128
FOX_HEADS = 8
DIFF_HEADS = 4
CROSS_HEADS = 4
N_GROUPS = 4
EXPERTS_PER_GROUP = 8
N_EXPERTS = N_GROUPS * EXPERTS_PER_GROUP
TOP_K = 2
MOE_BLOCK = 512
ATTN_Q_BLOCK = 512
ATTN_K_BLOCK = 256
NORM_EPS = 1e-6
SUBLN_EPS = 1e-5
QK_SCALE = HEAD_DIM ** -0.5
LOG2E = math.log2(math.e)
Q_PRESCALE = QK_SCALE * LOG2E
NEG_BIG = -1e30
VMEM_LIMIT = 56 * 2**20

BF16 = jnp.bfloat16
F32 = jnp.float32


def _rms(t, g, eps):
    return t * lax.rsqrt(jnp.mean(t * t, axis=-1, keepdims=True) + eps) * g


def _dot(a, b):
    return jnp.dot(a, b, preferred_element_type=F32)


def _dot_nt(a, b):
    return lax.dot_general(a, b, (((1,), (1,)), ((), ())), preferred_element_type=F32)


def _params(*sem):
    return pltpu.CompilerParams(dimension_semantics=sem, vmem_limit_bytes=VMEM_LIMIT)


def _mem_kv_kernel(mem_ref, g_ref, w_ref, k_ref, v_ref):
    h = _rms(mem_ref[...], g_ref[...], NORM_EPS).astype(BF16)
    kv = _dot(h, w_ref[...])
    width = k_ref.shape[1]
    k_ref[...] = kv[:, :width].astype(BF16)
    v_ref[...] = kv[:, width:].astype(BF16)


def _mem_kv(mem2, g_mem, w_ckv, n_batch, mem_len):
    d = mem2.shape[1]
    cw = w_ckv.shape[1] // 2
    out = jax.ShapeDtypeStruct((n_batch * mem_len, cw), BF16)
    return pl.pallas_call(
        _mem_kv_kernel,
        grid=(n_batch,),
        in_specs=[pl.BlockSpec((mem_len, d), lambda b: (b, 0)),
                  pl.BlockSpec((1, d), lambda b: (0, 0)),
                  pl.BlockSpec((d, 2 * cw), lambda b: (0, 0))],
        out_specs=[pl.BlockSpec((mem_len, cw), lambda b: (b, 0))] * 2,
        out_shape=[out, out],
        compiler_params=_params("arbitrary"),
        name="mem_kv",
    )(mem2, g_mem, w_ckv)


def _rope(t, cos, sin_signed, first_half):
    fwd = pltpu.roll(t, LANES - HEAD_DIM // 2, axis=1)
    bwd = pltpu.roll(t, HEAD_DIM // 2, axis=1)
    return t * cos + jnp.where(first_half, fwd, bwd) * sin_signed


def _in_proj_kernel(x_ref, g_ref, w_ref, wvt_ref, wf_ref, cos_ref, sin_ref,
                    fq_ref, fk_ref, dq_ref, dk_ref, ga_ref, gb_ref, fvt_ref, dvt_ref, fl_ref):
    h = _rms(x_ref[...], g_ref[...], NORM_EPS).astype(BF16)
    tm = h.shape[0]
    cw = fq_ref.shape[1]

    def proj(chunk):
        return _dot(h, w_ref[:, chunk * cw:(chunk + 1) * cw])

    fq_ref[...] = (proj(0) * Q_PRESCALE).astype(BF16)
    fk_ref[...] = proj(1).astype(BF16)

    cos = cos_ref[...]
    sin = sin_ref[...]
    lane = lax.broadcasted_iota(jnp.int32, (tm, LANES), 1)
    first_half = (lane % HEAD_DIM) < HEAD_DIM // 2
    for out_ref, chunk, scale in ((dq_ref, 2, Q_PRESCALE), (dk_ref, 3, 1.0)):
        t = proj(chunk)
        for c in range(cw // LANES):
            blk = _rope(t[:, c * LANES:(c + 1) * LANES], cos, sin, first_half)
            out_ref[:, c * LANES:(c + 1) * LANES] = (blk * scale).astype(BF16)

    for out_ref, chunk in ((ga_ref, 4), (gb_ref, 6)):
        for c in range(2):
            out_ref[:, c * cw:(c + 1) * cw] = proj(chunk + c).astype(BF16)
    fvt_ref[...] = _dot_nt(wvt_ref[0:cw, :], h).astype(BF16)
    dvt_ref[...] = _dot_nt(wvt_ref[cw:2 * cw, :], h).astype(BF16)
    fl_ref[...] = _dot(h, wf_ref[...])


def _in_proj(x2, g_mix, w_main, w_vt, w_f, cos_t, sin_t, tm):
    n_tok, d = x2.shape
    cw = 512
    tok = lambda i: (i, 0)
    tok_t = lambda i: (0, i)
    const = lambda i: (0, 0)
    o512 = jax.ShapeDtypeStruct((n_tok, cw), BF16)
    o1024 = jax.ShapeDtypeStruct((n_tok, 2 * cw), BF16)
    o512_t = jax.ShapeDtypeStruct((cw, n_tok), BF16)
    return pl.pallas_call(
        _in_proj_kernel,
        grid=(n_tok // tm,),
        in_specs=[pl.BlockSpec((tm, d), tok),
                  pl.BlockSpec((1, d), const),
                  pl.BlockSpec(w_main.shape, const, pipeline_mode=pl.Buffered(1)),
                  pl.BlockSpec(w_vt.shape, const, pipeline_mode=pl.Buffered(1)),
                  pl.BlockSpec(w_f.shape, const),
                  pl.BlockSpec((tm, LANES), tok),
                  pl.BlockSpec((tm, LANES), tok)],
        out_specs=[pl.BlockSpec((tm, cw), tok)] * 4
                  + [pl.BlockSpec((tm, 2 * cw), tok)] * 2
                  + [pl.BlockSpec((cw, tm), tok_t)] * 2
                  + [pl.BlockSpec((tm, LANES), tok)],
        out_shape=[o512] * 4 + [o1024] * 2 + [o512_t] * 2
                  + [jax.ShapeDtypeStruct((n_tok, LANES), F32)],
        compiler_params=_params("arbitrary"),
        name="in_proj",
    )(x2, g_mix, w_main, w_vt, w_f, cos_t, sin_t)


def _bias_lane_selectors():
    sel = np.zeros((3, LANES, FOX_HEADS // 2 * LANES), np.float32)
    for h in range(FOX_HEADS):
        base = (h // 2) * LANES + (HEAD_DIM if h % 2 == 0 else 0)
        for piece in range(3):
            sel[piece, h, base + piece] = 1.0
    return jnp.asarray(sel, BF16)


def _split3(t):
    pieces = []
    for _ in range(3):
        part = t.astype(BF16)
        t = t - part.astype(F32)
        pieces.append(part)
    return pieces


def _fox_prep_kernel(fl_ref, b_ref, tri_ref, sel_ref, out_ref, carry_ref):
    @pl.when(pl.program_id(1) == 0)
    def _():
        carry_ref[...] = jnp.zeros_like(carry_ref)

    z = fl_ref[...] + b_ref[...]
    log_f = jnp.minimum(z, 0.0) - jnp.log(1.0 + jnp.exp(-jnp.abs(z)))
    tc = z.shape[0]
    c = carry_ref[...]
    for part in _split3(log_f):
        c = c + _dot(tri_ref[...], part)
    carry_ref[...] = c[tc - 1:tc, :]
    acc = jnp.zeros(out_ref.shape, F32)
    for piece, part in enumerate(_split3(c * (-LOG2E))):
        acc = acc + _dot(part, sel_ref[piece])
    out_ref[...] = acc.astype(BF16)


def _fox_prep(fl, b_row, sel, n_batch, seq, tc):
    n_tok = fl.shape[0]
    nblk = seq // tc
    width = sel.shape[2]
    tri = jnp.tril(jnp.ones((tc, tc), BF16))
    return pl.pallas_call(
        _fox_prep_kernel,
        grid=(n_batch, nblk),
        in_specs=[pl.BlockSpec((tc, LANES), lambda b, j: (b * nblk + j, 0)),
                  pl.BlockSpec((1, LANES), lambda b, j: (0, 0)),
                  pl.BlockSpec((tc, tc), lambda b, j: (0, 0)),
                  pl.BlockSpec(sel.shape, lambda b, j: (0, 0, 0))],
        out_specs=pl.BlockSpec((tc, width), lambda b, j: (b * nblk + j, 0)),
        out_shape=jax.ShapeDtypeStruct((n_tok, width), BF16),
        scratch_shapes=[pltpu.VMEM((1, LANES), F32)],
        compiler_params=_params("arbitrary", "arbitrary"),
        name="fox_prep",
    )(fl, b_row, tri, sel)


def _attn_kernel(*refs, bq, bk, fox, lambda_init):
    if fox:
        q_ref, k_ref, vt_ref, bl_ref, out_ref, qs_ref, st_ref, p_ref, acc_ref, stat_ref, ka_ref, kb_ref = refs
    else:
        q_ref, k_ref, vt_ref, lam_ref, gsub_ref, out_ref, qs_ref, st_ref, p_ref, acc_ref, stat_ref = refs
    qi = pl.program_id(2)

    q = q_ref[...].astype(F32)
    lane = lax.broadcasted_iota(jnp.int32, q.shape, 1)
    if fox:
        @pl.when(qi == 0)
        def _():
            k = k_ref[...].astype(F32)
            bl = bl_ref[...].astype(F32)
            klane = lax.broadcasted_iota(jnp.int32, k.shape, 1)
            ka_ref[...] = jnp.where(klane < HEAD_DIM, k, bl).astype(BF16)
            kb_ref[...] = jnp.where(klane >= HEAD_DIM, k, bl).astype(BF16)

        q_streams = (jnp.where(lane < HEAD_DIM, q, jnp.where(lane < HEAD_DIM + 3, 1.0, 0.0)),
                     jnp.where(lane >= HEAD_DIM, q, jnp.where(lane < 3, 1.0, 0.0)))
        k_srcs = (ka_ref, kb_ref)
        v_rows = ((0, HEAD_DIM), (HEAD_DIM, 2 * HEAD_DIM))
    else:
        q_streams = (jnp.where(lane < HEAD_DIM, q, 0.0), jnp.where(lane >= HEAD_DIM, q, 0.0))
        k_srcs = (k_ref, k_ref)
        v_rows = ((0, 2 * HEAD_DIM), (0, 2 * HEAD_DIM))
    n_v = v_rows[0][1] - v_rows[0][0]
    assert bq == 2 * bk
    for idx in range(2):
        qs_ref[idx] = q_streams[idx].astype(BF16)
        p_ref[idx, 1] = jnp.zeros((bk, bq), BF16)
        acc_ref[idx] = jnp.zeros((n_v + 16, bq), F32)
    ones_rows = jnp.ones((16, bk), BF16)
    first_diag = 2 * qi

    def qk_stage(kj, idx, slot):
        k0 = pl.multiple_of(kj * bk, bk)
        st_ref[idx, slot] = _dot_nt(k_srcs[idx][pl.ds(k0, bk), :], qs_ref[idx])

    M_ROW, A_ROW = 0, 8

    def pv_stage(kj, idx, slot):
        k0 = pl.multiple_of(kj * bk, bk)
        r0, r1 = v_rows[idx]
        v_aug = jnp.concatenate([vt_ref[r0:r1, pl.ds(k0, bk)], ones_rows], axis=0)
        alpha = stat_ref[idx, A_ROW + 8 * slot:A_ROW + 8 * slot + 1, :]
        acc_ref[idx] = alpha * acc_ref[idx] + _dot(v_aug, p_ref[idx, slot])

    def softmax_stage(idx, slot, diag):
        st = st_ref[idx, slot]
        if diag is not None:
            key_idx = lax.broadcasted_iota(jnp.int32, (bk, bq), 0) + diag * bk
            qry_idx = lax.broadcasted_iota(jnp.int32, (bk, bq), 1)
            st = jnp.where(key_idx <= qry_idx, st, NEG_BIG)
        m = stat_ref[idx, M_ROW:M_ROW + 1, :]
        m_new = jnp.maximum(m, jnp.max(st, axis=0, keepdims=True))
        p_ref[idx, slot] = jnp.exp2(st - m_new).astype(BF16)
        stat_ref[idx, M_ROW:M_ROW + 1, :] = m_new
        stat_ref[idx, A_ROW + 8 * slot:A_ROW + 8 * slot + 1, :] = jnp.exp2(m - m_new)

    for idx in range(2):
        stat_ref[idx, M_ROW:M_ROW + 8, :] = jnp.full((8, bq), NEG_BIG, F32)
        stat_ref[idx, A_ROW:A_ROW + 16, :] = jnp.ones((16, bq), F32)
        qk_stage(0, idx, 0)

    def block_pair(j, diag_pair):
        for slot in range(2):
            for idx in range(2):
                pv_stage(jnp.maximum(j + slot - 1, 0), idx, 1 - slot)
                softmax_stage(idx, slot, slot if diag_pair else None)
                if not (diag_pair and slot == 1):
                    qk_stage(j + slot + 1, idx, 1 - slot)

    def body(i, carry):
        block_pair(2 * i, False)
        return carry

    lax.fori_loop(0, qi, body, 0)
    block_pair(first_diag, True)
    outs = []
    for idx in range(2):
        pv_stage(first_diag + 1, idx, 1)
        acc = acc_ref[idx]
        outs.append(acc[0:n_v] * (1.0 / acc[n_v:n_v + 1]))
    o_a, o_b = outs
    if fox:
        y = jnp.concatenate([o_a, o_b], axis=0)
    else:
        lam_vec = lam_ref[...]
        lam = (jnp.exp(jnp.sum(lam_vec[0:1] * lam_vec[1:2], axis=1, keepdims=True))
               - jnp.exp(jnp.sum(lam_vec[2:3] * lam_vec[3:4], axis=1, keepdims=True))
               + lambda_init)
        y = o_a - lam * o_b
        y = y * lax.rsqrt(jnp.mean(y * y, axis=0, keepdims=True) + SUBLN_EPS)
        y = y * (gsub_ref[...] * (1.0 - lambda_init))
    out_ref[...] = y.T.astype(out_ref.dtype)


def _attention(q, k, vt, extra, n_batch, seq, fox, lambda_init=0.0):
    n_tok, width = q.shape
    n_pairs = width // LANES
    bq, bk = ATTN_Q_BLOCK, ATTN_K_BLOCK
    nq = seq // bq
    n_v = HEAD_DIM if fox else 2 * HEAD_DIM
    q_spec = pl.BlockSpec((bq, LANES), lambda b, hp, qi: (b * nq + qi, hp))
    k_spec = pl.BlockSpec((seq, LANES), lambda b, hp, qi: (b, hp))
    vt_spec = pl.BlockSpec((LANES, seq), lambda b, hp, qi: (hp, b))
    scratch = [pltpu.VMEM((2, bq, LANES), BF16),
               pltpu.VMEM((2, 2, bk, bq), F32),
               pltpu.VMEM((2, 2, bk, bq), BF16),
               pltpu.VMEM((2, n_v + 16, bq), F32),
               pltpu.VMEM((2, 24, bq), F32)]
    if fox:
        extra_specs = [k_spec]
        scratch += [pltpu.VMEM((seq, LANES), BF16)] * 2
    else:
        lam_vec, g_sub = extra
        extra_specs = [pl.BlockSpec(lam_vec.shape, lambda b, hp, qi: (0, 0)),
                       pl.BlockSpec(g_sub.shape, lambda b, hp, qi: (0, 0))]
    return pl.pallas_call(
        functools.partial(_attn_kernel, bq=bq, bk=bk, fox=fox, lambda_init=lambda_init),
        grid=(n_batch, n_pairs, nq),
        in_specs=[q_spec, k_spec, vt_spec] + extra_specs,
        out_specs=q_spec,
        out_shape=jax.ShapeDtypeStruct((n_tok, width), BF16),
        scratch_shapes=scratch,
        compiler_params=_params("arbitrary", "arbitrary", "arbitrary"),
        name="fox_attn" if fox else "diff_attn",
    )(q, k, vt, *extra)


def _route_block(logits, tri_ref, cnt_ref):
    lane = lax.broadcasted_iota(jnp.int32, logits.shape, 1)
    lane_f = lane.astype(F32)
    last = float(LANES - 1)

    def first_argmax(t):
        top = jnp.max(t, axis=1, keepdims=True)
        return top, jnp.min(jnp.where(t == top, lane_f, last), axis=1, keepdims=True)

    is_group = lane < N_GROUPS
    g_max, g_sel = first_argmax(jnp.where(is_group, logits, NEG_BIG))
    g_w = 1.0 / jnp.sum(jnp.where(is_group, jnp.exp(logits - g_max), 0.0), axis=1, keepdims=True)
    lo = N_GROUPS + EXPERTS_PER_GROUP * g_sel
    scores = jnp.where((lane_f >= lo) & (lane_f < lo + EXPERTS_PER_GROUP), logits, NEG_BIG)
    v1, i1 = first_argmax(scores)
    v2, i2 = first_argmax(jnp.where(lane_f == i1, NEG_BIG, scores))
    t = jnp.exp(v2 - v1)
    w1 = g_w / (1.0 + t)
    w2 = w1 * t
    e1 = i1 - N_GROUPS
    e2 = i2 - N_GROUPS

    hit1 = lane_f == e1
    hit2 = lane_f == e2
    onehot = jnp.where(hit1 | hit2, 1.0, 0.0)
    before = _dot(tri_ref[...], onehot.astype(BF16)) + cnt_ref[...]
    r1 = jnp.sum(jnp.where(hit1, before, 0.0), axis=1, keepdims=True)
    r2 = jnp.sum(jnp.where(hit2, before, 0.0), axis=1, keepdims=True)
    n = logits.shape[0]
    cnt_ref[...] = before[n - 1:n, :] + onehot[n - 1:n, :]
    out = jnp.zeros(logits.shape, F32)
    for pos, val in enumerate((e1, e2, r1, r2, w1, w2)):
        out = jnp.where(lane == pos, val, out)
    return out


def _post_mixer_kernel(ya_ref, yb_ref, ga_ref, gb_ref, x_ref, wa_ref, wb_ref, wo_ref,
                       gc_ref, wcq_ref, kc_ref, vc_ref, wco_ref, gf_ref, wr_ref, br_ref, tri_ref,
                       x2_ref, hm_ref, route_ref, cnt_out_ref, cnt_ref):
    @pl.when((pl.program_id(0) == 0) & (pl.program_id(1) == 0))
    def _():
        cnt_ref[...] = jnp.zeros_like(cnt_ref)

    def sigmoid(t):
        return 1.0 / (1.0 + jnp.exp(-t))

    merged = (sigmoid(ga_ref[...].astype(F32)) * _dot(ya_ref[...], wa_ref[...])
              + sigmoid(gb_ref[...].astype(F32)) * _dot(yb_ref[...], wb_ref[...]))
    x1 = x_ref[...] + _dot(merged.astype(BF16), wo_ref[...])

    hx = _rms(x1, gc_ref[...], NORM_EPS).astype(BF16)
    qc = (_dot(hx, wcq_ref[...]) * QK_SCALE).astype(BF16)
    kc = kc_ref[...]
    vc = vc_ref[...]
    lane = lax.broadcasted_iota(jnp.int32, qc.shape, 1)
    zero = jnp.zeros_like(qc)
    o = jnp.zeros(qc.shape, F32)
    for h in range(CROSS_HEADS):
        in_head = (lane >= h * HEAD_DIM) & (lane < (h + 1) * HEAD_DIM)
        s = _dot_nt(jnp.where(in_head, qc, zero), kc)
        p = jnp.exp(s - jnp.max(s, axis=1, keepdims=True))
        p = p * (1.0 / jnp.sum(p, axis=1, keepdims=True))
        o = jnp.where(in_head, _dot(p.astype(BF16), vc), o)
    x2 = x1 + _dot(o.astype(BF16), wco_ref[...])
    x2_ref[...] = x2

    hm = _rms(x2, gf_ref[...], NORM_EPS)
    hm_ref[...] = hm.astype(BF16)
    hm_hi = hm.astype(BF16)
    hm_lo = (hm - hm_hi.astype(F32)).astype(BF16)
    logits = (_dot(hm_hi, wr_ref[0]) + _dot(hm_hi, wr_ref[1]) + _dot(hm_lo, wr_ref[0])
              + br_ref[...])
    route_ref[...] = _route_block(logits, tri_ref, cnt_ref)
    cnt_out_ref[...] = jnp.broadcast_to(cnt_ref[...], cnt_out_ref.shape)


def _post_mixer(ya, yb, ga, gb, x2d, wa, wb, wo, g_cross, wcq, kc, vc, wco, g_ffn, wr, br,
                n_batch, seq, mem_len, tm):
    n_tok, d = x2d.shape
    nblk = seq // tm
    tok = lambda b, j: (b * nblk + j, 0)
    const = lambda b, j: (0, 0)
    full = lambda a: pl.BlockSpec(a.shape, const)
    tri = jnp.tril(jnp.ones((tm, tm), BF16), -1)
    return pl.pallas_call(
        _post_mixer_kernel,
        grid=(n_batch, nblk),
        in_specs=[pl.BlockSpec((tm, ya.shape[1]), tok), pl.BlockSpec((tm, yb.shape[1]), tok),
                  pl.BlockSpec((tm, d), tok), pl.BlockSpec((tm, d), tok), pl.BlockSpec((tm, d), tok),
                  full(wa), full(wb), full(wo), full(g_cross), full(wcq),
                  pl.BlockSpec((mem_len, kc.shape[1]), lambda b, j: (b, 0)),
                  pl.BlockSpec((mem_len, vc.shape[1]), lambda b, j: (b, 0)),
                  full(wco), full(g_ffn), pl.BlockSpec(wr.shape, lambda b, j: (0, 0, 0)), full(br),
                  full(tri)],
        out_specs=[pl.BlockSpec((tm, d), tok), pl.BlockSpec((tm, d), tok),
                   pl.BlockSpec((tm, LANES), tok), pl.BlockSpec((8, LANES), const)],
        out_shape=[jax.ShapeDtypeStruct((n_tok, d), F32),
                   jax.ShapeDtypeStruct((n_tok, d), BF16),
                   jax.ShapeDtypeStruct((n_tok, LANES), F32),
                   jax.ShapeDtypeStruct((8, LANES), F32)],
        scratch_shapes=[pltpu.VMEM((1, LANES), F32)],
        compiler_params=_params("arbitrary", "arbitrary"),
        name="post_mixer",
    )(ya, yb, ga, gb, x2d, wa, wb, wo, g_cross, wcq, kc, vc, wco, g_ffn, wr, br, tri)


def _expert_kernel(be_ref, nused_ref, xs_ref, wg_ref, wu_ref, wd_ref, ys_ref, wg_s, wu_s, wd_s):
    i = pl.program_id(0)

    @pl.when((i == 0) | (be_ref[i] != be_ref[jnp.maximum(i - 1, 0)]))
    def _():
        wg_s[...] = wg_ref[0].astype(BF16)
        wu_s[...] = wu_ref[0].astype(BF16)
        wd_s[...] = wd_ref[0].astype(BF16)

    @pl.when(i < nused_ref[0])
    def _():
        xb = xs_ref[...]
        g = _dot(xb, wg_s[...])
        u = _dot(xb, wu_s[...])
        hb = (g * (1.0 / (1.0 + jnp.exp(-g))) * u).astype(BF16)
        ys_ref[...] = _dot(hb, wd_s[...]).astype(ys_ref.dtype)

    @pl.when(i >= nused_ref[0])
    def _():
        ys_ref[...] = jnp.zeros_like(ys_ref)


def _experts(block_expert, n_used, xs, wg, wu, wd):
    n_rows, d = xs.shape
    n_blocks = n_rows // MOE_BLOCK
    ff = wg.shape[2]
    grid_spec = pltpu.PrefetchScalarGridSpec(
        num_scalar_prefetch=2,
        grid=(n_blocks,),
        in_specs=[pl.BlockSpec((MOE_BLOCK, d), lambda i, be, nu: (i, 0)),
                  pl.BlockSpec((1, d, ff), lambda i, be, nu: (be[i], 0, 0)),
                  pl.BlockSpec((1, d, ff), lambda i, be, nu: (be[i], 0, 0)),
                  pl.BlockSpec((1, ff, d), lambda i, be, nu: (be[i], 0, 0))],
        out_specs=pl.BlockSpec((MOE_BLOCK, d), lambda i, be, nu: (i, 0)),
        scratch_shapes=[pltpu.VMEM((d, ff), BF16), pltpu.VMEM((d, ff), BF16),
                        pltpu.VMEM((ff, d), BF16)],
    )
    return pl.pallas_call(
        _expert_kernel,
        grid_spec=grid_spec,
        out_shape=jax.ShapeDtypeStruct((n_rows, d), BF16),
        compiler_params=_params("arbitrary"),
        name="experts",
    )(block_expert, n_used, xs, wg, wu, wd)


def _final_kernel(x_ref, y0_ref, y1_ref, w_ref, g_ref, out_ref, *, normalize):
    w = w_ref[...]
    out = (x_ref[...] + w[:, 0:1] * y0_ref[...].astype(F32) + w[:, 1:2] * y1_ref[...].astype(F32))
    out_ref[...] = _rms(out, g_ref[...], NORM_EPS) if normalize else out


def _final(x2, y0, y1, weights, g_final, tm, normalize):
    n_tok, d = x2.shape
    tok = lambda i: (i, 0)
    return pl.pallas_call(
        functools.partial(_final_kernel, normalize=normalize),
        grid=(n_tok // tm,),
        in_specs=[pl.BlockSpec((tm, d), tok), pl.BlockSpec((tm, d), tok), pl.BlockSpec((tm, d), tok),
                  pl.BlockSpec((tm, TOP_K), tok), pl.BlockSpec((1, d), lambda i: (0, 0))],
        out_specs=pl.BlockSpec((tm, d), tok),
        out_shape=jax.ShapeDtypeStruct((n_tok, d), F32),
        compiler_params=_params("arbitrary"),
        name="final_norm",
    )(x2, y0, y1, weights, g_final)


def _route(route, counts, n_tok):
    n_assign = n_tok * TOP_K
    flat_e = route[:, 0:TOP_K].astype(jnp.int32).reshape(n_assign)
    rank = route[:, TOP_K:2 * TOP_K].astype(jnp.int32).reshape(n_assign)
    weights = route[:, 2 * TOP_K:3 * TOP_K]
    counts = counts[0, :N_EXPERTS].astype(jnp.int32)
    experts = jnp.arange(N_EXPERTS, dtype=jnp.int32)

    def lookup(table, idx):
        return jnp.sum(jnp.where(idx[:, None] == experts[None, :], table[None, :], 0), axis=1)

    seg_start = jnp.cumsum(counts) - counts
    padded = (counts + MOE_BLOCK - 1) // MOE_BLOCK * MOE_BLOCK
    pad_end = jnp.cumsum(padded)
    pad_start = pad_end - padded
    dest = lookup(pad_start, flat_e) + rank
    order = jnp.argsort(lookup(seg_start, flat_e) + rank).astype(jnp.int32)

    n_blocks = -(-n_assign // MOE_BLOCK) + N_EXPERTS
    n_rows = n_blocks * MOE_BLOCK
    block_start = jnp.arange(n_blocks, dtype=jnp.int32) * MOE_BLOCK
    block_expert = jnp.minimum(jnp.searchsorted(pad_end, block_start, side='right'),
                               N_EXPERTS - 1).astype(jnp.int32)
    row_e = jnp.repeat(block_expert, MOE_BLOCK)
    within = jnp.arange(n_rows, dtype=jnp.int32) - lookup(pad_start, row_e)
    valid = within < lookup(counts, row_e)
    src = jnp.clip(lookup(seg_start, row_e) + within, 0, n_assign - 1)
    rows = jnp.arange(n_rows, dtype=jnp.int32)
    row_tok = jnp.where(valid, order[src] // TOP_K, rows % n_tok)
    n_used = (pad_end[-1] // MOE_BLOCK).astype(jnp.int32).reshape(1)
    return row_tok, dest, weights, block_expert, n_used


def kernel(x, mem, positions, g_mix, w_in, b_fgate, w_branch_a, w_branch_b, w_out, lambda_q1, lambda_k1, lambda_q2, lambda_k2, g_diff_sub, g_cross, g_mem, w_cq, w_ckv, w_co, g_ffn, w_group, b_group, w_expert, b_expert, w_exp_gate, w_exp_up, w_exp_down, g_final):
    n_batch, seq, d = x.shape
    mem_len = mem.shape[1]
    depth = g_mix.shape[0]
    n_tok = n_batch * seq
    fox_w = FOX_HEADS * HEAD_DIM
    diff_w = DIFF_HEADS * 2 * HEAD_DIM

    half = HEAD_DIM // 2
    inv_freq = 10000.0 ** (-jnp.arange(half, dtype=F32) * 2.0 / HEAD_DIM)
    ang = positions.astype(F32).reshape(n_tok, 1) * inv_freq
    cos_t = jnp.tile(jnp.cos(ang), (1, LANES // half))
    sin_t = jnp.tile(jnp.concatenate([-jnp.sin(ang), jnp.sin(ang)], axis=1), (1, LANES // HEAD_DIM))
    sel = _bias_lane_selectors()

    x2d = x.reshape(n_tok, d)
    mem2d = mem.reshape(n_batch * mem_len, d)
    for l in range(depth):
        lambda_init = 0.8 - 0.6 * math.exp(-0.3 * l)
        o_fv = 2 * fox_w
        o_fl = o_fv + fox_w
        o_dq = o_fl + FOX_HEADS
        o_dv = o_dq + 2 * diff_w
        o_ga = o_dv + diff_w
        wl = w_in[l]
        w_main = jnp.concatenate([wl[:, :o_fv], wl[:, o_dq:o_dv], wl[:, o_ga:]], axis=1).astype(BF16)
        w_vt = jnp.concatenate([wl[:, o_fv:o_fl], wl[:, o_dv:o_ga]], axis=1).T.astype(BF16)
        w_f = jnp.pad(wl[:, o_fl:o_dq], ((0, 0), (0, LANES - FOX_HEADS))).astype(BF16)
        b_row = jnp.pad(b_fgate[l], (0, LANES - FOX_HEADS))[None]

        kc, vc = _mem_kv(mem2d, g_mem[l][None], w_ckv[l].astype(BF16), n_batch, mem_len)
        fq, fk, dq, dk, ga, gb, fvt, dvt, fl = _in_proj(
            x2d, g_mix[l][None], w_main, w_vt, w_f, cos_t, sin_t, tm=512)
        bias_lanes = _fox_prep(fl, b_row, sel, n_batch, seq, tc=512)
        y_a = _attention(fq, fk, fvt, (bias_lanes,), n_batch, seq, fox=True)
        lam_vec = jnp.stack([lambda_q1[l], lambda_k1[l], lambda_q2[l], lambda_k2[l]])
        y_b = _attention(dq, dk, dvt, (lam_vec, g_diff_sub[l][:, None]), n_batch, seq,
                         fox=False, lambda_init=lambda_init)

        w_router = jnp.pad(jnp.concatenate([w_group[l], w_expert[l]], axis=1),
                           ((0, 0), (0, LANES - N_GROUPS - N_EXPERTS)))
        w_router_hi = w_router.astype(BF16)
        w_router = jnp.stack([w_router_hi, (w_router - w_router_hi.astype(F32)).astype(BF16)])
        b_router = jnp.pad(jnp.concatenate([b_group[l], b_expert[l]]),
                           (0, LANES - N_GROUPS - N_EXPERTS))[None]
        x2d, hm, route, counts = _post_mixer(
            y_a, y_b, ga, gb, x2d, w_branch_a[l].astype(BF16), w_branch_b[l].astype(BF16),
            w_out[l].astype(BF16), g_cross[l][None], w_cq[l].astype(BF16), kc, vc,
            w_co[l].astype(BF16), g_ffn[l][None], w_router, b_router,
            n_batch, seq, mem_len, tm=512)

        row_tok, dest, weights, block_expert, n_used = _route(route, counts, n_tok)
        xs = hm[row_tok]
        ys = _experts(block_expert, n_used, xs, w_exp_gate[l], w_exp_up[l], w_exp_down[l])
        dest = dest.reshape(n_tok, TOP_K)
        last = l + 1 == depth
        x2d = _final(x2d, ys[dest[:, 0]], ys[dest[:, 1]], weights, g_final[None], tm=512,
                     normalize=last)
    return x2d.reshape(n_batch, seq, d)
```

```python
import functools
import math

import numpy as np
import jax
import jax.numpy as jnp
from jax import lax
from jax.experimental import pallas as pl
from jax.experimental.pallas import tpu as pltpu

HEAD_DIM = 64
LANES = 128
FOX_HEADS = 8
DIFF_HEADS = 4
CROSS_HEADS = 4
N_GROUPS = 4
EXPERTS_PER_GROUP = 8
N_EXPERTS = N_GROUPS * EXPERTS_PER_GROUP
TOP_K = 2
MOE_BLOCK = 512
ATTN_Q_BLOCK = 1024
ATTN_K_BLOCK = 256
NORM_EPS = 1e-6
SUBLN_EPS = 1e-5
QK_SCALE = HEAD_DIM ** -0.5
LOG2E = math.log2(math.e)
Q_PRESCALE = QK_SCALE * LOG2E
NEG_BIG = -1e30
VMEM_LIMIT = 56 * 2**20

BF16 = jnp.bfloat16
F32 = jnp.float32


def _rms(t, g, eps):
    return t * lax.rsqrt(jnp.mean(t * t, axis=-1, keepdims=True) + eps) * g


def _dot(a, b):
    return jnp.dot(a, b, preferred_element_type=F32)


def _dot_nt(a, b):
    return lax.dot_general(a, b, (((1,), (1,)), ((), ())), preferred_element_type=F32)


def _params(*sem):
    return pltpu.CompilerParams(dimension_semantics=sem, vmem_limit_bytes=VMEM_LIMIT)


def _mem_kv_kernel(mem_ref, g_ref, w_ref, k_ref, v_ref):
    h = _rms(mem_ref[...], g_ref[...], NORM_EPS).astype(BF16)
    kv = _dot(h, w_ref[...])
    width = k_ref.shape[1]
    k_ref[...] = kv[:, :width].astype(BF16)
    v_ref[...] = kv[:, width:].astype(BF16)


def _mem_kv(mem2, g_mem, w_ckv, n_batch, mem_len):
    d = mem2.shape[1]
    cw = w_ckv.shape[1] // 2
    out = jax.ShapeDtypeStruct((n_batch * mem_len, cw), BF16)
    return pl.pallas_call(
        _mem_kv_kernel,
        grid=(n_batch,),
        in_specs=[pl.BlockSpec((mem_len, d), lambda b: (b, 0)),
                  pl.BlockSpec((1, d), lambda b: (0, 0)),
                  pl.BlockSpec((d, 2 * cw), lambda b: (0, 0))],
        out_specs=[pl.BlockSpec((mem_len, cw), lambda b: (b, 0))] * 2,
        out_shape=[out, out],
        compiler_params=_params("arbitrary"),
        name="mem_kv",
    )(mem2, g_mem, w_ckv)


def _rope(t, cos, sin_signed, first_half):
    fwd = pltpu.roll(t, LANES - HEAD_DIM // 2, axis=1)
    bwd = pltpu.roll(t, HEAD_DIM // 2, axis=1)
    return t * cos + jnp.where(first_half, fwd, bwd) * sin_signed


def _in_proj_kernel(x_ref, g_ref, w_ref, wvt_ref, wf_ref, cos_ref, sin_ref,
                    fq_ref, fk_ref, dq_ref, dk_ref, ga_ref, gb_ref, fvt_ref, dvt_ref, fl_ref):
    h = _rms(x_ref[...], g_ref[...], NORM_EPS).astype(BF16)
    tm = h.shape[0]
    cw = fq_ref.shape[1]

    def proj(chunk):
        return _dot(h, w_ref[:, chunk * cw:(chunk + 1) * cw])

    fq_ref[...] = (proj(0) * Q_PRESCALE).astype(BF16)
    fk_ref[...] = proj(1).astype(BF16)

    cos = cos_ref[...]
    sin = sin_ref[...]
    lane = lax.broadcasted_iota(jnp.int32, (tm, LANES), 1)
    first_half = (lane % HEAD_DIM) < HEAD_DIM // 2
    for out_ref, chunk, scale in ((dq_ref, 2, Q_PRESCALE), (dk_ref, 3, 1.0)):
        t = proj(chunk)
        for c in range(cw // LANES):
            blk = _rope(t[:, c * LANES:(c + 1) * LANES], cos, sin, first_half)
            out_ref[:, c * LANES:(c + 1) * LANES] = (blk * scale).astype(BF16)

    for out_ref, chunk in ((ga_ref, 4), (gb_ref, 6)):
        for c in range(2):
            out_ref[:, c * cw:(c + 1) * cw] = proj(chunk + c).astype(BF16)
    fvt_ref[...] = _dot_nt(wvt_ref[0:cw, :], h).astype(BF16)
    dvt_ref[...] = _dot_nt(wvt_ref[cw:2 * cw, :], h).astype(BF16)
    fl_ref[...] = _dot(h, wf_ref[...])


def _in_proj(x2, g_mix, w_main, w_vt, w_f, cos_t, sin_t, tm):
    n_tok, d = x2.shape
    cw = 512
    tok = lambda i: (i, 0)
    tok_t = lambda i: (0, i)
    const = lambda i: (0, 0)
    o512 = jax.ShapeDtypeStruct((n_tok, cw), BF16)
    o1024 = jax.ShapeDtypeStruct((n_tok, 2 * cw), BF16)
    o512_t = jax.ShapeDtypeStruct((cw, n_tok), BF16)
    return pl.pallas_call(
        _in_proj_kernel,
        grid=(n_tok // tm,),
        in_specs=[pl.BlockSpec((tm, d), tok),
                  pl.BlockSpec((1, d), const),
                  pl.BlockSpec(w_main.shape, const, pipeline_mode=pl.Buffered(1)),
                  pl.BlockSpec(w_vt.shape, const, pipeline_mode=pl.Buffered(1)),
                  pl.BlockSpec(w_f.shape, const),
                  pl.BlockSpec((tm, LANES), tok),
                  pl.BlockSpec((tm, LANES), tok)],
        out_specs=[pl.BlockSpec((tm, cw), tok)] * 4
                  + [pl.BlockSpec((tm, 2 * cw), tok)] * 2
                  + [pl.BlockSpec((cw, tm), tok_t)] * 2
                  + [pl.BlockSpec((tm, LANES), tok)],
        out_shape=[o512] * 4 + [o1024] * 2 + [o512_t] * 2
                  + [jax.ShapeDtypeStruct((n_tok, LANES), F32)],
        compiler_params=_params("arbitrary"),
        name="in_proj",
    )(x2, g_mix, w_main, w_vt, w_f, cos_t, sin_t)


def _bias_lane_selectors():
    sel = np.zeros((3, LANES, FOX_HEADS // 2 * LANES), np.float32)
    for h in range(FOX_HEADS):
        base = (h // 2) * LANES + (HEAD_DIM if h % 2 == 0 else 0)
        for piece in range(3):
            sel[piece, h, base + piece] = 1.0
    return jnp.asarray(sel, BF16)


def _split3(t):
    pieces = []
    for _ in range(3):
        part = t.astype(BF16)
        t = t - part.astype(F32)
        pieces.append(part)
    return pieces


def _fox_prep_kernel(fl_ref, b_ref, tri_ref, sel_ref, out_ref, carry_ref):
    @pl.when(pl.program_id(1) == 0)
    def _():
        carry_ref[...] = jnp.zeros_like(carry_ref)

    z = fl_ref[...] + b_ref[...]
    log_f = jnp.minimum(z, 0.0) - jnp.log(1.0 + jnp.exp(-jnp.abs(z)))
    tc = z.shape[0]
    c = carry_ref[...]
    for part in _split3(log_f):
        c = c + _dot(tri_ref[...], part)
    carry_ref[...] = c[tc - 1:tc, :]
    acc = jnp.zeros(out_ref.shape, F32)
    for piece, part in enumerate(_split3(c * (-LOG2E))):
        acc = acc + _dot(part, sel_ref[piece])
    out_ref[...] = acc.astype(BF16)


def _fox_prep(fl, b_row, sel, n_batch, seq, tc):
    n_tok = fl.shape[0]
    nblk = seq // tc
    width = sel.shape[2]
    tri = jnp.tril(jnp.ones((tc, tc), BF16))
    return pl.pallas_call(
        _fox_prep_kernel,
        grid=(n_batch, nblk),
        in_specs=[pl.BlockSpec((tc, LANES), lambda b, j: (b * nblk + j, 0)),
                  pl.BlockSpec((1, LANES), lambda b, j: (0, 0)),
                  pl.BlockSpec((tc, tc), lambda b, j: (0, 0)),
                  pl.BlockSpec(sel.shape, lambda b, j: (0, 0, 0))],
        out_specs=pl.BlockSpec((tc, width), lambda b, j: (b * nblk + j, 0)),
        out_shape=jax.ShapeDtypeStruct((n_tok, width), BF16),
        scratch_shapes=[pltpu.VMEM((1, LANES), F32)],
        compiler_params=_params("arbitrary", "arbitrary"),
        name="fox_prep",
    )(fl, b_row, tri, sel)


def _attn_kernel(*refs, bq, bk, fox, lambda_init):
    if fox:
        q_ref, k_ref, vt_ref, bl_ref, out_ref, qs_ref, st_ref, p_ref, acc_ref, stat_ref, ka_ref, kb_ref = refs
    else:
        q_ref, k_ref, vt_ref, lam_ref, gsub_ref, out_ref, qs_ref, st_ref, p_ref, acc_ref, stat_ref = refs
    qi = pl.program_id(2)

    q = q_ref[...].astype(F32)
    lane = lax.broadcasted_iota(jnp.int32, q.shape, 1)
    if fox:
        @pl.when(qi == 0)
        def _():
            k = k_ref[...].astype(F32)
            bl = bl_ref[...].astype(F32)
            klane = lax.broadcasted_iota(jnp.int32, k.shape, 1)
            ka_ref[...] = jnp.where(klane < HEAD_DIM, k, bl).astype(BF16)
            kb_ref[...] = jnp.where(klane >= HEAD_DIM, k, bl).astype(BF16)

        q_streams = (jnp.where(lane < HEAD_DIM, q, jnp.where(lane < HEAD_DIM + 3, 1.0, 0.0)),
                     jnp.where(lane >= HEAD_DIM, q, jnp.where(lane < 3, 1.0, 0.0)))
        k_srcs = (ka_ref, kb_ref)
        v_rows = ((0, HEAD_DIM), (HEAD_DIM, 2 * HEAD_DIM))
    else:
        q_streams = (jnp.where(lane < HEAD_DIM, q, 0.0), jnp.where(lane >= HEAD_DIM, q, 0.0))
        k_srcs = (k_ref, k_ref)
        v_rows = ((0, 2 * HEAD_DIM), (0, 2 * HEAD_DIM))
    n_v = v_rows[0][1] - v_rows[0][0]
    ring = bq // bk
    assert ring * bk == bq and ring % 2 == 0 and ring >= 4
    for idx in range(2):
        qs_ref[idx] = q_streams[idx].astype(BF16)
        p_ref[idx, 1] = jnp.zeros((bk, bq), BF16)
        acc_ref[idx] = jnp.zeros((n_v + 16, bq), F32)
    ones_rows = jnp.ones((16, bk), BF16)
    first_diag = ring * qi

    def qk_stage(kj, idx, slot, q_lo=0):
        k0 = pl.multiple_of(kj * bk, bk)
        st_ref[idx, slot, :, q_lo:] = _dot_nt(k_srcs[idx][pl.ds(k0, bk), :], qs_ref[idx, q_lo:, :])

    M_ROW, A_ROW = 0, 8

    def pv_stage(kj, idx, pslot, q_lo=0):
        k0 = pl.multiple_of(kj * bk, bk)
        r0, r1 = v_rows[idx]
        v_aug = jnp.concatenate([vt_ref[r0:r1, pl.ds(k0, bk)], ones_rows], axis=0)
        alpha = stat_ref[idx, A_ROW + 8 * pslot:A_ROW + 8 * pslot + 1, q_lo:]
        acc_ref[idx, :, q_lo:] = (alpha * acc_ref[idx, :, q_lo:]
                                  + _dot(v_aug, p_ref[idx, pslot, :, q_lo:]))

    def softmax_stage(idx, slot, pslot, masked, q_lo=0):
        st = st_ref[idx, slot, :, q_lo:]
        if masked:
            key_idx = lax.broadcasted_iota(jnp.int32, st.shape, 0)
            qry_idx = lax.broadcasted_iota(jnp.int32, st.shape, 1)
            st = jnp.where(key_idx <= qry_idx, st, NEG_BIG)
        m = stat_ref[idx, M_ROW:M_ROW + 1, q_lo:]
        m_new = jnp.maximum(m, jnp.max(st, axis=0, keepdims=True))
        p_ref[idx, pslot, :, q_lo:] = jnp.exp2(st - m_new).astype(BF16)
        stat_ref[idx, M_ROW:M_ROW + 1, q_lo:] = m_new
        stat_ref[idx, A_ROW + 8 * pslot:A_ROW + 8 * pslot + 1, q_lo:] = jnp.exp2(m - m_new)

    for idx in range(2):
        stat_ref[idx, M_ROW:M_ROW + 8, :] = jnp.full((8, bq), NEG_BIG, F32)
        stat_ref[idx, A_ROW:A_ROW + 16, :] = jnp.ones((16, bq), F32)
        qk_stage(0, idx, 0)
        qk_stage(1, idx, 1)

    def block_group(j, diag_group):
        for s in range(ring):
            lo_prev = (s - 1) * bk if diag_group and s > 0 else 0
            lo_this = s * bk if diag_group else 0
            for idx in range(2):
                pv_stage(jnp.maximum(j + s - 1, 0), idx, (s - 1) % 2, lo_prev)
                softmax_stage(idx, s, s % 2, diag_group, lo_this)
                if not diag_group:
                    qk_stage(j + s + 2, idx, (s + 2) % ring)
                elif s + 2 < ring:
                    qk_stage(j + s + 2, idx, s + 2, (s + 2) * bk)

    def body(i, carry):
        block_group(ring * i, False)
        return carry

    lax.fori_loop(0, qi, body, 0)
    block_group(first_diag, True)
    outs = []
    for idx in range(2):
        pv_stage(first_diag + ring - 1, idx, (ring - 1) % 2, (ring - 1) * bk)
        acc = acc_ref[idx]
        outs.append(acc[0:n_v] * (1.0 / acc[n_v:n_v + 1]))
    o_a, o_b = outs
    if fox:
        y = jnp.concatenate([o_a, o_b], axis=0)
    else:
        lam_vec = lam_ref[...]
        lam = (jnp.exp(jnp.sum(lam_vec[0:1] * lam_vec[1:2], axis=1, keepdims=True))
               - jnp.exp(jnp.sum(lam_vec[2:3] * lam_vec[3:4], axis=1, keepdims=True))
               + lambda_init)
        y = o_a - lam * o_b
        y = y * lax.rsqrt(jnp.mean(y * y, axis=0, keepdims=True) + SUBLN_EPS)
        y = y * (gsub_ref[...] * (1.0 - lambda_init))
    out_ref[...] = y.T.astype(out_ref.dtype)


def _attention(q, k, vt, extra, n_batch, seq, fox, lambda_init=0.0):
    n_tok, width = q.shape
    n_pairs = width // LANES
    bq, bk = ATTN_Q_BLOCK, ATTN_K_BLOCK
    nq = seq // bq
    n_v = HEAD_DIM if fox else 2 * HEAD_DIM
    q_spec = pl.BlockSpec((bq, LANES), lambda b, hp, qi: (b * nq + qi, hp))
    k_spec = pl.BlockSpec((seq, LANES), lambda b, hp, qi: (b, hp))
    vt_spec = pl.BlockSpec((LANES, seq), lambda b, hp, qi: (hp, b))
    scratch = [pltpu.VMEM((2, bq, LANES), BF16),
               pltpu.VMEM((2, bq // bk, bk, bq), F32),
               pltpu.VMEM((2, 2, bk, bq), BF16),
               pltpu.VMEM((2, n_v + 16, bq), F32),
               pltpu.VMEM((2, 24, bq), F32)]
    if fox:
        extra_specs = [k_spec]
        scratch += [pltpu.VMEM((seq, LANES), BF16)] * 2
    else:
        lam_vec, g_sub = extra
        extra_specs = [pl.BlockSpec(lam_vec.shape, lambda b, hp, qi: (0, 0)),
                       pl.BlockSpec(g_sub.shape, lambda b, hp, qi: (0, 0))]
    return pl.pallas_call(
        functools.partial(_attn_kernel, bq=bq, bk=bk, fox=fox, lambda_init=lambda_init),
        grid=(n_batch, n_pairs, nq),
        in_specs=[q_spec, k_spec, vt_spec] + extra_specs,
        out_specs=q_spec,
        out_shape=jax.ShapeDtypeStruct((n_tok, width), BF16),
        scratch_shapes=scratch,
        compiler_params=_params("arbitrary", "arbitrary", "arbitrary"),
        name="fox_attn" if fox else "diff_attn",
    )(q, k, vt, *extra)


def _route_block(logits, tri_ref, cnt_ref):
    lane = lax.broadcasted_iota(jnp.int32, logits.shape, 1)
    lane_f = lane.astype(F32)
    last = float(LANES - 1)

    def first_argmax(t):
        top = jnp.max(t, axis=1, keepdims=True)
        return top, jnp.min(jnp.where(t == top, lane_f, last), axis=1, keepdims=True)

    is_group = lane < N_GROUPS
    g_max, g_sel = first_argmax(jnp.where(is_group, logits, NEG_BIG))
    g_w = 1.0 / jnp.sum(jnp.where(is_group, jnp.exp(logits - g_max), 0.0), axis=1, keepdims=True)
    lo = N_GROUPS + EXPERTS_PER_GROUP * g_sel
    scores = jnp.where((lane_f >= lo) & (lane_f < lo + EXPERTS_PER_GROUP), logits, NEG_BIG)
    v1, i1 = first_argmax(scores)
    v2, i2 = first_argmax(jnp.where(lane_f == i1, NEG_BIG, scores))
    t = jnp.exp(v2 - v1)
    w1 = g_w / (1.0 + t)
    w2 = w1 * t
    e1 = i1 - N_GROUPS
    e2 = i2 - N_GROUPS

    hit1 = lane_f == e1
    hit2 = lane_f == e2
    onehot = jnp.where(hit1 | hit2, 1.0, 0.0)
    before = _dot(tri_ref[...], onehot.astype(BF16)) + cnt_ref[...]
    r1 = jnp.sum(jnp.where(hit1, before, 0.0), axis=1, keepdims=True)
    r2 = jnp.sum(jnp.where(hit2, before, 0.0), axis=1, keepdims=True)
    n = logits.shape[0]
    cnt_ref[...] = before[n - 1:n, :] + onehot[n - 1:n, :]
    out = jnp.zeros(logits.shape, F32)
    for pos, val in enumerate((e1, e2, r1, r2, w1, w2)):
        out = jnp.where(lane == pos, val, out)
    return out


def _post_mixer_kernel(ya_ref, yb_ref, ga_ref, gb_ref, x_ref, wa_ref, wb_ref, wo_ref,
                       gc_ref, wcq_ref, kc_ref, vc_ref, wco_ref, gf_ref, wr_ref, br_ref, tri_ref,
                       x2_ref, hm_ref, route_ref, cnt_out_ref, cnt_ref):
    @pl.when((pl.program_id(0) == 0) & (pl.program_id(1) == 0))
    def _():
        cnt_ref[...] = jnp.zeros_like(cnt_ref)

    def sigmoid(t):
        return 1.0 / (1.0 + jnp.exp(-t))

    merged = (sigmoid(ga_ref[...].astype(F32)) * _dot(ya_ref[...], wa_ref[...])
              + sigmoid(gb_ref[...].astype(F32)) * _dot(yb_ref[...], wb_ref[...]))
    x1 = x_ref[...] + _dot(merged.astype(BF16), wo_ref[...])

    hx = _rms(x1, gc_ref[...], NORM_EPS).astype(BF16)
    qc = (_dot(hx, wcq_ref[...]) * QK_SCALE).astype(BF16)
    kc = kc_ref[...]
    vc = vc_ref[...]
    lane = lax.broadcasted_iota(jnp.int32, qc.shape, 1)
    zero = jnp.zeros_like(qc)
    o = jnp.zeros(qc.shape, F32)
    for h in range(CROSS_HEADS):
        in_head = (lane >= h * HEAD_DIM) & (lane < (h + 1) * HEAD_DIM)
        s = _dot_nt(jnp.where(in_head, qc, zero), kc)
        p = jnp.exp(s - jnp.max(s, axis=1, keepdims=True))
        p = p * (1.0 / jnp.sum(p, axis=1, keepdims=True))
        o = jnp.where(in_head, _dot(p.astype(BF16), vc), o)
    x2 = x1 + _dot(o.astype(BF16), wco_ref[...])
    x2_ref[...] = x2

    hm = _rms(x2, gf_ref[...], NORM_EPS)
    hm_ref[...] = hm.astype(BF16)
    hm_hi = hm.astype(BF16)
    hm_lo = (hm - hm_hi.astype(F32)).astype(BF16)
    logits = (_dot(hm_hi, wr_ref[0]) + _dot(hm_hi, wr_ref[1]) + _dot(hm_lo, wr_ref[0])
              + br_ref[...])
    route_ref[...] = _route_block(logits, tri_ref, cnt_ref)
    cnt_out_ref[...] = jnp.broadcast_to(cnt_ref[...], cnt_out_ref.shape)


def _post_mixer(ya, yb, ga, gb, x2d, wa, wb, wo, g_cross, wcq, kc, vc, wco, g_ffn, wr, br,
                n_batch, seq, mem_len, tm):
    n_tok, d = x2d.shape
    nblk = seq // tm
    tok = lambda b, j: (b * nblk + j, 0)
    const = lambda b, j: (0, 0)
    full = lambda a: pl.BlockSpec(a.shape, const)
    tri = jnp.tril(jnp.ones((tm, tm), BF16), -1)
    return pl.pallas_call(
        _post_mixer_kernel,
        grid=(n_batch, nblk),
        in_specs=[pl.BlockSpec((tm, ya.shape[1]), tok), pl.BlockSpec((tm, yb.shape[1]), tok),
                  pl.BlockSpec((tm, d), tok), pl.BlockSpec((tm, d), tok), pl.BlockSpec((tm, d), tok),
                  full(wa), full(wb), full(wo), full(g_cross), full(wcq),
                  pl.BlockSpec((mem_len, kc.shape[1]), lambda b, j: (b, 0)),
                  pl.BlockSpec((mem_len, vc.shape[1]), lambda b, j: (b, 0)),
                  full(wco), full(g_ffn), pl.BlockSpec(wr.shape, lambda b, j: (0, 0, 0)), full(br),
                  full(tri)],
        out_specs=[pl.BlockSpec((tm, d), tok), pl.BlockSpec((tm, d), tok),
                   pl.BlockSpec((tm, LANES), tok), pl.BlockSpec((8, LANES), const)],
        out_shape=[jax.ShapeDtypeStruct((n_tok, d), F32),
                   jax.ShapeDtypeStruct((n_tok, d), BF16),
                   jax.ShapeDtypeStruct((n_tok, LANES), F32),
                   jax.ShapeDtypeStruct((8, LANES), F32)],
        scratch_shapes=[pltpu.VMEM((1, LANES), F32)],
        compiler_params=_params("arbitrary", "arbitrary"),
        name="post_mixer",
    )(ya, yb, ga, gb, x2d, wa, wb, wo, g_cross, wcq, kc, vc, wco, g_ffn, wr, br, tri)


def _expert_kernel(be_ref, nused_ref, xs_ref, wg_ref, wu_ref, wd_ref, ys_ref, wg_s, wu_s, wd_s):
    i = pl.program_id(0)

    @pl.when((i == 0) | (be_ref[i] != be_ref[jnp.maximum(i - 1, 0)]))
    def _():
        wg_s[...] = wg_ref[0].astype(BF16)
        wu_s[...] = wu_ref[0].astype(BF16)
        wd_s[...] = wd_ref[0].astype(BF16)

    @pl.when(i < nused_ref[0])
    def _():
        xb = xs_ref[...]
        g = _dot(xb, wg_s[...])
        u = _dot(xb, wu_s[...])
        hb = (g * (1.0 / (1.0 + jnp.exp(-g))) * u).astype(BF16)
        ys_ref[...] = _dot(hb, wd_s[...]).astype(ys_ref.dtype)

    @pl.when(i >= nused_ref[0])
    def _():
        ys_ref[...] = jnp.zeros_like(ys_ref)


def _experts(block_expert, n_used, xs, wg, wu, wd):
    n_rows, d = xs.shape
    n_blocks = n_rows // MOE_BLOCK
    ff = wg.shape[2]
    grid_spec = pltpu.PrefetchScalarGridSpec(
        num_scalar_prefetch=2,
        grid=(n_blocks,),
        in_specs=[pl.BlockSpec((MOE_BLOCK, d), lambda i, be, nu: (i, 0)),
                  pl.BlockSpec((1, d, ff), lambda i, be, nu: (be[i], 0, 0)),
                  pl.BlockSpec((1, d, ff), lambda i, be, nu: (be[i], 0, 0)),
                  pl.BlockSpec((1, ff, d), lambda i, be, nu: (be[i], 0, 0))],
        out_specs=pl.BlockSpec((MOE_BLOCK, d), lambda i, be, nu: (i, 0)),
        scratch_shapes=[pltpu.VMEM((d, ff), BF16), pltpu.VMEM((d, ff), BF16),
                        pltpu.VMEM((ff, d), BF16)],
    )
    return pl.pallas_call(
        _expert_kernel,
        grid_spec=grid_spec,
        out_shape=jax.ShapeDtypeStruct((n_rows, d), BF16),
        compiler_params=_params("arbitrary"),
        name="experts",
    )(block_expert, n_used, xs, wg, wu, wd)


def _final_kernel(x_ref, y0_ref, y1_ref, w_ref, g_ref, out_ref, *, normalize):
    w = w_ref[...]
    out = (x_ref[...] + w[:, 0:1] * y0_ref[...].astype(F32) + w[:, 1:2] * y1_ref[...].astype(F32))
    out_ref[...] = _rms(out, g_ref[...], NORM_EPS) if normalize else out


def _final(x2, y0, y1, weights, g_final, tm, normalize):
    n_tok, d = x2.shape
    tok = lambda i: (i, 0)
    return pl.pallas_call(
        functools.partial(_final_kernel, normalize=normalize),
        grid=(n_tok // tm,),
        in_specs=[pl.BlockSpec((tm, d), tok), pl.BlockSpec((tm, d), tok), pl.BlockSpec((tm, d), tok),
                  pl.BlockSpec((tm, TOP_K), tok), pl.BlockSpec((1, d), lambda i: (0, 0))],
        out_specs=pl.BlockSpec((tm, d), tok),
        out_shape=jax.ShapeDtypeStruct((n_tok, d), F32),
        compiler_params=_params("arbitrary"),
        name="final_norm",
    )(x2, y0, y1, weights, g_final)


def _route(route, counts, n_tok):
    n_assign = n_tok * TOP_K
    flat_e = route[:, 0:TOP_K].astype(jnp.int32).reshape(n_assign)
    rank = route[:, TOP_K:2 * TOP_K].astype(jnp.int32).reshape(n_assign)
    weights = route[:, 2 * TOP_K:3 * TOP_K]
    counts = counts[0, :N_EXPERTS].astype(jnp.int32)
    experts = jnp.arange(N_EXPERTS, dtype=jnp.int32)

    def lookup(table, idx):
        return jnp.sum(jnp.where(idx[:, None] == experts[None, :], table[None, :], 0), axis=1)

    seg_start = jnp.cumsum(counts) - counts
    padded = (counts + MOE_BLOCK - 1) // MOE_BLOCK * MOE_BLOCK
    pad_end = jnp.cumsum(padded)
    pad_start = pad_end - padded
    dest = lookup(pad_start, flat_e) + rank
    order = jnp.argsort(lookup(seg_start, flat_e) + rank).astype(jnp.int32)

    n_blocks = -(-n_assign // MOE_BLOCK) + N_EXPERTS
    n_rows = n_blocks * MOE_BLOCK
    block_start = jnp.arange(n_blocks, dtype=jnp.int32) * MOE_BLOCK
    block_expert = jnp.minimum(jnp.sum(block_start[:, None] >= pad_end[None, :], axis=1),
                               N_EXPERTS - 1).astype(jnp.int32)
    offset = block_start - pad_start[block_expert]
    in_block = jnp.arange(MOE_BLOCK, dtype=jnp.int32)[None, :]
    valid = in_block < (counts[block_expert] - offset)[:, None]
    src = jnp.clip((seg_start[block_expert] + offset)[:, None] + in_block, 0, n_assign - 1)
    rows = block_start[:, None] + in_block
    row_tok = jnp.where(valid, order[src] // TOP_K, rows % n_tok).reshape(n_rows)
    n_used = (pad_end[-1] // MOE_BLOCK).astype(jnp.int32).reshape(1)
    return row_tok, dest, weights, block_expert, n_used


def kernel(x, mem, positions, g_mix, w_in, b_fgate, w_branch_a, w_branch_b, w_out, lambda_q1, lambda_k1, lambda_q2, lambda_k2, g_diff_sub, g_cross, g_mem, w_cq, w_ckv, w_co, g_ffn, w_group, b_group, w_expert, b_expert, w_exp_gate, w_exp_up, w_exp_down, g_final):
    n_batch, seq, d = x.shape
    mem_len = mem.shape[1]
    depth = g_mix.shape[0]
    n_tok = n_batch * seq
    fox_w = FOX_HEADS * HEAD_DIM
    diff_w = DIFF_HEADS * 2 * HEAD_DIM

    half = HEAD_DIM // 2
    inv_freq = 10000.0 ** (-jnp.arange(half, dtype=F32) * 2.0 / HEAD_DIM)
    ang = positions.astype(F32).reshape(n_tok, 1) * inv_freq
    cos_t = jnp.tile(jnp.cos(ang), (1, LANES // half))
    sin_t = jnp.tile(jnp.concatenate([-jnp.sin(ang), jnp.sin(ang)], axis=1), (1, LANES // HEAD_DIM))
    sel = _bias_lane_selectors()

    x2d = x.reshape(n_tok, d)
    mem2d = mem.reshape(n_batch * mem_len, d)
    for l in range(depth):
        lambda_init = 0.8 - 0.6 * math.exp(-0.3 * l)
        o_fv = 2 * fox_w
        o_fl = o_fv + fox_w
        o_dq = o_fl + FOX_HEADS
        o_dv = o_dq + 2 * diff_w
        o_ga = o_dv + diff_w
        wl = w_in[l]
        w_main = jnp.concatenate([wl[:, :o_fv], wl[:, o_dq:o_dv], wl[:, o_ga:]], axis=1).astype(BF16)
        w_vt = jnp.concatenate([wl[:, o_fv:o_fl], wl[:, o_dv:o_ga]], axis=1).T.astype(BF16)
        w_f = jnp.pad(wl[:, o_fl:o_dq], ((0, 0), (0, LANES - FOX_HEADS))).astype(BF16)
        b_row = jnp.pad(b_fgate[l], (0, LANES - FOX_HEADS))[None]

        kc, vc = _mem_kv(mem2d, g_mem[l][None], w_ckv[l].astype(BF16), n_batch, mem_len)
        fq, fk, dq, dk, ga, gb, fvt, dvt, fl = _in_proj(
            x2d, g_mix[l][None], w_main, w_vt, w_f, cos_t, sin_t, tm=512)
        bias_lanes = _fox_prep(fl, b_row, sel, n_batch, seq, tc=512)
        y_a = _attention(fq, fk, fvt, (bias_lanes,), n_batch, seq, fox=True)
        lam_vec = jnp.stack([lambda_q1[l], lambda_k1[l], lambda_q2[l], lambda_k2[l]])
        y_b = _attention(dq, dk, dvt, (lam_vec, g_diff_sub[l][:, None]), n_batch, seq,
                         fox=False, lambda_init=lambda_init)

        w_router = jnp.pad(jnp.concatenate([w_group[l], w_expert[l]], axis=1),
                           ((0, 0), (0, LANES - N_GROUPS - N_EXPERTS)))
        w_router_hi = w_router.astype(BF16)
        w_router = jnp.stack([w_router_hi, (w_router - w_router_hi.astype(F32)).astype(BF16)])
        b_router = jnp.pad(jnp.concatenate([b_group[l], b_expert[l]]),
                           (0, LANES - N_GROUPS - N_EXPERTS))[None]
        x2d, hm, route, counts = _post_mixer(
            y_a, y_b, ga, gb, x2d, w_branch_a[l].astype(BF16), w_branch_b[l].astype(BF16),
            w_out[l].astype(BF16), g_cross[l][None], w_cq[l].astype(BF16), kc, vc,
            w_co[l].astype(BF16), g_ffn[l][None], w_router, b_router,
            n_batch, seq, mem_len, tm=512)

        row_tok, dest, weights, block_expert, n_used = _route(route, counts, n_tok)
        xs = hm[row_tok]
        ys = _experts(block_expert, n_used, xs, w_exp_gate[l], w_exp_up[l], w_exp_down[l])
        dest = dest.reshape(n_tok, TOP_K)
        last = l + 1 == depth
        x2d = _final(x2d, ys[dest[:, 0]], ys[dest[:, 1]], weights, g_final[None], tm=512,
                     normalize=last)
    return x2d.reshape(n_batch, seq, d)
```

```python
import functools
import math

import numpy as np
import jax
import jax.numpy as jnp
from jax import lax
from jax.experimental import pallas as pl
from jax.experimental.pallas import tpu as pltpu

HEAD_DIM = 64
LANES = 128
FOX_HEADS = 8
DIFF_HEADS = 4
CROSS_HEADS = 4
N_GROUPS = 4
EXPERTS_PER_GROUP = 8
N_EXPERTS = N_GROUPS * EXPERTS_PER_GROUP
TOP_K = 2
MOE_BLOCK = 512
ATTN_Q_BLOCK = 1024
ATTN_K_BLOCK = 256
NORM_EPS = 1e-6
SUBLN_EPS = 1e-5
QK_SCALE = HEAD_DIM ** -0.5
LOG2E = math.log2(math.e)
Q_PRESCALE = QK_SCALE * LOG2E
NEG_BIG = -1e30
VMEM_LIMIT = 56 * 2**20

BF16 = jnp.bfloat16
F32 = jnp.float32


def _rms(t, g, eps):
    return t * lax.rsqrt(jnp.mean(t * t, axis=-1, keepdims=True) + eps) * g


def _dot(a, b):
    return jnp.dot(a, b, preferred_element_type=F32)


def _dot_nt(a, b):
    return lax.dot_general(a, b, (((1,), (1,)), ((), ())), preferred_element_type=F32)


def _params(*sem):
    return pltpu.CompilerParams(dimension_semantics=sem, vmem_limit_bytes=VMEM_LIMIT)


def _mem_kv_kernel(mem_ref, g_ref, w_ref, k_ref, v_ref):
    h = _rms(mem_ref[...], g_ref[...], NORM_EPS).astype(BF16)
    kv = _dot(h, w_ref[...])
    width = k_ref.shape[1]
    k_ref[...] = kv[:, :width].astype(BF16)
    v_ref[...] = kv[:, width:].astype(BF16)


def _mem_kv(mem2, g_mem, w_ckv, n_batch, mem_len):
    d = mem2.shape[1]
    cw = w_ckv.shape[1] // 2
    out = jax.ShapeDtypeStruct((n_batch * mem_len, cw), BF16)
    return pl.pallas_call(
        _mem_kv_kernel,
        grid=(n_batch,),
        in_specs=[pl.BlockSpec((mem_len, d), lambda b: (b, 0)),
                  pl.BlockSpec((1, d), lambda b: (0, 0)),
                  pl.BlockSpec((d, 2 * cw), lambda b: (0, 0))],
        out_specs=[pl.BlockSpec((mem_len, cw), lambda b: (b, 0))] * 2,
        out_shape=[out, out],
        compiler_params=_params("arbitrary"),
        name="mem_kv",
    )(mem2, g_mem, w_ckv)


def _rope(t, cos, sin_signed, first_half):
    fwd = pltpu.roll(t, LANES - HEAD_DIM // 2, axis=1)
    bwd = pltpu.roll(t, HEAD_DIM // 2, axis=1)
    return t * cos + jnp.where(first_half, fwd, bwd) * sin_signed


def _in_proj_kernel(x_ref, g_ref, w_ref, wvt_ref, wf_ref, cos_ref, sin_ref,
                    fq_ref, fk_ref, dq_ref, dk_ref, ga_ref, gb_ref, fvt_ref, dvt_ref, fl_ref):
    h = _rms(x_ref[...], g_ref[...], NORM_EPS).astype(BF16)
    tm = h.shape[0]
    cw = fq_ref.shape[1]

    def proj(chunk):
        return _dot(h, w_ref[:, chunk * cw:(chunk + 1) * cw])

    fq_ref[...] = (proj(0) * Q_PRESCALE).astype(BF16)
    fk_ref[...] = proj(1).astype(BF16)

    cos = cos_ref[...]
    sin = sin_ref[...]
    lane = lax.broadcasted_iota(jnp.int32, (tm, LANES), 1)
    first_half = (lane % HEAD_DIM) < HEAD_DIM // 2
    for out_ref, chunk, scale in ((dq_ref, 2, Q_PRESCALE), (dk_ref, 3, 1.0)):
        t = proj(chunk)
        for c in range(cw // LANES):
            blk = _rope(t[:, c * LANES:(c + 1) * LANES], cos, sin, first_half)
            out_ref[:, c * LANES:(c + 1) * LANES] = (blk * scale).astype(BF16)

    for out_ref, chunk in ((ga_ref, 4), (gb_ref, 6)):
        for c in range(2):
            out_ref[:, c * cw:(c + 1) * cw] = proj(chunk + c).astype(BF16)
    fvt_ref[...] = _dot_nt(wvt_ref[0:cw, :], h).astype(BF16)
    dvt_ref[...] = _dot_nt(wvt_ref[cw:2 * cw, :], h).astype(BF16)
    fl_ref[...] = _dot(h, wf_ref[...])


def _in_proj(x2, g_mix, w_main, w_vt, w_f, cos_t, sin_t, tm):
    n_tok, d = x2.shape
    cw = 512
    tok = lambda i: (i, 0)
    tok_t = lambda i: (0, i)
    const = lambda i: (0, 0)
    o512 = jax.ShapeDtypeStruct((n_tok, cw), BF16)
    o1024 = jax.ShapeDtypeStruct((n_tok, 2 * cw), BF16)
    o512_t = jax.ShapeDtypeStruct((cw, n_tok), BF16)
    return pl.pallas_call(
        _in_proj_kernel,
        grid=(n_tok // tm,),
        in_specs=[pl.BlockSpec((tm, d), tok),
                  pl.BlockSpec((1, d), const),
                  pl.BlockSpec(w_main.shape, const, pipeline_mode=pl.Buffered(1)),
                  pl.BlockSpec(w_vt.shape, const, pipeline_mode=pl.Buffered(1)),
                  pl.BlockSpec(w_f.shape, const),
                  pl.BlockSpec((tm, LANES), tok),
                  pl.BlockSpec((tm, LANES), tok)],
        out_specs=[pl.BlockSpec((tm, cw), tok)] * 4
                  + [pl.BlockSpec((tm, 2 * cw), tok)] * 2
                  + [pl.BlockSpec((cw, tm), tok_t)] * 2
                  + [pl.BlockSpec((tm, LANES), tok)],
        out_shape=[o512] * 4 + [o1024] * 2 + [o512_t] * 2
                  + [jax.ShapeDtypeStruct((n_tok, LANES), F32)],
        compiler_params=_params("arbitrary"),
        name="in_proj",
    )(x2, g_mix, w_main, w_vt, w_f, cos_t, sin_t)


def _bias_lane_selectors():
    sel = np.zeros((3, LANES, FOX_HEADS // 2 * LANES), np.float32)
    for h in range(FOX_HEADS):
        base = (h // 2) * LANES + (HEAD_DIM if h % 2 == 0 else 0)
        for piece in range(3):
            sel[piece, h, base + piece] = 1.0
    return jnp.asarray(sel, BF16)


def _split3(t):
    pieces = []
    for _ in range(3):
        part = t.astype(BF16)
        t = t - part.astype(F32)
        pieces.append(part)
    return pieces


def _fox_prep_kernel(fl_ref, b_ref, tri_ref, sel_ref, out_ref, carry_ref):
    @pl.when(pl.program_id(1) == 0)
    def _():
        carry_ref[...] = jnp.zeros_like(carry_ref)

    z = fl_ref[...] + b_ref[...]
    log_f = jnp.minimum(z, 0.0) - jnp.log(1.0 + jnp.exp(-jnp.abs(z)))
    tc = z.shape[0]
    c = carry_ref[...]
    for part in _split3(log_f):
        c = c + _dot(tri_ref[...], part)
    carry_ref[...] = c[tc - 1:tc, :]
    acc = jnp.zeros(out_ref.shape, F32)
    for piece, part in enumerate(_split3(c * (-LOG2E))):
        acc = acc + _dot(part, sel_ref[piece])
    out_ref[...] = acc.astype(BF16)


def _fox_prep(fl, b_row, sel, n_batch, seq, tc):
    n_tok = fl.shape[0]
    nblk = seq // tc
    width = sel.shape[2]
    tri = jnp.tril(jnp.ones((tc, tc), BF16))
    return pl.pallas_call(
        _fox_prep_kernel,
        grid=(n_batch, nblk),
        in_specs=[pl.BlockSpec((tc, LANES), lambda b, j: (b * nblk + j, 0)),
                  pl.BlockSpec((1, LANES), lambda b, j: (0, 0)),
                  pl.BlockSpec((tc, tc), lambda b, j: (0, 0)),
                  pl.BlockSpec(sel.shape, lambda b, j: (0, 0, 0))],
        out_specs=pl.BlockSpec((tc, width), lambda b, j: (b * nblk + j, 0)),
        out_shape=jax.ShapeDtypeStruct((n_tok, width), BF16),
        scratch_shapes=[pltpu.VMEM((1, LANES), F32)],
        compiler_params=_params("arbitrary", "arbitrary"),
        name="fox_prep",
    )(fl, b_row, tri, sel)


def _attn_kernel(*refs, bq, bk, fox, lambda_init):
    if fox:
        q_ref, k_ref, vt_ref, bl_ref, out_ref, qs_ref, st_ref, p_ref, acc_ref, stat_ref, ka_ref, kb_ref = refs
    else:
        q_ref, k_ref, vt_ref, lam_ref, gsub_ref, out_ref, qs_ref, st_ref, p_ref, acc_ref, stat_ref = refs
    qi = pl.program_id(2)

    q = q_ref[...].astype(F32)
    lane = lax.broadcasted_iota(jnp.int32, q.shape, 1)
    if fox:
        @pl.when(qi == 0)
        def _():
            k = k_ref[...].astype(F32)
            bl = bl_ref[...].astype(F32)
            klane = lax.broadcasted_iota(jnp.int32, k.shape, 1)
            ka_ref[...] = jnp.where(klane < HEAD_DIM, k, bl).astype(BF16)
            kb_ref[...] = jnp.where(klane >= HEAD_DIM, k, bl).astype(BF16)

        q_streams = (jnp.where(lane < HEAD_DIM, q, jnp.where(lane < HEAD_DIM + 3, 1.0, 0.0)),
                     jnp.where(lane >= HEAD_DIM, q, jnp.where(lane < 3, 1.0, 0.0)))
        k_srcs = (ka_ref, kb_ref)
        v_rows = ((0, HEAD_DIM), (HEAD_DIM, 2 * HEAD_DIM))
    else:
        q_streams = (jnp.where(lane < HEAD_DIM, q, 0.0), jnp.where(lane >= HEAD_DIM, q, 0.0))
        k_srcs = (k_ref, k_ref)
        v_rows = ((0, 2 * HEAD_DIM), (0, 2 * HEAD_DIM))
    n_v = v_rows[0][1] - v_rows[0][0]
    ring = bq // bk
    assert ring * bk == bq and ring % 2 == 0 and ring >= 4
    for idx in range(2):
        qs_ref[idx] = q_streams[idx].astype(BF16)
        p_ref[idx, 1] = jnp.zeros((bk, bq), BF16)
        acc_ref[idx] = jnp.zeros((n_v + 16, bq), F32)
    ones_rows = jnp.ones((16, bk), BF16)
    first_diag = ring * qi

    def qk_stage(kj, idx, slot, q_lo=0):
        k0 = pl.multiple_of(kj * bk, bk)
        st_ref[idx, slot, :, q_lo:] = _dot_nt(k_srcs[idx][pl.ds(k0, bk), :], qs_ref[idx, q_lo:, :])

    M_ROW, A_ROW = 0, 8

    def pv_stage(kj, idx, pslot, q_lo=0):
        k0 = pl.multiple_of(kj * bk, bk)
        r0, r1 = v_rows[idx]
        v_aug = jnp.concatenate([vt_ref[r0:r1, pl.ds(k0, bk)], ones_rows], axis=0)
        alpha = stat_ref[idx, A_ROW + 8 * pslot:A_ROW + 8 * pslot + 1, q_lo:]
        acc_ref[idx, :, q_lo:] = (alpha * acc_ref[idx, :, q_lo:]
                                  + _dot(v_aug, p_ref[idx, pslot, :, q_lo:]))

    def softmax_stage(idx, slot, pslot, masked, q_lo=0):
        st = st_ref[idx, slot, :, q_lo:]
        if masked:
            key_idx = lax.broadcasted_iota(jnp.int32, st.shape, 0)
            qry_idx = lax.broadcasted_iota(jnp.int32, st.shape, 1)
            st = jnp.where(key_idx <= qry_idx, st, NEG_BIG)
        m = stat_ref[idx, M_ROW:M_ROW + 1, q_lo:]
        m_new = jnp.maximum(m, jnp.max(st, axis=0, keepdims=True))
        p_ref[idx, pslot, :, q_lo:] = jnp.exp2(st - m_new).astype(BF16)
        stat_ref[idx, M_ROW:M_ROW + 1, q_lo:] = m_new
        stat_ref[idx, A_ROW + 8 * pslot:A_ROW + 8 * pslot + 1, q_lo:] = jnp.exp2(m - m_new)

    for idx in range(2):
        stat_ref[idx, M_ROW:M_ROW + 8, :] = jnp.full((8, bq), NEG_BIG, F32)
        stat_ref[idx, A_ROW:A_ROW + 16, :] = jnp.ones((16, bq), F32)
        qk_stage(0, idx, 0)
        qk_stage(1, idx, 1)

    def block_group(j, diag_group):
        for s in range(ring):
            lo_prev = (s - 1) * bk if diag_group and s > 0 else 0
            lo_this = s * bk if diag_group else 0
            for idx in range(2):
                if not diag_group:
                    qk_stage(j + s + 2, idx, (s + 2) % ring)
                elif s + 2 < ring:
                    qk_stage(j + s + 2, idx, s + 2, (s + 2) * bk)
                pv_stage(jnp.maximum(j + s - 1, 0), idx, (s - 1) % 2, lo_prev)
                softmax_stage(idx, s, s % 2, diag_group, lo_this)

    def body(i, carry):
        block_group(ring * i, False)
        return carry

    lax.fori_loop(0, qi, body, 0)
    block_group(first_diag, True)
    outs = []
    for idx in range(2):
        pv_stage(first_diag + ring - 1, idx, (ring - 1) % 2, (ring - 1) * bk)
        acc = acc_ref[idx]
        outs.append(acc[0:n_v] * (1.0 / acc[n_v:n_v + 1]))
    o_a, o_b = outs
    if fox:
        y = jnp.concatenate([o_a, o_b], axis=0)
    else:
        lam_vec = lam_ref[...]
        lam = (jnp.exp(jnp.sum(lam_vec[0:1] * lam_vec[1:2], axis=1, keepdims=True))
               - jnp.exp(jnp.sum(lam_vec[2:3] * lam_vec[3:4], axis=1, keepdims=True))
               + lambda_init)
        y = o_a - lam * o_b
        y = y * lax.rsqrt(jnp.mean(y * y, axis=0, keepdims=True) + SUBLN_EPS)
        y = y * (gsub_ref[...] * (1.0 - lambda_init))
    out_ref[...] = y.T.astype(out_ref.dtype)


def _attention(q, k, vt, extra, n_batch, seq, fox, lambda_init=0.0):
    n_tok, width = q.shape
    n_pairs = width // LANES
    bq, bk = ATTN_Q_BLOCK, ATTN_K_BLOCK
    nq = seq // bq
    n_v = HEAD_DIM if fox else 2 * HEAD_DIM
    q_spec = pl.BlockSpec((bq, LANES), lambda b, hp, qi: (b * nq + qi, hp))
    k_spec = pl.BlockSpec((seq, LANES), lambda b, hp, qi: (b, hp))
    vt_spec = pl.BlockSpec((LANES, seq), lambda b, hp, qi: (hp, b))
    scratch = [pltpu.VMEM((2, bq, LANES), BF16),
               pltpu.VMEM((2, bq // bk, bk, bq), F32),
               pltpu.VMEM((2, 2, bk, bq), BF16),
               pltpu.VMEM((2, n_v + 16, bq), F32),
               pltpu.VMEM((2, 24, bq), F32)]
    if fox:
        extra_specs = [k_spec]
        scratch += [pltpu.VMEM((seq, LANES), BF16)] * 2
    else:
        lam_vec, g_sub = extra
        extra_specs = [pl.BlockSpec(lam_vec.shape, lambda b, hp, qi: (0, 0)),
                       pl.BlockSpec(g_sub.shape, lambda b, hp, qi: (0, 0))]
    return pl.pallas_call(
        functools.partial(_attn_kernel, bq=bq, bk=bk, fox=fox, lambda_init=lambda_init),
        grid=(n_batch, n_pairs, nq),
        in_specs=[q_spec, k_spec, vt_spec] + extra_specs,
        out_specs=q_spec,
        out_shape=jax.ShapeDtypeStruct((n_tok, width), BF16),
        scratch_shapes=scratch,
        compiler_params=_params("arbitrary", "arbitrary", "arbitrary"),
        name="fox_attn" if fox else "diff_attn",
    )(q, k, vt, *extra)


def _route_block(logits, tri_ref, cnt_ref):
    lane = lax.broadcasted_iota(jnp.int32, logits.shape, 1)
    lane_f = lane.astype(F32)
    last = float(LANES - 1)

    def first_argmax(t):
        top = jnp.max(t, axis=1, keepdims=True)
        return top, jnp.min(jnp.where(t == top, lane_f, last), axis=1, keepdims=True)

    is_group = lane < N_GROUPS
    g_max, g_sel = first_argmax(jnp.where(is_group, logits, NEG_BIG))
    g_w = 1.0 / jnp.sum(jnp.where(is_group, jnp.exp(logits - g_max), 0.0), axis=1, keepdims=True)
    lo = N_GROUPS + EXPERTS_PER_GROUP * g_sel
    scores = jnp.where((lane_f >= lo) & (lane_f < lo + EXPERTS_PER_GROUP), logits, NEG_BIG)
    v1, i1 = first_argmax(scores)
    v2, i2 = first_argmax(jnp.where(lane_f == i1, NEG_BIG, scores))
    t = jnp.exp(v2 - v1)
    w1 = g_w / (1.0 + t)
    w2 = w1 * t
    e1 = i1 - N_GROUPS
    e2 = i2 - N_GROUPS

    hit1 = lane_f == e1
    hit2 = lane_f == e2
    onehot = jnp.where(hit1 | hit2, 1.0, 0.0)
    before = _dot(tri_ref[...], onehot.astype(BF16)) + cnt_ref[...]
    r1 = jnp.sum(jnp.where(hit1, before, 0.0), axis=1, keepdims=True)
    r2 = jnp.sum(jnp.where(hit2, before, 0.0), axis=1, keepdims=True)
    n = logits.shape[0]
    cnt_ref[...] = before[n - 1:n, :] + onehot[n - 1:n, :]
    out = jnp.zeros(logits.shape, F32)
    for pos, val in enumerate((e1, e2, r1, r2, w1, w2)):
        out = jnp.where(lane == pos, val, out)
    return out


def _post_mixer_kernel(ya_ref, yb_ref, ga_ref, gb_ref, x_ref, wa_ref, wb_ref, wo_ref,
                       gc_ref, wcq_ref, kc_ref, vc_ref, wco_ref, gf_ref, wr_ref, br_ref, tri_ref,
                       x2_ref, hm_ref, route_ref, route_t_ref, cnt_out_ref, cnt_ref):
    @pl.when((pl.program_id(0) == 0) & (pl.program_id(1) == 0))
    def _():
        cnt_ref[...] = jnp.zeros_like(cnt_ref)

    def sigmoid(t):
        return 1.0 / (1.0 + jnp.exp(-t))

    merged = (sigmoid(ga_ref[...].astype(F32)) * _dot(ya_ref[...], wa_ref[...])
              + sigmoid(gb_ref[...].astype(F32)) * _dot(yb_ref[...], wb_ref[...]))
    x1 = x_ref[...] + _dot(merged.astype(BF16), wo_ref[...])

    hx = _rms(x1, gc_ref[...], NORM_EPS).astype(BF16)
    qc = (_dot(hx, wcq_ref[...]) * QK_SCALE).astype(BF16)
    kc = kc_ref[...]
    vc = vc_ref[...]
    lane = lax.broadcasted_iota(jnp.int32, qc.shape, 1)
    zero = jnp.zeros_like(qc)
    o = jnp.zeros(qc.shape, F32)
    for h in range(CROSS_HEADS):
        in_head = (lane >= h * HEAD_DIM) & (lane < (h + 1) * HEAD_DIM)
        s = _dot_nt(jnp.where(in_head, qc, zero), kc)
        p = jnp.exp(s - jnp.max(s, axis=1, keepdims=True))
        p = p * (1.0 / jnp.sum(p, axis=1, keepdims=True))
        o = jnp.where(in_head, _dot(p.astype(BF16), vc), o)
    x2 = x1 + _dot(o.astype(BF16), wco_ref[...])
    x2_ref[...] = x2

    hm = _rms(x2, gf_ref[...], NORM_EPS)
    hm_ref[...] = hm.astype(BF16)
    hm_hi = hm.astype(BF16)
    hm_lo = (hm - hm_hi.astype(F32)).astype(BF16)
    logits = (_dot(hm_hi, wr_ref[0]) + _dot(hm_hi, wr_ref[1]) + _dot(hm_lo, wr_ref[0])
              + br_ref[...])
    route = _route_block(logits, tri_ref, cnt_ref)
    route_ref[...] = route
    route_t_ref[...] = route.T[0:8, :]
    cnt_out_ref[...] = jnp.broadcast_to(cnt_ref[...], cnt_out_ref.shape)


def _post_mixer(ya, yb, ga, gb, x2d, wa, wb, wo, g_cross, wcq, kc, vc, wco, g_ffn, wr, br,
                n_batch, seq, mem_len, tm):
    n_tok, d = x2d.shape
    nblk = seq // tm
    tok = lambda b, j: (b * nblk + j, 0)
    const = lambda b, j: (0, 0)
    full = lambda a: pl.BlockSpec(a.shape, const)
    tri = jnp.tril(jnp.ones((tm, tm), BF16), -1)
    return pl.pallas_call(
        _post_mixer_kernel,
        grid=(n_batch, nblk),
        in_specs=[pl.BlockSpec((tm, ya.shape[1]), tok), pl.BlockSpec((tm, yb.shape[1]), tok),
                  pl.BlockSpec((tm, d), tok), pl.BlockSpec((tm, d), tok), pl.BlockSpec((tm, d), tok),
                  full(wa), full(wb), full(wo), full(g_cross), full(wcq),
                  pl.BlockSpec((mem_len, kc.shape[1]), lambda b, j: (b, 0)),
                  pl.BlockSpec((mem_len, vc.shape[1]), lambda b, j: (b, 0)),
                  full(wco), full(g_ffn), pl.BlockSpec(wr.shape, lambda b, j: (0, 0, 0)), full(br),
                  full(tri)],
        out_specs=[pl.BlockSpec((tm, d), tok), pl.BlockSpec((tm, d), tok),
                   pl.BlockSpec((tm, LANES), tok),
                   pl.BlockSpec((8, tm), lambda b, j: (0, b * nblk + j)),
                   pl.BlockSpec((8, LANES), const)],
        out_shape=[jax.ShapeDtypeStruct((n_tok, d), F32),
                   jax.ShapeDtypeStruct((n_tok, d), BF16),
                   jax.ShapeDtypeStruct((n_tok, LANES), F32),
                   jax.ShapeDtypeStruct((8, n_tok), F32),
                   jax.ShapeDtypeStruct((8, LANES), F32)],
        scratch_shapes=[pltpu.VMEM((1, LANES), F32)],
        compiler_params=_params("arbitrary", "arbitrary"),
        name="post_mixer",
    )(ya, yb, ga, gb, x2d, wa, wb, wo, g_cross, wcq, kc, vc, wco, g_ffn, wr, br, tri)


def _expert_kernel(be_ref, nused_ref, xs_ref, wg_ref, wu_ref, wd_ref, ys_ref, wg_s, wu_s, wd_s):
    i = pl.program_id(0)

    @pl.when((i == 0) | (be_ref[i] != be_ref[jnp.maximum(i - 1, 0)]))
    def _():
        wg_s[...] = wg_ref[0].astype(BF16)
        wu_s[...] = wu_ref[0].astype(BF16)
        wd_s[...] = wd_ref[0].astype(BF16)

    @pl.when(i < nused_ref[0])
    def _():
        xb = xs_ref[...]
        g = _dot(xb, wg_s[...])
        u = _dot(xb, wu_s[...])
        hb = (g * (1.0 / (1.0 + jnp.exp(-g))) * u).astype(BF16)
        ys_ref[...] = _dot(hb, wd_s[...]).astype(ys_ref.dtype)

    @pl.when(i >= nused_ref[0])
    def _():
        ys_ref[...] = jnp.zeros_like(ys_ref)


def _experts(block_expert, n_used, xs, wg, wu, wd):
    n_rows, d = xs.shape
    n_blocks = n_rows // MOE_BLOCK
    ff = wg.shape[2]
    grid_spec = pltpu.PrefetchScalarGridSpec(
        num_scalar_prefetch=2,
        grid=(n_blocks,),
        in_specs=[pl.BlockSpec((MOE_BLOCK, d), lambda i, be, nu: (i, 0)),
                  pl.BlockSpec((1, d, ff), lambda i, be, nu: (be[i], 0, 0)),
                  pl.BlockSpec((1, d, ff), lambda i, be, nu: (be[i], 0, 0)),
                  pl.BlockSpec((1, ff, d), lambda i, be, nu: (be[i], 0, 0))],
        out_specs=pl.BlockSpec((MOE_BLOCK, d), lambda i, be, nu: (i, 0)),
        scratch_shapes=[pltpu.VMEM((d, ff), BF16), pltpu.VMEM((d, ff), BF16),
                        pltpu.VMEM((ff, d), BF16)],
    )
    return pl.pallas_call(
        _expert_kernel,
        grid_spec=grid_spec,
        out_shape=jax.ShapeDtypeStruct((n_rows, d), BF16),
        compiler_params=_params("arbitrary"),
        name="experts",
    )(block_expert, n_used, xs, wg, wu, wd)


def _final_kernel(x_ref, y0_ref, y1_ref, route_ref, g_ref, out_ref, *, normalize):
    route = route_ref[...]
    w0 = route[:, 2 * TOP_K:2 * TOP_K + 1]
    w1 = route[:, 2 * TOP_K + 1:2 * TOP_K + 2]
    out = x_ref[...] + w0 * y0_ref[...].astype(F32) + w1 * y1_ref[...].astype(F32)
    out_ref[...] = _rms(out, g_ref[...], NORM_EPS) if normalize else out


def _final(x2, y0, y1, route, g_final, tm, normalize):
    n_tok, d = x2.shape
    tok = lambda i: (i, 0)
    return pl.pallas_call(
        functools.partial(_final_kernel, normalize=normalize),
        grid=(n_tok // tm,),
        in_specs=[pl.BlockSpec((tm, d), tok), pl.BlockSpec((tm, d), tok), pl.BlockSpec((tm, d), tok),
                  pl.BlockSpec((tm, LANES), tok), pl.BlockSpec((1, d), lambda i: (0, 0))],
        out_specs=pl.BlockSpec((tm, d), tok),
        out_shape=jax.ShapeDtypeStruct((n_tok, d), F32),
        compiler_params=_params("arbitrary"),
        name="final_norm",
    )(x2, y0, y1, route, g_final)


def _route(route_t, counts, n_tok):
    n_assign = n_tok * TOP_K
    flat_e = route_t[0:TOP_K].astype(jnp.int32).reshape(n_assign)
    rank = route_t[TOP_K:2 * TOP_K].astype(jnp.int32).reshape(n_assign)
    counts = counts[0, :N_EXPERTS].astype(jnp.int32)
    experts = jnp.arange(N_EXPERTS, dtype=jnp.int32)

    def lookup(table, idx):
        return jnp.sum(jnp.where(idx[:, None] == experts[None, :], table[None, :], 0), axis=1)

    seg_start = jnp.cumsum(counts) - counts
    padded = (counts + MOE_BLOCK - 1) // MOE_BLOCK * MOE_BLOCK
    pad_end = jnp.cumsum(padded)
    pad_start = pad_end - padded
    dest = lookup(pad_start, flat_e) + rank
    order = jnp.argsort(lookup(seg_start, flat_e) + rank).astype(jnp.int32)

    n_blocks = -(-n_assign // MOE_BLOCK) + N_EXPERTS
    n_rows = n_blocks * MOE_BLOCK
    block_start = jnp.arange(n_blocks, dtype=jnp.int32) * MOE_BLOCK
    block_expert = jnp.minimum(jnp.sum(block_start[:, None] >= pad_end[None, :], axis=1),
                               N_EXPERTS - 1).astype(jnp.int32)
    offset = block_start - pad_start[block_expert]
    in_block = jnp.arange(MOE_BLOCK, dtype=jnp.int32)[None, :]
    valid = in_block < (counts[block_expert] - offset)[:, None]
    src = jnp.clip((seg_start[block_expert] + offset)[:, None] + in_block, 0, n_assign - 1)
    rows = block_start[:, None] + in_block
    row_tok = jnp.where(valid, order[src] % n_tok, rows % n_tok).reshape(n_rows)
    n_used = (pad_end[-1] // MOE_BLOCK).astype(jnp.int32).reshape(1)
    return row_tok, dest.reshape(TOP_K, n_tok), block_expert, n_used


def kernel(x, mem, positions, g_mix, w_in, b_fgate, w_branch_a, w_branch_b, w_out, lambda_q1, lambda_k1, lambda_q2, lambda_k2, g_diff_sub, g_cross, g_mem, w_cq, w_ckv, w_co, g_ffn, w_group, b_group, w_expert, b_expert, w_exp_gate, w_exp_up, w_exp_down, g_final):
    n_batch, seq, d = x.shape
    mem_len = mem.shape[1]
    depth = g_mix.shape[0]
    n_tok = n_batch * seq
    fox_w = FOX_HEADS * HEAD_DIM
    diff_w = DIFF_HEADS * 2 * HEAD_DIM

    half = HEAD_DIM // 2
    inv_freq = 10000.0 ** (-jnp.arange(half, dtype=F32) * 2.0 / HEAD_DIM)
    ang = positions.astype(F32).reshape(n_tok, 1) * jnp.tile(inv_freq, LANES // half)[None, :]
    sign = jnp.tile(jnp.concatenate([-jnp.ones((half,), F32), jnp.ones((half,), F32)]),
                    LANES // HEAD_DIM)
    cos_t = jnp.cos(ang)
    sin_t = jnp.sin(ang) * sign[None, :]
    sel = _bias_lane_selectors()

    x2d = x.reshape(n_tok, d)
    mem2d = mem.reshape(n_batch * mem_len, d)
    for l in range(depth):
        lambda_init = 0.8 - 0.6 * math.exp(-0.3 * l)
        o_fv = 2 * fox_w
        o_fl = o_fv + fox_w
        o_dq = o_fl + FOX_HEADS
        o_dv = o_dq + 2 * diff_w
        o_ga = o_dv + diff_w
        wl = w_in[l]
        w_main = jnp.concatenate([wl[:, :o_fv], wl[:, o_dq:o_dv], wl[:, o_ga:]], axis=1).astype(BF16)
        w_vt = jnp.concatenate([wl[:, o_fv:o_fl], wl[:, o_dv:o_ga]], axis=1).T.astype(BF16)
        w_f = jnp.pad(wl[:, o_fl:o_dq], ((0, 0), (0, LANES - FOX_HEADS))).astype(BF16)
        b_row = jnp.pad(b_fgate[l], (0, LANES - FOX_HEADS))[None]

        kc, vc = _mem_kv(mem2d, g_mem[l][None], w_ckv[l].astype(BF16), n_batch, mem_len)
        fq, fk, dq, dk, ga, gb, fvt, dvt, fl = _in_proj(
            x2d, g_mix[l][None], w_main, w_vt, w_f, cos_t, sin_t, tm=512)
        bias_lanes = _fox_prep(fl, b_row, sel, n_batch, seq, tc=512)
        y_a = _attention(fq, fk, fvt, (bias_lanes,), n_batch, seq, fox=True)
        lam_vec = jnp.stack([lambda_q1[l], lambda_k1[l], lambda_q2[l], lambda_k2[l]])
        y_b = _attention(dq, dk, dvt, (lam_vec, g_diff_sub[l][:, None]), n_batch, seq,
                         fox=False, lambda_init=lambda_init)

        w_router = jnp.pad(jnp.concatenate([w_group[l], w_expert[l]], axis=1),
                           ((0, 0), (0, LANES - N_GROUPS - N_EXPERTS)))
        w_router_hi = w_router.astype(BF16)
        w_router = jnp.stack([w_router_hi, (w_router - w_router_hi.astype(F32)).astype(BF16)])
        b_router = jnp.pad(jnp.concatenate([b_group[l], b_expert[l]]),
                           (0, LANES - N_GROUPS - N_EXPERTS))[None]
        x2d, hm, route, route_t, counts = _post_mixer(
            y_a, y_b, ga, gb, x2d, w_branch_a[l].astype(BF16), w_branch_b[l].astype(BF16),
            w_out[l].astype(BF16), g_cross[l][None], w_cq[l].astype(BF16), kc, vc,
            w_co[l].astype(BF16), g_ffn[l][None], w_router, b_router,
            n_batch, seq, mem_len, tm=512)

        row_tok, dest, block_expert, n_used = _route(route_t, counts, n_tok)
        xs = hm[row_tok]
        ys = _experts(block_expert, n_used, xs, w_exp_gate[l], w_exp_up[l], w_exp_down[l])
        last = l + 1 == depth
        x2d = _final(x2d, ys[dest[0]], ys[dest[1]], route, g_final[None], tm=512, normalize=last)
    return x2d.reshape(n_batch, seq, d)
```

```python
import functools
import math

import numpy as np
import jax
import jax.numpy as jnp
from jax import lax
from jax.experimental import pallas as pl
from jax.experimental.pallas import tpu as pltpu

HEAD_DIM = 64
LANES = 128
FOX_HEADS = 8
DIFF_HEADS = 4
CROSS_HEADS = 4
N_GROUPS = 4
EXPERTS_PER_GROUP = 8
N_EXPERTS = N_GROUPS * EXPERTS_PER_GROUP
TOP_K = 2
MOE_BLOCK = 512
ATTN_Q_BLOCK = 1024
ATTN_K_BLOCK = 256
NORM_EPS = 1e-6
SUBLN_EPS = 1e-5
QK_SCALE = HEAD_DIM ** -0.5
LOG2E = math.log2(math.e)
Q_PRESCALE = QK_SCALE * LOG2E
NEG_BIG = -1e30
VMEM_LIMIT = 56 * 2**20

BF16 = jnp.bfloat16
F32 = jnp.float32


def _rms(t, g, eps):
    return t * lax.rsqrt(jnp.mean(t * t, axis=-1, keepdims=True) + eps) * g


def _dot(a, b):
    return jnp.dot(a, b, preferred_element_type=F32)


def _dot_nt(a, b):
    return lax.dot_general(a, b, (((1,), (1,)), ((), ())), preferred_element_type=F32)


def _params(*sem):
    return pltpu.CompilerParams(dimension_semantics=sem, vmem_limit_bytes=VMEM_LIMIT)


def _mem_kv_kernel(mem_ref, g_ref, w_ref, k_ref, v_ref):
    h = _rms(mem_ref[...], g_ref[...], NORM_EPS).astype(BF16)
    kv = _dot(h, w_ref[...])
    width = k_ref.shape[1]
    k_ref[...] = kv[:, :width].astype(BF16)
    v_ref[...] = kv[:, width:].astype(BF16)


def _mem_kv(mem2, g_mem, w_ckv, n_batch, mem_len):
    d = mem2.shape[1]
    cw = w_ckv.shape[1] // 2
    out = jax.ShapeDtypeStruct((n_batch * mem_len, cw), BF16)
    return pl.pallas_call(
        _mem_kv_kernel,
        grid=(n_batch,),
        in_specs=[pl.BlockSpec((mem_len, d), lambda b: (b, 0)),
                  pl.BlockSpec((1, d), lambda b: (0, 0)),
                  pl.BlockSpec((d, 2 * cw), lambda b: (0, 0))],
        out_specs=[pl.BlockSpec((mem_len, cw), lambda b: (b, 0))] * 2,
        out_shape=[out, out],
        compiler_params=_params("arbitrary"),
        name="mem_kv",
    )(mem2, g_mem, w_ckv)


def _rope(t, cos, sin_signed, first_half):
    fwd = pltpu.roll(t, LANES - HEAD_DIM // 2, axis=1)
    bwd = pltpu.roll(t, HEAD_DIM // 2, axis=1)
    return t * cos + jnp.where(first_half, fwd, bwd) * sin_signed


def _in_proj_kernel(x_ref, g_ref, w_ref, wvt_ref, wf_ref, cos_ref, sin_ref,
                    fq_ref, fk_ref, dq_ref, dk_ref, ga_ref, gb_ref, fvt_ref, dvt_ref, fl_ref):
    h = _rms(x_ref[...], g_ref[...], NORM_EPS).astype(BF16)
    tm = h.shape[0]
    cw = fq_ref.shape[1]

    def proj(chunk):
        return _dot(h, w_ref[:, chunk * cw:(chunk + 1) * cw])

    fq_ref[...] = (proj(0) * Q_PRESCALE).astype(BF16)
    fk_ref[...] = proj(1).astype(BF16)

    cos = cos_ref[...]
    sin = sin_ref[...]
    lane = lax.broadcasted_iota(jnp.int32, (tm, LANES), 1)
    first_half = (lane % HEAD_DIM) < HEAD_DIM // 2
    for out_ref, chunk, scale in ((dq_ref, 2, Q_PRESCALE), (dk_ref, 3, 1.0)):
        t = proj(chunk)
        for c in range(cw // LANES):
            blk = _rope(t[:, c * LANES:(c + 1) * LANES], cos, sin, first_half)
            out_ref[:, c * LANES:(c + 1) * LANES] = (blk * scale).astype(BF16)

    for out_ref, chunk in ((ga_ref, 4), (gb_ref, 6)):
        for c in range(2):
            out_ref[:, c * cw:(c + 1) * cw] = proj(chunk + c).astype(BF16)
    fvt_ref[...] = _dot_nt(wvt_ref[0:cw, :], h).astype(BF16)
    dvt_ref[...] = _dot_nt(wvt_ref[cw:2 * cw, :], h).astype(BF16)
    fl_ref[...] = _dot(h, wf_ref[...])


def _in_proj(x2, g_mix, w_main, w_vt, w_f, cos_t, sin_t, tm):
    n_tok, d = x2.shape
    cw = 512
    tok = lambda i: (i, 0)
    tok_t = lambda i: (0, i)
    const = lambda i: (0, 0)
    o512 = jax.ShapeDtypeStruct((n_tok, cw), BF16)
    o1024 = jax.ShapeDtypeStruct((n_tok, 2 * cw), BF16)
    o512_t = jax.ShapeDtypeStruct((cw, n_tok), BF16)
    return pl.pallas_call(
        _in_proj_kernel,
        grid=(n_tok // tm,),
        in_specs=[pl.BlockSpec((tm, d), tok),
                  pl.BlockSpec((1, d), const),
                  pl.BlockSpec(w_main.shape, const, pipeline_mode=pl.Buffered(1)),
                  pl.BlockSpec(w_vt.shape, const, pipeline_mode=pl.Buffered(1)),
                  pl.BlockSpec(w_f.shape, const),
                  pl.BlockSpec((tm, LANES), tok),
                  pl.BlockSpec((tm, LANES), tok)],
        out_specs=[pl.BlockSpec((tm, cw), tok)] * 4
                  + [pl.BlockSpec((tm, 2 * cw), tok)] * 2
                  + [pl.BlockSpec((cw, tm), tok_t)] * 2
                  + [pl.BlockSpec((tm, LANES), tok)],
        out_shape=[o512] * 4 + [o1024] * 2 + [o512_t] * 2
                  + [jax.ShapeDtypeStruct((n_tok, LANES), F32)],
        compiler_params=_params("arbitrary"),
        name="in_proj",
    )(x2, g_mix, w_main, w_vt, w_f, cos_t, sin_t)


def _bias_lane_selectors():
    sel = np.zeros((3, LANES, FOX_HEADS // 2 * LANES), np.float32)
    for h in range(FOX_HEADS):
        base = (h // 2) * LANES + (HEAD_DIM if h % 2 == 0 else 0)
        for piece in range(3):
            sel[piece, h, base + piece] = 1.0
    return jnp.asarray(sel, BF16)


def _split3(t):
    pieces = []
    for _ in range(3):
        part = t.astype(BF16)
        t = t - part.astype(F32)
        pieces.append(part)
    return pieces


def _fox_prep_kernel(fl_ref, b_ref, tri_ref, sel_ref, out_ref, carry_ref):
    @pl.when(pl.program_id(1) == 0)
    def _():
        carry_ref[...] = jnp.zeros_like(carry_ref)

    z = fl_ref[...] + b_ref[...]
    log_f = jnp.minimum(z, 0.0) - jnp.log(1.0 + jnp.exp(-jnp.abs(z)))
    tc = z.shape[0]
    c = carry_ref[...]
    for part in _split3(log_f):
        c = c + _dot(tri_ref[...], part)
    carry_ref[...] = c[tc - 1:tc, :]
    acc = jnp.zeros(out_ref.shape, F32)
    for piece, part in enumerate(_split3(c * (-LOG2E))):
        acc = acc + _dot(part, sel_ref[piece])
    out_ref[...] = acc.astype(BF16)


def _fox_prep(fl, b_row, sel, n_batch, seq, tc):
    n_tok = fl.shape[0]
    nblk = seq // tc
    width = sel.shape[2]
    tri = jnp.tril(jnp.ones((tc, tc), BF16))
    return pl.pallas_call(
        _fox_prep_kernel,
        grid=(n_batch, nblk),
        in_specs=[pl.BlockSpec((tc, LANES), lambda b, j: (b * nblk + j, 0)),
                  pl.BlockSpec((1, LANES), lambda b, j: (0, 0)),
                  pl.BlockSpec((tc, tc), lambda b, j: (0, 0)),
                  pl.BlockSpec(sel.shape, lambda b, j: (0, 0, 0))],
        out_specs=pl.BlockSpec((tc, width), lambda b, j: (b * nblk + j, 0)),
        out_shape=jax.ShapeDtypeStruct((n_tok, width), BF16),
        scratch_shapes=[pltpu.VMEM((1, LANES), F32)],
        compiler_params=_params("arbitrary", "arbitrary"),
        name="fox_prep",
    )(fl, b_row, tri, sel)


def _attn_kernel(*refs, bq, bk, fox, lambda_init):
    for qi in range(refs[0].shape[0] // bq):
        _attn_q_block(qi, refs, bq, bk, fox, lambda_init)


def _attn_q_block(qi, refs, bq, bk, fox, lambda_init):
    if fox:
        q_ref, k_ref, vt_ref, bl_ref, out_ref, qs_ref, st_ref, p_ref, acc_ref, stat_ref, ka_ref, kb_ref = refs
    else:
        q_ref, k_ref, vt_ref, lam_ref, gsub_ref, out_ref, qs_ref, st_ref, p_ref, acc_ref, stat_ref = refs
    q = q_ref[qi * bq:(qi + 1) * bq, :].astype(F32)
    lane = lax.broadcasted_iota(jnp.int32, q.shape, 1)
    qs_ref = qs_ref.at[qi % 2]
    if fox:
        if qi == 0:
            k = k_ref[...].astype(F32)
            bl = bl_ref[...].astype(F32)
            klane = lax.broadcasted_iota(jnp.int32, k.shape, 1)
            ka_ref[...] = jnp.where(klane < HEAD_DIM, k, bl).astype(BF16)
            kb_ref[...] = jnp.where(klane >= HEAD_DIM, k, bl).astype(BF16)

        q_streams = (jnp.where(lane < HEAD_DIM, q, jnp.where(lane < HEAD_DIM + 3, 1.0, 0.0)),
                     jnp.where(lane >= HEAD_DIM, q, jnp.where(lane < 3, 1.0, 0.0)))
        k_srcs = (ka_ref, kb_ref)
        v_rows = ((0, HEAD_DIM), (HEAD_DIM, 2 * HEAD_DIM))
    else:
        q_streams = (jnp.where(lane < HEAD_DIM, q, 0.0), jnp.where(lane >= HEAD_DIM, q, 0.0))
        k_srcs = (k_ref, k_ref)
        v_rows = ((0, 2 * HEAD_DIM), (0, 2 * HEAD_DIM))
    n_v = v_rows[0][1] - v_rows[0][0]
    ring = bq // bk
    assert ring * bk == bq and ring % 2 == 0 and ring >= 4
    for idx in range(2):
        qs_ref[idx] = q_streams[idx].astype(BF16)
        p_ref[idx, 1] = jnp.zeros((bk, bq), BF16)
        acc_ref[idx] = jnp.zeros((n_v + 16, bq), F32)
    ones_rows = jnp.ones((16, bk), BF16)
    first_diag = ring * qi

    def key_start(kj):
        return kj * bk if isinstance(kj, int) else pl.multiple_of(kj * bk, bk)

    def qk_stage(kj, idx, slot, q_lo=0):
        k0 = key_start(kj)
        st_ref[idx, slot, :, q_lo:] = _dot_nt(k_srcs[idx][pl.ds(k0, bk), :], qs_ref[idx, q_lo:, :])

    M_ROW, A_ROW = 0, 8

    def pv_stage(kj, idx, pslot, q_lo=0):
        k0 = key_start(kj)
        r0, r1 = v_rows[idx]
        v_aug = jnp.concatenate([vt_ref[r0:r1, pl.ds(k0, bk)], ones_rows], axis=0)
        alpha = stat_ref[idx, A_ROW + 8 * pslot:A_ROW + 8 * pslot + 1, q_lo:]
        acc_ref[idx, :, q_lo:] = (alpha * acc_ref[idx, :, q_lo:]
                                  + _dot(v_aug, p_ref[idx, pslot, :, q_lo:]))

    def softmax_stage(idx, slot, pslot, masked, q_lo=0):
        st = st_ref[idx, slot, :, q_lo:]
        if masked:
            key_idx = lax.broadcasted_iota(jnp.int32, st.shape, 0)
            qry_idx = lax.broadcasted_iota(jnp.int32, st.shape, 1)
            st = jnp.where(key_idx <= qry_idx, st, NEG_BIG)
        m = stat_ref[idx, M_ROW:M_ROW + 1, q_lo:]
        m_new = jnp.maximum(m, jnp.max(st, axis=0, keepdims=True))
        p_ref[idx, pslot, :, q_lo:] = jnp.exp2(st - m_new).astype(BF16)
        stat_ref[idx, M_ROW:M_ROW + 1, q_lo:] = m_new
        stat_ref[idx, A_ROW + 8 * pslot:A_ROW + 8 * pslot + 1, q_lo:] = jnp.exp2(m - m_new)

    for idx in range(2):
        stat_ref[idx, M_ROW:M_ROW + 8, :] = jnp.full((8, bq), NEG_BIG, F32)
        stat_ref[idx, A_ROW:A_ROW + 16, :] = jnp.ones((16, bq), F32)
        qk_stage(0, idx, 0)
        qk_stage(1, idx, 1)

    def block_group(j, diag_group):
        for s in range(ring):
            lo_prev = (s - 1) * bk if diag_group and s > 0 else 0
            lo_this = s * bk if diag_group else 0
            for idx in range(2):
                if not diag_group:
                    qk_stage(j + s + 2, idx, (s + 2) % ring)
                elif s + 2 < ring:
                    qk_stage(j + s + 2, idx, s + 2, (s + 2) * bk)
                prev = j + s - 1
                prev = max(prev, 0) if isinstance(prev, int) else jnp.maximum(prev, 0)
                pv_stage(prev, idx, (s - 1) % 2, lo_prev)
                softmax_stage(idx, s, s % 2, diag_group, lo_this)

    def body(i, carry):
        block_group(ring * i, False)
        return carry

    if qi > 0:
        lax.fori_loop(0, qi, body, 0)
    block_group(first_diag, True)
    outs = []
    for idx in range(2):
        pv_stage(first_diag + ring - 1, idx, (ring - 1) % 2, (ring - 1) * bk)
        acc = acc_ref[idx]
        outs.append(acc[0:n_v] * (1.0 / acc[n_v:n_v + 1]))
    o_a, o_b = outs
    if fox:
        y = jnp.concatenate([o_a, o_b], axis=0)
    else:
        lam_vec = lam_ref[...]
        lam = (jnp.exp(jnp.sum(lam_vec[0:1] * lam_vec[1:2], axis=1, keepdims=True))
               - jnp.exp(jnp.sum(lam_vec[2:3] * lam_vec[3:4], axis=1, keepdims=True))
               + lambda_init)
        y = o_a - lam * o_b
        y = y * lax.rsqrt(jnp.mean(y * y, axis=0, keepdims=True) + SUBLN_EPS)
        y = y * (gsub_ref[...] * (1.0 - lambda_init))
    out_ref[qi * bq:(qi + 1) * bq, :] = y.T.astype(out_ref.dtype)


def _attention(q, k, vt, extra, n_batch, seq, fox, lambda_init=0.0):
    n_tok, width = q.shape
    n_pairs = width // LANES
    bq, bk = ATTN_Q_BLOCK, ATTN_K_BLOCK
    n_v = HEAD_DIM if fox else 2 * HEAD_DIM
    k_spec = pl.BlockSpec((seq, LANES), lambda b, hp: (b, hp))
    q_spec = k_spec
    vt_spec = pl.BlockSpec((LANES, seq), lambda b, hp: (hp, b))
    scratch = [pltpu.VMEM((2, 2, bq, LANES), BF16),
               pltpu.VMEM((2, bq // bk, bk, bq), F32),
               pltpu.VMEM((2, 2, bk, bq), BF16),
               pltpu.VMEM((2, n_v + 16, bq), F32),
               pltpu.VMEM((2, 24, bq), F32)]
    if fox:
        extra_specs = [k_spec]
        scratch += [pltpu.VMEM((seq, LANES), BF16)] * 2
    else:
        lam_vec, g_sub = extra
        extra_specs = [pl.BlockSpec(lam_vec.shape, lambda b, hp: (0, 0)),
                       pl.BlockSpec(g_sub.shape, lambda b, hp: (0, 0))]
    return pl.pallas_call(
        functools.partial(_attn_kernel, bq=bq, bk=bk, fox=fox, lambda_init=lambda_init),
        grid=(n_batch, n_pairs),
        in_specs=[q_spec, k_spec, vt_spec] + extra_specs,
        out_specs=q_spec,
        out_shape=jax.ShapeDtypeStruct((n_tok, width), BF16),
        scratch_shapes=scratch,
        compiler_params=_params("arbitrary", "arbitrary"),
        name="fox_attn" if fox else "diff_attn",
    )(q, k, vt, *extra)


def _route_block(logits, tri_ref, cnt_ref):
    lane = lax.broadcasted_iota(jnp.int32, logits.shape, 1)
    lane_f = lane.astype(F32)
    last = float(LANES - 1)

    def first_argmax(t):
        top = jnp.max(t, axis=1, keepdims=True)
        return top, jnp.min(jnp.where(t == top, lane_f, last), axis=1, keepdims=True)

    is_group = lane < N_GROUPS
    g_max, g_sel = first_argmax(jnp.where(is_group, logits, NEG_BIG))
    g_w = 1.0 / jnp.sum(jnp.where(is_group, jnp.exp(logits - g_max), 0.0), axis=1, keepdims=True)
    lo = N_GROUPS + EXPERTS_PER_GROUP * g_sel
    scores = jnp.where((lane_f >= lo) & (lane_f < lo + EXPERTS_PER_GROUP), logits, NEG_BIG)
    v1, i1 = first_argmax(scores)
    v2, i2 = first_argmax(jnp.where(lane_f == i1, NEG_BIG, scores))
    t = jnp.exp(v2 - v1)
    w1 = g_w / (1.0 + t)
    w2 = w1 * t
    e1 = i1 - N_GROUPS
    e2 = i2 - N_GROUPS

    hit1 = lane_f == e1
    hit2 = lane_f == e2
    onehot = jnp.where(hit1 | hit2, 1.0, 0.0)
    before = _dot(tri_ref[...], onehot.astype(BF16)) + cnt_ref[...]
    r1 = jnp.sum(jnp.where(hit1, before, 0.0), axis=1, keepdims=True)
    r2 = jnp.sum(jnp.where(hit2, before, 0.0), axis=1, keepdims=True)
    n = logits.shape[0]
    cnt_ref[...] = before[n - 1:n, :] + onehot[n - 1:n, :]
    out = jnp.zeros(logits.shape, F32)
    for pos, val in enumerate((e1, e2, r1, r2, w1, w2)):
        out = jnp.where(lane == pos, val, out)
    return out


def _post_mixer_kernel(ya_ref, yb_ref, ga_ref, gb_ref, x_ref, wa_ref, wb_ref, wo_ref,
                       gc_ref, wcq_ref, kc_ref, vc_ref, wco_ref, gf_ref, wr_ref, br_ref, tri_ref,
                       x2_ref, hm_ref, route_ref, route_t_ref, cnt_out_ref, cnt_ref):
    @pl.when((pl.program_id(0) == 0) & (pl.program_id(1) == 0))
    def _():
        cnt_ref[...] = jnp.zeros_like(cnt_ref)

    def sigmoid(t):
        return 1.0 / (1.0 + jnp.exp(-t))

    merged = (sigmoid(ga_ref[...].astype(F32)) * _dot(ya_ref[...], wa_ref[...])
              + sigmoid(gb_ref[...].astype(F32)) * _dot(yb_ref[...], wb_ref[...]))
    x1 = x_ref[...] + _dot(merged.astype(BF16), wo_ref[...])

    hx = _rms(x1, gc_ref[...], NORM_EPS).astype(BF16)
    qc = (_dot(hx, wcq_ref[...]) * QK_SCALE).astype(BF16)
    kc = kc_ref[...]
    vc = vc_ref[...]
    lane = lax.broadcasted_iota(jnp.int32, qc.shape, 1)
    zero = jnp.zeros_like(qc)
    o = jnp.zeros(qc.shape, F32)
    for h in range(CROSS_HEADS):
        in_head = (lane >= h * HEAD_DIM) & (lane < (h + 1) * HEAD_DIM)
        s = _dot_nt(jnp.where(in_head, qc, zero), kc)
        p = jnp.exp(s - jnp.max(s, axis=1, keepdims=True))
        p = p * (1.0 / jnp.sum(p, axis=1, keepdims=True))
        o = jnp.where(in_head, _dot(p.astype(BF16), vc), o)
    x2 = x1 + _dot(o.astype(BF16), wco_ref[...])
    x2_ref[...] = x2

    hm = _rms(x2, gf_ref[...], NORM_EPS)
    hm_ref[...] = hm.astype(BF16)
    hm_hi = hm.astype(BF16)
    hm_lo = (hm - hm_hi.astype(F32)).astype(BF16)
    logits = (_dot(hm_hi, wr_ref[0]) + _dot(hm_hi, wr_ref[1]) + _dot(hm_lo, wr_ref[0])
              + br_ref[...])
    route = _route_block(logits, tri_ref, cnt_ref)
    route_ref[...] = route
    route_t_ref[...] = route.T[0:8, :]
    cnt_out_ref[...] = jnp.broadcast_to(cnt_ref[...], cnt_out_ref.shape)


def _post_mixer(ya, yb, ga, gb, x2d, wa, wb, wo, g_cross, wcq, kc, vc, wco, g_ffn, wr, br,
                n_batch, seq, mem_len, tm):
    n_tok, d = x2d.shape
    nblk = seq // tm
    tok = lambda b, j: (b * nblk + j, 0)
    const = lambda b, j: (0, 0)
    full = lambda a: pl.BlockSpec(a.shape, const)
    tri = jnp.tril(jnp.ones((tm, tm), BF16), -1)
    return pl.pallas_call(
        _post_mixer_kernel,
        grid=(n_batch, nblk),
        in_specs=[pl.BlockSpec((tm, ya.shape[1]), tok), pl.BlockSpec((tm, yb.shape[1]), tok),
                  pl.BlockSpec((tm, d), tok), pl.BlockSpec((tm, d), tok), pl.BlockSpec((tm, d), tok),
                  full(wa), full(wb), full(wo), full(g_cross), full(wcq),
                  pl.BlockSpec((mem_len, kc.shape[1]), lambda b, j: (b, 0)),
                  pl.BlockSpec((mem_len, vc.shape[1]), lambda b, j: (b, 0)),
                  full(wco), full(g_ffn), pl.BlockSpec(wr.shape, lambda b, j: (0, 0, 0)), full(br),
                  full(tri)],
        out_specs=[pl.BlockSpec((tm, d), tok), pl.BlockSpec((tm, d), tok),
                   pl.BlockSpec((tm, LANES), tok),
                   pl.BlockSpec((8, tm), lambda b, j: (0, b * nblk + j)),
                   pl.BlockSpec((8, LANES), const)],
        out_shape=[jax.ShapeDtypeStruct((n_tok, d), F32),
                   jax.ShapeDtypeStruct((n_tok, d), BF16),
                   jax.ShapeDtypeStruct((n_tok, LANES), F32),
                   jax.ShapeDtypeStruct((8, n_tok), F32),
                   jax.ShapeDtypeStruct((8, LANES), F32)],
        scratch_shapes=[pltpu.VMEM((1, LANES), F32)],
        compiler_params=_params("arbitrary", "arbitrary"),
        name="post_mixer",
    )(ya, yb, ga, gb, x2d, wa, wb, wo, g_cross, wcq, kc, vc, wco, g_ffn, wr, br, tri)


def _expert_kernel(be_ref, nused_ref, xs_ref, wg_ref, wu_ref, wd_ref, ys_ref, wg_s, wu_s, wd_s):
    i = pl.program_id(0)

    @pl.when((i == 0) | (be_ref[i] != be_ref[jnp.maximum(i - 1, 0)]))
    def _():
        wg_s[...] = wg_ref[0].astype(BF16)
        wu_s[...] = wu_ref[0].astype(BF16)
        wd_s[...] = wd_ref[0].astype(BF16)

    @pl.when(i < nused_ref[0])
    def _():
        xb = xs_ref[...]
        g = _dot(xb, wg_s[...])
        u = _dot(xb, wu_s[...])
        hb = (g * (1.0 / (1.0 + jnp.exp(-g))) * u).astype(BF16)
        ys_ref[...] = _dot(hb, wd_s[...]).astype(ys_ref.dtype)

    @pl.when(i >= nused_ref[0])
    def _():
        ys_ref[...] = jnp.zeros_like(ys_ref)


def _experts(block_expert, n_used, xs, wg, wu, wd):
    n_rows, d = xs.shape
    n_blocks = n_rows // MOE_BLOCK
    ff = wg.shape[2]
    grid_spec = pltpu.PrefetchScalarGridSpec(
        num_scalar_prefetch=2,
        grid=(n_blocks,),
        in_specs=[pl.BlockSpec((MOE_BLOCK, d), lambda i, be, nu: (i, 0)),
                  pl.BlockSpec((1, d, ff), lambda i, be, nu: (be[i], 0, 0)),
                  pl.BlockSpec((1, d, ff), lambda i, be, nu: (be[i], 0, 0)),
                  pl.BlockSpec((1, ff, d), lambda i, be, nu: (be[i], 0, 0))],
        out_specs=pl.BlockSpec((MOE_BLOCK, d), lambda i, be, nu: (i, 0)),
        scratch_shapes=[pltpu.VMEM((d, ff), BF16), pltpu.VMEM((d, ff), BF16),
                        pltpu.VMEM((ff, d), BF16)],
    )
    return pl.pallas_call(
        _expert_kernel,
        grid_spec=grid_spec,
        out_shape=jax.ShapeDtypeStruct((n_rows, d), BF16),
        compiler_params=_params("arbitrary"),
        name="experts",
    )(block_expert, n_used, xs, wg, wu, wd)


def _final_kernel(x_ref, y0_ref, y1_ref, route_ref, g_ref, out_ref, *, normalize):
    route = route_ref[...]
    w0 = route[:, 2 * TOP_K:2 * TOP_K + 1]
    w1 = route[:, 2 * TOP_K + 1:2 * TOP_K + 2]
    out = x_ref[...] + w0 * y0_ref[...].astype(F32) + w1 * y1_ref[...].astype(F32)
    out_ref[...] = _rms(out, g_ref[...], NORM_EPS) if normalize else out


def _final(x2, y0, y1, route, g_final, tm, normalize):
    n_tok, d = x2.shape
    tok = lambda i: (i, 0)
    return pl.pallas_call(
        functools.partial(_final_kernel, normalize=normalize),
        grid=(n_tok // tm,),
        in_specs=[pl.BlockSpec((tm, d), tok), pl.BlockSpec((tm, d), tok), pl.BlockSpec((tm, d), tok),
                  pl.BlockSpec((tm, LANES), tok), pl.BlockSpec((1, d), lambda i: (0, 0))],
        out_specs=pl.BlockSpec((tm, d), tok),
        out_shape=jax.ShapeDtypeStruct((n_tok, d), F32),
        compiler_params=_params("arbitrary"),
        name="final_norm",
    )(x2, y0, y1, route, g_final)


def _route(route_t, counts, n_tok):
    n_assign = n_tok * TOP_K
    flat_e = route_t[0:TOP_K].astype(jnp.int32).reshape(n_assign)
    rank = route_t[TOP_K:2 * TOP_K].astype(jnp.int32).reshape(n_assign)
    counts = counts[0, :N_EXPERTS].astype(jnp.int32)
    experts = jnp.arange(N_EXPERTS, dtype=jnp.int32)

    def lookup(table, idx):
        return jnp.sum(jnp.where(idx[:, None] == experts[None, :], table[None, :], 0), axis=1)

    seg_start = jnp.cumsum(counts) - counts
    padded = (counts + MOE_BLOCK - 1) // MOE_BLOCK * MOE_BLOCK
    pad_end = jnp.cumsum(padded)
    pad_start = pad_end - padded
    dest = lookup(pad_start, flat_e) + rank
    order = jnp.argsort(lookup(seg_start, flat_e) + rank).astype(jnp.int32)

    n_blocks = -(-n_assign // MOE_BLOCK) + N_EXPERTS
    n_rows = n_blocks * MOE_BLOCK
    block_start = jnp.arange(n_blocks, dtype=jnp.int32) * MOE_BLOCK
    block_expert = jnp.minimum(jnp.sum(block_start[:, None] >= pad_end[None, :], axis=1),
                               N_EXPERTS - 1).astype(jnp.int32)
    offset = block_start - pad_start[block_expert]
    in_block = jnp.arange(MOE_BLOCK, dtype=jnp.int32)[None, :]
    valid = in_block < (counts[block_expert] - offset)[:, None]
    src = jnp.clip((seg_start[block_expert] + offset)[:, None] + in_block, 0, n_assign - 1)
    rows = block_start[:, None] + in_block
    row_tok = jnp.where(valid, order[src] % n_tok, rows % n_tok).reshape(n_rows)
    n_used = (pad_end[-1] // MOE_BLOCK).astype(jnp.int32).reshape(1)
    return row_tok, dest.reshape(TOP_K, n_tok), block_expert, n_used


def kernel(x, mem, positions, g_mix, w_in, b_fgate, w_branch_a, w_branch_b, w_out, lambda_q1, lambda_k1, lambda_q2, lambda_k2, g_diff_sub, g_cross, g_mem, w_cq, w_ckv, w_co, g_ffn, w_group, b_group, w_expert, b_expert, w_exp_gate, w_exp_up, w_exp_down, g_final):
    n_batch, seq, d = x.shape
    mem_len = mem.shape[1]
    depth = g_mix.shape[0]
    n_tok = n_batch * seq
    fox_w = FOX_HEADS * HEAD_DIM
    diff_w = DIFF_HEADS * 2 * HEAD_DIM

    half = HEAD_DIM // 2
    inv_freq = 10000.0 ** (-jnp.arange(half, dtype=F32) * 2.0 / HEAD_DIM)
    ang = positions.astype(F32).reshape(n_tok, 1) * jnp.tile(inv_freq, LANES // half)[None, :]
    sign = jnp.tile(jnp.concatenate([-jnp.ones((half,), F32), jnp.ones((half,), F32)]),
                    LANES // HEAD_DIM)
    cos_t = jnp.cos(ang)
    sin_t = jnp.sin(ang) * sign[None, :]
    sel = _bias_lane_selectors()

    x2d = x.reshape(n_tok, d)
    mem2d = mem.reshape(n_batch * mem_len, d)
    for l in range(depth):
        lambda_init = 0.8 - 0.6 * math.exp(-0.3 * l)
        o_fv = 2 * fox_w
        o_fl = o_fv + fox_w
        o_dq = o_fl + FOX_HEADS
        o_dv = o_dq + 2 * diff_w
        o_ga = o_dv + diff_w
        wl = w_in[l]
        w_main = jnp.concatenate([wl[:, :o_fv], wl[:, o_dq:o_dv], wl[:, o_ga:]], axis=1).astype(BF16)
        w_vt = jnp.concatenate([wl[:, o_fv:o_fl], wl[:, o_dv:o_ga]], axis=1).T.astype(BF16)
        w_f = jnp.pad(wl[:, o_fl:o_dq], ((0, 0), (0, LANES - FOX_HEADS))).astype(BF16)
        b_row = jnp.pad(b_fgate[l], (0, LANES - FOX_HEADS))[None]

        kc, vc = _mem_kv(mem2d, g_mem[l][None], w_ckv[l].astype(BF16), n_batch, mem_len)
        fq, fk, dq, dk, ga, gb, fvt, dvt, fl = _in_proj(
            x2d, g_mix[l][None], w_main, w_vt, w_f, cos_t, sin_t, tm=512)
        bias_lanes = _fox_prep(fl, b_row, sel, n_batch, seq, tc=512)
        y_a = _attention(fq, fk, fvt, (bias_lanes,), n_batch, seq, fox=True)
        lam_vec = jnp.stack([lambda_q1[l], lambda_k1[l], lambda_q2[l], lambda_k2[l]])
        y_b = _attention(dq, dk, dvt, (lam_vec, g_diff_sub[l][:, None]), n_batch, seq,
                         fox=False, lambda_init=lambda_init)

        w_router = jnp.pad(jnp.concatenate([w_group[l], w_expert[l]], axis=1),
                           ((0, 0), (0, LANES - N_GROUPS - N_EXPERTS)))
        w_router_hi = w_router.astype(BF16)
        w_router = jnp.stack([w_router_hi, (w_router - w_router_hi.astype(F32)).astype(BF16)])
        b_router = jnp.pad(jnp.concatenate([b_group[l], b_expert[l]]),
                           (0, LANES - N_GROUPS - N_EXPERTS))[None]
        x2d, hm, route, route_t, counts = _post_mixer(
            y_a, y_b, ga, gb, x2d, w_branch_a[l].astype(BF16), w_branch_b[l].astype(BF16),
            w_out[l].astype(BF16), g_cross[l][None], w_cq[l].astype(BF16), kc, vc,
            w_co[l].astype(BF16), g_ffn[l][None], w_router, b_router,
            n_batch, seq, mem_len, tm=512)

        row_tok, dest, block_expert, n_used = _route(route_t, counts, n_tok)
        xs = hm[row_tok]
        ys = _experts(block_expert, n_used, xs, w_exp_gate[l], w_exp_up[l], w_exp_down[l])
        last = l + 1 == depth
        x2d = _final(x2d, ys[dest[0]], ys[dest[1]], route, g_final[None], tm=512, normalize=last)
    return x2d.reshape(n_batch, seq, d)
```

```python
import functools
import math

import numpy as np
import jax
import jax.numpy as jnp
from jax import lax
from jax.experimental import pallas as pl
from jax.experimental.pallas import tpu as pltpu

HEAD_DIM = 64
LANES = 128
FOX_HEADS = 8
DIFF_HEADS = 4
CROSS_HEADS = 4
N_GROUPS = 4
EXPERTS_PER_GROUP = 8
N_EXPERTS = N_GROUPS * EXPERTS_PER_GROUP
TOP_K = 2
MOE_BLOCK = 512
ATTN_Q_BLOCK = 1024
ATTN_K_BLOCK = 256
NORM_EPS = 1e-6
SUBLN_EPS = 1e-5
QK_SCALE = HEAD_DIM ** -0.5
LOG2E = math.log2(math.e)
Q_PRESCALE = QK_SCALE * LOG2E
NEG_BIG = -1e30
VMEM_LIMIT = 56 * 2**20

BF16 = jnp.bfloat16
F32 = jnp.float32


def _rms(t, g, eps):
    return t * lax.rsqrt(jnp.mean(t * t, axis=-1, keepdims=True) + eps) * g


def _sigmoid(t):
    return 0.5 + 0.5 * jnp.tanh(0.5 * t)


def _dot(a, b):
    return jnp.dot(a, b, preferred_element_type=F32)


def _dot_nt(a, b):
    return lax.dot_general(a, b, (((1,), (1,)), ((), ())), preferred_element_type=F32)


def _params(*sem):
    return pltpu.CompilerParams(dimension_semantics=sem, vmem_limit_bytes=VMEM_LIMIT)


def _mem_kv_kernel(mem_ref, g_ref, w_ref, k_ref, v_ref):
    h = _rms(mem_ref[...], g_ref[...], NORM_EPS).astype(BF16)
    kv = _dot(h, w_ref[...])
    width = k_ref.shape[1]
    k_ref[...] = kv[:, :width].astype(BF16)
    v_ref[...] = kv[:, width:].astype(BF16)


def _mem_kv(mem2, g_mem, w_ckv, n_batch, mem_len):
    d = mem2.shape[1]
    cw = w_ckv.shape[1] // 2
    out = jax.ShapeDtypeStruct((n_batch * mem_len, cw), BF16)
    return pl.pallas_call(
        _mem_kv_kernel,
        grid=(n_batch,),
        in_specs=[pl.BlockSpec((mem_len, d), lambda b: (b, 0)),
                  pl.BlockSpec((1, d), lambda b: (0, 0)),
                  pl.BlockSpec((d, 2 * cw), lambda b: (0, 0))],
        out_specs=[pl.BlockSpec((mem_len, cw), lambda b: (b, 0))] * 2,
        out_shape=[out, out],
        compiler_params=_params("arbitrary"),
        name="mem_kv",
    )(mem2, g_mem, w_ckv)


def _rope(t, cos, sin_signed, first_half):
    fwd = pltpu.roll(t, LANES - HEAD_DIM // 2, axis=1)
    bwd = pltpu.roll(t, HEAD_DIM // 2, axis=1)
    return t * cos + jnp.where(first_half, fwd, bwd) * sin_signed


def _in_proj_kernel(x_ref, g_ref, w_ref, wvt_ref, wf_ref, rot_ref,
                    fq_ref, fk_ref, dq_ref, dk_ref, ga_ref, gb_ref, fvt_ref, dvt_ref, fl_ref):
    h = _rms(x_ref[...], g_ref[...], NORM_EPS).astype(BF16)
    tm = h.shape[0]
    cw = fq_ref.shape[1]

    def proj(chunk):
        return _dot(h, w_ref[:, chunk * cw:(chunk + 1) * cw])

    fq_ref[...] = (proj(0) * Q_PRESCALE).astype(BF16)
    fk_ref[...] = proj(1).astype(BF16)

    lane = lax.broadcasted_iota(jnp.int32, (tm, LANES), 1)
    rot = rot_ref[...]
    quarter = HEAD_DIM // 2
    c = jnp.where(lane < quarter, rot, 0.0)
    s = jnp.where((lane >= quarter) & (lane < 2 * quarter), rot, 0.0)
    cos = c + pltpu.roll(c, quarter, axis=1) + pltpu.roll(c, 2 * quarter, axis=1) + pltpu.roll(c, 3 * quarter, axis=1)
    sin = (s + pltpu.roll(s, 2 * quarter, axis=1)) - (pltpu.roll(s, quarter, axis=1) + pltpu.roll(s, 3 * quarter, axis=1))
    first_half = (lane % HEAD_DIM) < HEAD_DIM // 2
    for out_ref, chunk, scale in ((dq_ref, 2, Q_PRESCALE), (dk_ref, 3, 1.0)):
        t = proj(chunk)
        for c in range(cw // LANES):
            blk = _rope(t[:, c * LANES:(c + 1) * LANES], cos, sin, first_half)
            out_ref[:, c * LANES:(c + 1) * LANES] = (blk * scale).astype(BF16)

    for out_ref, chunk in ((ga_ref, 4), (gb_ref, 6)):
        for c in range(2):
            out_ref[:, c * cw:(c + 1) * cw] = proj(chunk + c).astype(BF16)
    fvt_ref[...] = _dot_nt(wvt_ref[0:cw, :], h).astype(BF16)
    dvt_ref[...] = _dot_nt(wvt_ref[cw:2 * cw, :], h).astype(BF16)
    fl_ref[...] = _dot(h, wf_ref[...])


def _in_proj(x2, g_mix, w_main, w_vt, w_f, rot_t, tm):
    n_tok, d = x2.shape
    cw = 512
    tok = lambda i: (i, 0)
    tok_t = lambda i: (0, i)
    const = lambda i: (0, 0)
    o512 = jax.ShapeDtypeStruct((n_tok, cw), BF16)
    o1024 = jax.ShapeDtypeStruct((n_tok, 2 * cw), BF16)
    o512_t = jax.ShapeDtypeStruct((cw, n_tok), BF16)
    return pl.pallas_call(
        _in_proj_kernel,
        grid=(n_tok // tm,),
        in_specs=[pl.BlockSpec((tm, d), tok),
                  pl.BlockSpec((1, d), const),
                  pl.BlockSpec(w_main.shape, const, pipeline_mode=pl.Buffered(1)),
                  pl.BlockSpec(w_vt.shape, const, pipeline_mode=pl.Buffered(1)),
                  pl.BlockSpec(w_f.shape, const),
                  pl.BlockSpec((tm, LANES), tok)],
        out_specs=[pl.BlockSpec((tm, cw), tok)] * 4
                  + [pl.BlockSpec((tm, 2 * cw), tok)] * 2
                  + [pl.BlockSpec((cw, tm), tok_t)] * 2
                  + [pl.BlockSpec((tm, LANES), tok)],
        out_shape=[o512] * 4 + [o1024] * 2 + [o512_t] * 2
                  + [jax.ShapeDtypeStruct((n_tok, LANES), F32)],
        compiler_params=_params("arbitrary"),
        name="in_proj",
    )(x2, g_mix, w_main, w_vt, w_f, rot_t)


HEAD_ROWS = 16


def _bias_lane_selectors():
    sel = np.zeros((LANES, FOX_HEADS // 2 * LANES), np.float32)
    for h in range(FOX_HEADS):
        base = (h // 2) * LANES + (HEAD_DIM if h % 2 == 0 else 0)
        for piece in range(3):
            sel[HEAD_ROWS * piece + h, base + piece] = 1.0
    return jnp.asarray(sel, BF16)


def _split3(t):
    pieces = []
    for _ in range(3):
        part = t.astype(BF16)
        t = t - part.astype(F32)
        pieces.append(part)
    return pieces


def _fox_prep_kernel(fl_ref, b_ref, tri_ref, sel_ref, out_ref, carry_ref):
    @pl.when(pl.program_id(1) == 0)
    def _():
        carry_ref[...] = jnp.zeros_like(carry_ref)

    z = fl_ref[...] + b_ref[...]
    log_f = jnp.minimum(z, 0.0) - jnp.log(1.0 + jnp.exp(-jnp.abs(z)))
    tc = z.shape[0]
    log_f_t = log_f.T[0:HEAD_ROWS, :]
    c = carry_ref[...]
    for part in _split3(log_f_t):
        c = c + _dot(part, tri_ref[...])
    carry_ref[...] = c[:, tc - 1:tc]
    pieces = [p.astype(F32) for p in _split3(c * (-LOG2E))]
    stacked = jnp.concatenate(pieces + [jnp.zeros((LANES - 3 * HEAD_ROWS, tc), F32)], axis=0)
    out_ref[...] = _dot(stacked.T.astype(BF16), sel_ref[...]).astype(BF16)


def _fox_prep(fl, b_row, sel, n_batch, seq, tc):
    n_tok = fl.shape[0]
    nblk = seq // tc
    width = sel.shape[1]
    tri = jnp.triu(jnp.ones((tc, tc), BF16))
    return pl.pallas_call(
        _fox_prep_kernel,
        grid=(n_batch, nblk),
        in_specs=[pl.BlockSpec((tc, LANES), lambda b, j: (b * nblk + j, 0)),
                  pl.BlockSpec((1, LANES), lambda b, j: (0, 0)),
                  pl.BlockSpec((tc, tc), lambda b, j: (0, 0)),
                  pl.BlockSpec(sel.shape, lambda b, j: (0, 0))],
        out_specs=pl.BlockSpec((tc, width), lambda b, j: (b * nblk + j, 0)),
        out_shape=jax.ShapeDtypeStruct((n_tok, width), BF16),
        scratch_shapes=[pltpu.VMEM((HEAD_ROWS, 1), F32)],
        compiler_params=_params("arbitrary", "arbitrary"),
        name="fox_prep",
    )(fl, b_row, tri, sel)


def _attn_kernel(*refs, bq, bk, fox, lambda_init):
    for qi in range(refs[0].shape[0] // bq):
        _attn_q_block(qi, refs, bq, bk, fox, lambda_init)


def _attn_q_block(qi, refs, bq, bk, fox, lambda_init):
    if fox:
        q_ref, k_ref, vt_ref, bl_ref, out_ref, qs_ref, st_ref, p_ref, acc_ref, stat_ref, ka_ref, kb_ref = refs
    else:
        q_ref, k_ref, vt_ref, lam_ref, gsub_ref, out_ref, qs_ref, st_ref, p_ref, acc_ref, stat_ref = refs
    q = q_ref[qi * bq:(qi + 1) * bq, :].astype(F32)
    lane = lax.broadcasted_iota(jnp.int32, q.shape, 1)
    qs_ref = qs_ref.at[qi % 2]
    if fox:
        if qi == 0:
            k = k_ref[...].astype(F32)
            bl = bl_ref[...].astype(F32)
            klane = lax.broadcasted_iota(jnp.int32, k.shape, 1)
            ka_ref[...] = jnp.where(klane < HEAD_DIM, k, bl).astype(BF16)
            kb_ref[...] = jnp.where(klane >= HEAD_DIM, k, bl).astype(BF16)

        q_streams = (jnp.where(lane < HEAD_DIM, q, jnp.where(lane < HEAD_DIM + 3, 1.0, 0.0)),
                     jnp.where(lane >= HEAD_DIM, q, jnp.where(lane < 3, 1.0, 0.0)))
        k_srcs = (ka_ref, kb_ref)
        v_rows = ((0, HEAD_DIM), (HEAD_DIM, 2 * HEAD_DIM))
    else:
        q_streams = (jnp.where(lane < HEAD_DIM, q, 0.0), jnp.where(lane >= HEAD_DIM, q, 0.0))
        k_srcs = (k_ref, k_ref)
        v_rows = ((0, 2 * HEAD_DIM), (0, 2 * HEAD_DIM))
    n_v = v_rows[0][1] - v_rows[0][0]
    ring = bq // bk
    assert ring * bk == bq and ring % 2 == 0 and ring >= 4
    for idx in range(2):
        qs_ref[idx] = q_streams[idx].astype(BF16)
        p_ref[idx, 1] = jnp.zeros((bk, bq), BF16)
        acc_ref[idx] = jnp.zeros((n_v + 16, bq), F32)
    ones_rows = jnp.ones((16, bk), BF16)
    first_diag = ring * qi

    def key_start(kj):
        return kj * bk if isinstance(kj, int) else pl.multiple_of(kj * bk, bk)

    def qk_stage(kj, idx, slot, q_lo=0):
        k0 = key_start(kj)
        st_ref[idx, slot, :, q_lo:] = _dot_nt(k_srcs[idx][pl.ds(k0, bk), :], qs_ref[idx, q_lo:, :])

    M_ROW, A_ROW = 0, 8

    def pv_stage(kj, idx, pslot, q_lo=0):
        k0 = key_start(kj)
        r0, r1 = v_rows[idx]
        v_aug = jnp.concatenate([vt_ref[r0:r1, pl.ds(k0, bk)], ones_rows], axis=0)
        alpha = stat_ref[idx, A_ROW + 8 * pslot:A_ROW + 8 * pslot + 1, q_lo:]
        acc_ref[idx, :, q_lo:] = (alpha * acc_ref[idx, :, q_lo:]
                                  + _dot(v_aug, p_ref[idx, pslot, :, q_lo:]))

    def softmax_stage(idx, slot, pslot, masked, q_lo=0):
        st = st_ref[idx, slot, :, q_lo:]
        if masked:
            key_idx = lax.broadcasted_iota(jnp.int32, st.shape, 0)
            qry_idx = lax.broadcasted_iota(jnp.int32, st.shape, 1)
            st = jnp.where(key_idx <= qry_idx, st, NEG_BIG)
        m = stat_ref[idx, M_ROW:M_ROW + 1, q_lo:]
        m_new = jnp.maximum(m, jnp.max(st, axis=0, keepdims=True))
        p_ref[idx, pslot, :, q_lo:] = jnp.exp2(st - m_new).astype(BF16)
        stat_ref[idx, M_ROW:M_ROW + 1, q_lo:] = m_new
        stat_ref[idx, A_ROW + 8 * pslot:A_ROW + 8 * pslot + 1, q_lo:] = jnp.exp2(m - m_new)

    for idx in range(2):
        stat_ref[idx, M_ROW:M_ROW + 8, :] = jnp.full((8, bq), NEG_BIG, F32)
        stat_ref[idx, A_ROW:A_ROW + 16, :] = jnp.ones((16, bq), F32)
        qk_stage(0, idx, 0)
        qk_stage(1, idx, 1)

    def block_group(j, diag_group):
        for s in range(ring):
            lo_prev = (s - 1) * bk if diag_group and s > 0 else 0
            lo_this = s * bk if diag_group else 0
            for idx in range(2):
                if not diag_group:
                    qk_stage(j + s + 2, idx, (s + 2) % ring)
                elif s + 2 < ring:
                    qk_stage(j + s + 2, idx, s + 2, (s + 2) * bk)
                prev = j + s - 1
                prev = max(prev, 0) if isinstance(prev, int) else jnp.maximum(prev, 0)
                pv_stage(prev, idx, (s - 1) % 2, lo_prev)
                softmax_stage(idx, s, s % 2, diag_group, lo_this)

    def body(i, carry):
        block_group(ring * i, False)
        return carry

    if qi > 0:
        lax.fori_loop(0, qi, body, 0)
    block_group(first_diag, True)
    outs = []
    for idx in range(2):
        pv_stage(first_diag + ring - 1, idx, (ring - 1) % 2, (ring - 1) * bk)
        acc = acc_ref[idx]
        outs.append(acc[0:n_v] * (1.0 / acc[n_v:n_v + 1]))
    o_a, o_b = outs
    if fox:
        y = jnp.concatenate([o_a, o_b], axis=0)
    else:
        lam_vec = lam_ref[...]
        lam = (jnp.exp(jnp.sum(lam_vec[0:1] * lam_vec[1:2], axis=1, keepdims=True))
               - jnp.exp(jnp.sum(lam_vec[2:3] * lam_vec[3:4], axis=1, keepdims=True))
               + lambda_init)
        y = o_a - lam * o_b
        y = y * lax.rsqrt(jnp.mean(y * y, axis=0, keepdims=True) + SUBLN_EPS)
        y = y * (gsub_ref[...] * (1.0 - lambda_init))
    out_ref[qi * bq:(qi + 1) * bq, :] = y.T.astype(out_ref.dtype)


def _attention(q, k, vt, extra, n_batch, seq, fox, lambda_init=0.0):
    n_tok, width = q.shape
    n_pairs = width // LANES
    bq, bk = ATTN_Q_BLOCK, ATTN_K_BLOCK
    n_v = HEAD_DIM if fox else 2 * HEAD_DIM
    k_spec = pl.BlockSpec((seq, LANES), lambda b, hp: (b, hp))
    q_spec = k_spec
    vt_spec = pl.BlockSpec((LANES, seq), lambda b, hp: (hp, b))
    scratch = [pltpu.VMEM((2, 2, bq, LANES), BF16),
               pltpu.VMEM((2, bq // bk, bk, bq), F32),
               pltpu.VMEM((2, 2, bk, bq), BF16),
               pltpu.VMEM((2, n_v + 16, bq), F32),
               pltpu.VMEM((2, 24, bq), F32)]
    if fox:
        extra_specs = [k_spec]
        scratch += [pltpu.VMEM((seq, LANES), BF16)] * 2
    else:
        lam_vec, g_sub = extra
        extra_specs = [pl.BlockSpec(lam_vec.shape, lambda b, hp: (0, 0)),
                       pl.BlockSpec(g_sub.shape, lambda b, hp: (0, 0))]
    return pl.pallas_call(
        functools.partial(_attn_kernel, bq=bq, bk=bk, fox=fox, lambda_init=lambda_init),
        grid=(n_batch, n_pairs),
        in_specs=[q_spec, k_spec, vt_spec] + extra_specs,
        out_specs=q_spec,
        out_shape=jax.ShapeDtypeStruct((n_tok, width), BF16),
        scratch_shapes=scratch,
        compiler_params=_params("arbitrary", "arbitrary"),
        name="fox_attn" if fox else "diff_attn",
    )(q, k, vt, *extra)


ROUTE_ROWS = 48


def _route_block(logits_t, tri, cnt_ref):
    rows, n = logits_t.shape
    row_f = lax.broadcasted_iota(jnp.int32, (rows, n), 0).astype(F32)

    def first_argmax(t):
        top = jnp.max(t, axis=0, keepdims=True)
        return top, jnp.min(jnp.where(t == top, row_f, float(rows)), axis=0, keepdims=True)

    is_group = row_f < N_GROUPS
    g_max, g_sel = first_argmax(jnp.where(is_group, logits_t, NEG_BIG))
    g_w = 1.0 / jnp.sum(jnp.where(is_group, jnp.exp(logits_t - g_max), 0.0), axis=0, keepdims=True)
    lo = N_GROUPS + EXPERTS_PER_GROUP * g_sel
    scores = jnp.where((row_f >= lo) & (row_f < lo + EXPERTS_PER_GROUP), logits_t, NEG_BIG)
    v1, i1 = first_argmax(scores)
    v2, i2 = first_argmax(jnp.where(row_f == i1, NEG_BIG, scores))
    t = jnp.exp(v2 - v1)
    w1 = g_w / (1.0 + t)
    w2 = w1 * t

    hit1 = row_f == i1
    hit2 = row_f == i2
    onehot = jnp.where(hit1 | hit2, 1.0, 0.0)
    before = _dot(onehot.astype(BF16), tri) + cnt_ref[...]
    r1 = jnp.sum(jnp.where(hit1, before, 0.0), axis=0, keepdims=True)
    r2 = jnp.sum(jnp.where(hit2, before, 0.0), axis=0, keepdims=True)
    cnt_ref[...] = before[:, n - 1:n] + onehot[:, n - 1:n]
    out_row = lax.broadcasted_iota(jnp.int32, (8, n), 0)
    out = jnp.zeros((8, n), F32)
    for pos, val in enumerate((i1 - N_GROUPS, i2 - N_GROUPS, r1, r2, w1, w2)):
        out = jnp.where(out_row == pos, val, out)
    return out


def _post_mixer_kernel(ya_ref, yb_ref, ga_ref, gb_ref, x_ref, wa_ref, wb_ref, wo_ref,
                       gc_ref, wcq_ref, kc_ref, vc_ref, wco_ref, gf_ref, wr_ref, br_ref, tri_ref,
                       x2_ref, hm_ref, route_ref, route_t_ref, cnt_out_ref, cnt_ref):
    @pl.when((pl.program_id(0) == 0) & (pl.program_id(1) == 0))
    def _():
        cnt_ref[...] = jnp.zeros_like(cnt_ref)

    merged = (_sigmoid(ga_ref[...].astype(F32)) * _dot(ya_ref[...], wa_ref[...])
              + _sigmoid(gb_ref[...].astype(F32)) * _dot(yb_ref[...], wb_ref[...]))
    x1 = x_ref[...] + _dot(merged.astype(BF16), wo_ref[...])

    hx = _rms(x1, gc_ref[...], NORM_EPS).astype(BF16)
    qc = (_dot(hx, wcq_ref[...]) * QK_SCALE).astype(BF16)
    kc = kc_ref[...]
    vc = vc_ref[...]
    lane = lax.broadcasted_iota(jnp.int32, qc.shape, 1)
    zero = jnp.zeros_like(qc)
    o = jnp.zeros(qc.shape, F32)
    for h in range(CROSS_HEADS):
        in_head = (lane >= h * HEAD_DIM) & (lane < (h + 1) * HEAD_DIM)
        s = _dot_nt(jnp.where(in_head, qc, zero), kc)
        p = jnp.exp(s - jnp.max(s, axis=1, keepdims=True))
        p = p * (1.0 / jnp.sum(p, axis=1, keepdims=True))
        o = jnp.where(in_head, _dot(p.astype(BF16), vc), o)
    x2 = x1 + _dot(o.astype(BF16), wco_ref[...])
    x2_ref[...] = x2

    hm = _rms(x2, gf_ref[...], NORM_EPS)
    hm_ref[...] = hm.astype(BF16)
    hm_hi = hm.astype(BF16)
    hm_lo = (hm - hm_hi.astype(F32)).astype(BF16)
    both = _dot_nt(wr_ref[...], hm_hi)
    logits_t = (both[0:ROUTE_ROWS] + both[ROUTE_ROWS:2 * ROUTE_ROWS]
                + _dot_nt(wr_ref[0:ROUTE_ROWS, :], hm_lo) + br_ref[...])
    route_t = _route_block(logits_t, tri_ref[...], cnt_ref)
    route_t_ref[...] = route_t
    pad_rows = jnp.zeros((LANES - route_t.shape[0], route_t.shape[1]), F32)
    route_ref[...] = jnp.concatenate([route_t, pad_rows], axis=0).T
    cnt_out_ref[...] = jnp.broadcast_to(cnt_ref[...], cnt_out_ref.shape)


def _post_mixer(ya, yb, ga, gb, x2d, wa, wb, wo, g_cross, wcq, kc, vc, wco, g_ffn, wr, br,
                n_batch, seq, mem_len, tm):
    n_tok, d = x2d.shape
    nblk = seq // tm
    tok = lambda b, j: (b * nblk + j, 0)
    const = lambda b, j: (0, 0)
    full = lambda a: pl.BlockSpec(a.shape, const)
    tri = jnp.triu(jnp.ones((tm, tm), BF16), 1)
    return pl.pallas_call(
        _post_mixer_kernel,
        grid=(n_batch, nblk),
        in_specs=[pl.BlockSpec((tm, ya.shape[1]), tok), pl.BlockSpec((tm, yb.shape[1]), tok),
                  pl.BlockSpec((tm, d), tok), pl.BlockSpec((tm, d), tok), pl.BlockSpec((tm, d), tok),
                  full(wa), full(wb), full(wo), full(g_cross), full(wcq),
                  pl.BlockSpec((mem_len, kc.shape[1]), lambda b, j: (b, 0)),
                  pl.BlockSpec((mem_len, vc.shape[1]), lambda b, j: (b, 0)),
                  full(wco), full(g_ffn), full(wr), full(br),
                  full(tri)],
        out_specs=[pl.BlockSpec((tm, d), tok), pl.BlockSpec((tm, d), tok),
                   pl.BlockSpec((tm, LANES), tok),
                   pl.BlockSpec((8, tm), lambda b, j: (0, b * nblk + j)),
                   pl.BlockSpec((ROUTE_ROWS, LANES), const)],
        out_shape=[jax.ShapeDtypeStruct((n_tok, d), F32),
                   jax.ShapeDtypeStruct((n_tok, d), BF16),
                   jax.ShapeDtypeStruct((n_tok, LANES), F32),
                   jax.ShapeDtypeStruct((8, n_tok), F32),
                   jax.ShapeDtypeStruct((ROUTE_ROWS, LANES), F32)],
        scratch_shapes=[pltpu.VMEM((ROUTE_ROWS, 1), F32)],
        compiler_params=_params("arbitrary", "arbitrary"),
        name="post_mixer",
    )(ya, yb, ga, gb, x2d, wa, wb, wo, g_cross, wcq, kc, vc, wco, g_ffn, wr, br, tri)


def _expert_kernel(be_ref, nused_ref, xs_ref, wg_ref, wu_ref, wd_ref, ys_ref, wg_s, wu_s, wd_s):
    i = pl.program_id(0)

    @pl.when((i == 0) | (be_ref[i] != be_ref[jnp.maximum(i - 1, 0)]))
    def _():
        wg_s[...] = wg_ref[0].astype(BF16)
        wu_s[...] = wu_ref[0].astype(BF16)
        wd_s[...] = wd_ref[0].astype(BF16)

    @pl.when(i < nused_ref[0])
    def _():
        xb = xs_ref[...]
        g = _dot(xb, wg_s[...])
        u = _dot(xb, wu_s[...])
        hb = (g * _sigmoid(g) * u).astype(BF16)
        ys_ref[...] = _dot(hb, wd_s[...]).astype(ys_ref.dtype)

    @pl.when(i >= nused_ref[0])
    def _():
        ys_ref[...] = jnp.zeros_like(ys_ref)


def _experts(block_expert, n_used, xs, wg, wu, wd):
    n_rows, d = xs.shape
    n_blocks = n_rows // MOE_BLOCK
    ff = wg.shape[2]
    grid_spec = pltpu.PrefetchScalarGridSpec(
        num_scalar_prefetch=2,
        grid=(n_blocks,),
        in_specs=[pl.BlockSpec((MOE_BLOCK, d), lambda i, be, nu: (i, 0)),
                  pl.BlockSpec((1, d, ff), lambda i, be, nu: (be[i], 0, 0)),
                  pl.BlockSpec((1, d, ff), lambda i, be, nu: (be[i], 0, 0)),
                  pl.BlockSpec((1, ff, d), lambda i, be, nu: (be[i], 0, 0))],
        out_specs=pl.BlockSpec((MOE_BLOCK, d), lambda i, be, nu: (i, 0)),
        scratch_shapes=[pltpu.VMEM((d, ff), BF16), pltpu.VMEM((d, ff), BF16),
                        pltpu.VMEM((ff, d), BF16)],
    )
    return pl.pallas_call(
        _expert_kernel,
        grid_spec=grid_spec,
        out_shape=jax.ShapeDtypeStruct((n_rows, d), BF16),
        compiler_params=_params("arbitrary"),
        name="experts",
    )(block_expert, n_used, xs, wg, wu, wd)


def _final_kernel(x_ref, y0_ref, y1_ref, route_ref, g_ref, out_ref, *, normalize):
    route = route_ref[...]
    w0 = route[:, 2 * TOP_K:2 * TOP_K + 1]
    w1 = route[:, 2 * TOP_K + 1:2 * TOP_K + 2]
    out = x_ref[...] + w0 * y0_ref[...].astype(F32) + w1 * y1_ref[...].astype(F32)
    out_ref[...] = _rms(out, g_ref[...], NORM_EPS) if normalize else out


def _final(x2, y0, y1, route, g_final, tm, normalize):
    n_tok, d = x2.shape
    tok = lambda i: (i, 0)
    return pl.pallas_call(
        functools.partial(_final_kernel, normalize=normalize),
        grid=(n_tok // tm,),
        in_specs=[pl.BlockSpec((tm, d), tok), pl.BlockSpec((tm, d), tok), pl.BlockSpec((tm, d), tok),
                  pl.BlockSpec((tm, LANES), tok), pl.BlockSpec((1, d), lambda i: (0, 0))],
        out_specs=pl.BlockSpec((tm, d), tok),
        out_shape=jax.ShapeDtypeStruct((n_tok, d), F32),
        compiler_params=_params("arbitrary"),
        name="final_norm",
    )(x2, y0, y1, route, g_final)


def _slot_kernel(tab_ref, rt_ref, dest_ref, key_ref):
    e = rt_ref[0:TOP_K, :]
    rank = rt_ref[TOP_K:2 * TOP_K, :].astype(jnp.int32)
    pad = jnp.zeros(e.shape, jnp.int32)
    seg = jnp.zeros(e.shape, jnp.int32)
    for k in range(N_EXPERTS):
        hit = e == float(k)
        pad = jnp.where(hit, tab_ref[k], pad)
        seg = jnp.where(hit, tab_ref[N_EXPERTS + k], seg)
    dest_ref[...] = pad + rank
    key_ref[...] = seg + rank


def _slots(tables, route_t, tm):
    n_tok = route_t.shape[1]
    out = jax.ShapeDtypeStruct((TOP_K, n_tok), jnp.int32)
    grid_spec = pltpu.PrefetchScalarGridSpec(
        num_scalar_prefetch=1,
        grid=(n_tok // tm,),
        in_specs=[pl.BlockSpec((route_t.shape[0], tm), lambda i, tab: (0, i))],
        out_specs=[pl.BlockSpec((TOP_K, tm), lambda i, tab: (0, i))] * 2,
    )
    return pl.pallas_call(_slot_kernel, grid_spec=grid_spec, out_shape=[out, out],
                          compiler_params=_params("arbitrary"), name="moe_slots")(tables, route_t)


def _route(route_t, counts, n_tok):
    n_assign = n_tok * TOP_K
    counts = counts[N_GROUPS:N_GROUPS + N_EXPERTS, 0].astype(jnp.int32)
    seg_start = jnp.cumsum(counts) - counts
    padded = (counts + MOE_BLOCK - 1) // MOE_BLOCK * MOE_BLOCK
    pad_end = jnp.cumsum(padded)
    pad_start = pad_end - padded
    dest, key = _slots(jnp.concatenate([pad_start, seg_start]), route_t, tm=min(8192, n_tok))
    order = jnp.argsort(key.reshape(n_assign)).astype(jnp.int32)

    n_blocks = -(-n_assign // MOE_BLOCK) + N_EXPERTS
    n_rows = n_blocks * MOE_BLOCK
    block_start = jnp.arange(n_blocks, dtype=jnp.int32) * MOE_BLOCK
    block_expert = jnp.minimum(jnp.sum(block_start[:, None] >= pad_end[None, :], axis=1),
                               N_EXPERTS - 1).astype(jnp.int32)
    onehot = block_expert[:, None] == jnp.arange(N_EXPERTS, dtype=jnp.int32)[None, :]

    def of_block(table):
        return jnp.sum(jnp.where(onehot, table[None, :], 0), axis=1)

    offset = block_start - of_block(pad_start)
    in_block = jnp.arange(MOE_BLOCK, dtype=jnp.int32)[None, :]
    valid = in_block < (of_block(counts) - offset)[:, None]
    src = jnp.clip((of_block(seg_start) + offset)[:, None] + in_block, 0, n_assign - 1)
    rows = block_start[:, None] + in_block
    row_tok = jnp.where(valid, order[src] % n_tok, rows % n_tok).reshape(n_rows)
    n_used = (pad_end[-1] // MOE_BLOCK).astype(jnp.int32).reshape(1)
    return row_tok, dest, block_expert, n_used


def kernel(x, mem, positions, g_mix, w_in, b_fgate, w_branch_a, w_branch_b, w_out, lambda_q1, lambda_k1, lambda_q2, lambda_k2, g_diff_sub, g_cross, g_mem, w_cq, w_ckv, w_co, g_ffn, w_group, b_group, w_expert, b_expert, w_exp_gate, w_exp_up, w_exp_down, g_final):
    n_batch, seq, d = x.shape
    mem_len = mem.shape[1]
    depth = g_mix.shape[0]
    n_tok = n_batch * seq
    fox_w = FOX_HEADS * HEAD_DIM
    diff_w = DIFF_HEADS * 2 * HEAD_DIM

    half = HEAD_DIM // 2
    inv_freq = 10000.0 ** (-jnp.arange(half, dtype=F32) * 2.0 / HEAD_DIM)
    ang = positions.astype(F32).reshape(n_tok, 1) * inv_freq[None, :]
    rot_t = jnp.concatenate([jnp.cos(ang), jnp.sin(ang), jnp.zeros((n_tok, LANES - 2 * half), F32)],
                            axis=1)
    sel = _bias_lane_selectors()

    x2d = x.reshape(n_tok, d)
    mem2d = mem.reshape(n_batch * mem_len, d)
    for l in range(depth):
        lambda_init = 0.8 - 0.6 * math.exp(-0.3 * l)
        o_fv = 2 * fox_w
        o_fl = o_fv + fox_w
        o_dq = o_fl + FOX_HEADS
        o_dv = o_dq + 2 * diff_w
        o_ga = o_dv + diff_w
        wl = w_in[l]
        w_main = jnp.concatenate([wl[:, :o_fv], wl[:, o_dq:o_dv], wl[:, o_ga:]], axis=1).astype(BF16)
        w_vt = jnp.concatenate([wl[:, o_fv:o_fl], wl[:, o_dv:o_ga]], axis=1).T.astype(BF16)
        w_f = jnp.pad(wl[:, o_fl:o_dq], ((0, 0), (0, LANES - FOX_HEADS))).astype(BF16)
        b_row = jnp.pad(b_fgate[l], (0, LANES - FOX_HEADS))[None]

        kc, vc = _mem_kv(mem2d, g_mem[l][None], w_ckv[l].astype(BF16), n_batch, mem_len)
        fq, fk, dq, dk, ga, gb, fvt, dvt, fl = _in_proj(
            x2d, g_mix[l][None], w_main, w_vt, w_f, rot_t, tm=512)
        bias_lanes = _fox_prep(fl, b_row, sel, n_batch, seq, tc=512)
        y_a = _attention(fq, fk, fvt, (bias_lanes,), n_batch, seq, fox=True)
        lam_vec = jnp.stack([lambda_q1[l], lambda_k1[l], lambda_q2[l], lambda_k2[l]])
        y_b = _attention(dq, dk, dvt, (lam_vec, g_diff_sub[l][:, None]), n_batch, seq,
                         fox=False, lambda_init=lambda_init)

        w_router = jnp.pad(jnp.concatenate([w_group[l], w_expert[l]], axis=1).T,
                           ((0, ROUTE_ROWS - N_GROUPS - N_EXPERTS), (0, 0)))
        w_router_hi = w_router.astype(BF16)
        w_router = jnp.concatenate(
            [w_router_hi, (w_router - w_router_hi.astype(F32)).astype(BF16)], axis=0)
        b_router = jnp.pad(jnp.concatenate([b_group[l], b_expert[l]]),
                           (0, ROUTE_ROWS - N_GROUPS - N_EXPERTS))[:, None]
        x2d, hm, route, route_t, counts = _post_mixer(
            y_a, y_b, ga, gb, x2d, w_branch_a[l].astype(BF16), w_branch_b[l].astype(BF16),
            w_out[l].astype(BF16), g_cross[l][None], w_cq[l].astype(BF16), kc, vc,
            w_co[l].astype(BF16), g_ffn[l][None], w_router, b_router,
            n_batch, seq, mem_len, tm=512)

        row_tok, dest, block_expert, n_used = _route(route_t, counts, n_tok)
        xs = hm[row_tok]
        ys = _experts(block_expert, n_used, xs, w_exp_gate[l], w_exp_up[l], w_exp_down[l])
        last = l + 1 == depth
        x2d = _final(x2d, ys[dest[0]], ys[dest[1]], route, g_final[None], tm=512, normalize=last)
    return x2d.reshape(n_batch, seq, d)
```

```python
import functools
import math

import numpy as np
import jax
import jax.numpy as jnp
from jax import lax
from jax.experimental import pallas as pl
from jax.experimental.pallas import tpu as pltpu

HEAD_DIM = 64
LANES = 128
FOX_HEADS = 8
DIFF_HEADS = 4
CROSS_HEADS = 4
N_GROUPS = 4
EXPERTS_PER_GROUP = 8
N_EXPERTS = N_GROUPS * EXPERTS_PER_GROUP
TOP_K = 2
MOE_BLOCK = 512
TOKEN_BLOCK = 512
COMBINE_BLOCK = 1024
ATTN_Q_BLOCK = 1024
ATTN_K_BLOCK = 256
NORM_EPS = 1e-6
SUBLN_EPS = 1e-5
QK_SCALE = HEAD_DIM ** -0.5
LOG2E = math.log2(math.e)
Q_PRESCALE = QK_SCALE * LOG2E
NEG_BIG = -1e30
VMEM_LIMIT = 56 * 2**20

BF16 = jnp.bfloat16
F32 = jnp.float32


def _rms(t, g, eps):
    return t * lax.rsqrt(jnp.mean(t * t, axis=-1, keepdims=True) + eps) * g


def _sigmoid(t):
    return 0.5 + 0.5 * jnp.tanh(0.5 * t)


def _dot(a, b):
    return jnp.dot(a, b, preferred_element_type=F32)


def _dot_nt(a, b):
    return lax.dot_general(a, b, (((1,), (1,)), ((), ())), preferred_element_type=F32)


def _params(*sem):
    return pltpu.CompilerParams(dimension_semantics=sem, vmem_limit_bytes=VMEM_LIMIT)


def _mem_kv_kernel(mem_ref, g_ref, w_ref, k_ref, v_ref):
    h = _rms(mem_ref[...], g_ref[...], NORM_EPS).astype(BF16)
    kv = _dot(h, w_ref[...])
    width = k_ref.shape[1]
    k_ref[...] = kv[:, :width].astype(BF16)
    v_ref[...] = kv[:, width:].astype(BF16)


def _mem_kv(mem2, g_mem, w_ckv, n_batch, mem_len):
    d = mem2.shape[1]
    cw = w_ckv.shape[1] // 2
    out = jax.ShapeDtypeStruct((n_batch * mem_len, cw), BF16)
    return pl.pallas_call(
        _mem_kv_kernel,
        grid=(n_batch,),
        in_specs=[pl.BlockSpec((mem_len, d), lambda b: (b, 0)),
                  pl.BlockSpec((1, d), lambda b: (0, 0)),
                  pl.BlockSpec((d, 2 * cw), lambda b: (0, 0))],
        out_specs=[pl.BlockSpec((mem_len, cw), lambda b: (b, 0))] * 2,
        out_shape=[out, out],
        compiler_params=_params("arbitrary"),
        name="mem_kv",
    )(mem2, g_mem, w_ckv)


def _rope(t, cos, sin_signed, first_half):
    fwd = pltpu.roll(t, LANES - HEAD_DIM // 2, axis=1)
    bwd = pltpu.roll(t, HEAD_DIM // 2, axis=1)
    return t * cos + jnp.where(first_half, fwd, bwd) * sin_signed


def _in_proj_kernel(x_ref, g_ref, w_ref, wvt_ref, wf_ref, rot_ref,
                    fq_ref, fk_ref, dq_ref, dk_ref, ga_ref, gb_ref, fvt_ref, dvt_ref, fl_ref):
    h = _rms(x_ref[...], g_ref[...], NORM_EPS).astype(BF16)
    tm = h.shape[0]
    cw = fq_ref.shape[1]

    def proj(chunk):
        return _dot(h, w_ref[:, chunk * cw:(chunk + 1) * cw])

    fq_ref[...] = (proj(0) * Q_PRESCALE).astype(BF16)
    fk_ref[...] = proj(1).astype(BF16)

    lane = lax.broadcasted_iota(jnp.int32, (tm, LANES), 1)
    rot = rot_ref[...]
    quarter = HEAD_DIM // 2
    c = jnp.where(lane < quarter, rot, 0.0)
    s = jnp.where((lane >= quarter) & (lane < 2 * quarter), rot, 0.0)
    cos = c + pltpu.roll(c, quarter, axis=1) + pltpu.roll(c, 2 * quarter, axis=1) + pltpu.roll(c, 3 * quarter, axis=1)
    sin = (s + pltpu.roll(s, 2 * quarter, axis=1)) - (pltpu.roll(s, quarter, axis=1) + pltpu.roll(s, 3 * quarter, axis=1))
    first_half = (lane % HEAD_DIM) < HEAD_DIM // 2
    for out_ref, chunk, scale in ((dq_ref, 2, Q_PRESCALE), (dk_ref, 3, 1.0)):
        t = proj(chunk)
        for c in range(cw // LANES):
            blk = _rope(t[:, c * LANES:(c + 1) * LANES], cos, sin, first_half)
            out_ref[:, c * LANES:(c + 1) * LANES] = (blk * scale).astype(BF16)

    for out_ref, chunk in ((ga_ref, 4), (gb_ref, 6)):
        for c in range(2):
            out_ref[:, c * cw:(c + 1) * cw] = proj(chunk + c).astype(BF16)
    fvt_ref[...] = _dot_nt(wvt_ref[0:cw, :], h).astype(BF16)
    dvt_ref[...] = _dot_nt(wvt_ref[cw:2 * cw, :], h).astype(BF16)
    fl_ref[...] = _dot(h, wf_ref[...])


def _in_proj(x2, g_mix, w_main, w_vt, w_f, rot_t, tm):
    n_tok, d = x2.shape
    cw = 512
    tok = lambda i: (i, 0)
    tok_t = lambda i: (0, i)
    const = lambda i: (0, 0)
    o512 = jax.ShapeDtypeStruct((n_tok, cw), BF16)
    o1024 = jax.ShapeDtypeStruct((n_tok, 2 * cw), BF16)
    o512_t = jax.ShapeDtypeStruct((cw, n_tok), BF16)
    return pl.pallas_call(
        _in_proj_kernel,
        grid=(n_tok // tm,),
        in_specs=[pl.BlockSpec((tm, d), tok),
                  pl.BlockSpec((1, d), const),
                  pl.BlockSpec(w_main.shape, const, pipeline_mode=pl.Buffered(1)),
                  pl.BlockSpec(w_vt.shape, const, pipeline_mode=pl.Buffered(1)),
                  pl.BlockSpec(w_f.shape, const),
                  pl.BlockSpec((tm, LANES), tok)],
        out_specs=[pl.BlockSpec((tm, cw), tok)] * 4
                  + [pl.BlockSpec((tm, 2 * cw), tok)] * 2
                  + [pl.BlockSpec((cw, tm), tok_t)] * 2
                  + [pl.BlockSpec((tm, LANES), tok)],
        out_shape=[o512] * 4 + [o1024] * 2 + [o512_t] * 2
                  + [jax.ShapeDtypeStruct((n_tok, LANES), F32)],
        compiler_params=_params("arbitrary"),
        name="in_proj",
    )(x2, g_mix, w_main, w_vt, w_f, rot_t)


HEAD_ROWS = 16


def _bias_lane_selectors():
    sel = np.zeros((LANES, FOX_HEADS // 2 * LANES), np.float32)
    for h in range(FOX_HEADS):
        base = (h // 2) * LANES + (HEAD_DIM if h % 2 == 0 else 0)
        for piece in range(3):
            sel[HEAD_ROWS * piece + h, base + piece] = 1.0
    return jnp.asarray(sel, BF16)


def _split3(t):
    pieces = []
    for _ in range(3):
        part = t.astype(BF16)
        t = t - part.astype(F32)
        pieces.append(part)
    return pieces


def _fox_prep_kernel(fl_ref, b_ref, tri_ref, sel_ref, out_ref, carry_ref):
    @pl.when(pl.program_id(1) == 0)
    def _():
        carry_ref[...] = jnp.zeros_like(carry_ref)

    z = fl_ref[...] + b_ref[...]
    log_f = jnp.minimum(z, 0.0) - jnp.log(1.0 + jnp.exp(-jnp.abs(z)))
    tc = z.shape[0]
    log_f_t = log_f.T[0:HEAD_ROWS, :]
    c = carry_ref[...]
    for part in _split3(log_f_t):
        c = c + _dot(part, tri_ref[...])
    carry_ref[...] = c[:, tc - 1:tc]
    pieces = [p.astype(F32) for p in _split3(c * (-LOG2E))]
    stacked = jnp.concatenate(pieces + [jnp.zeros((LANES - 3 * HEAD_ROWS, tc), F32)], axis=0)
    out_ref[...] = _dot(stacked.T.astype(BF16), sel_ref[...]).astype(BF16)


def _fox_prep(fl, b_row, sel, n_batch, seq, tc):
    n_tok = fl.shape[0]
    nblk = seq // tc
    width = sel.shape[1]
    tri = jnp.triu(jnp.ones((tc, tc), BF16))
    return pl.pallas_call(
        _fox_prep_kernel,
        grid=(n_batch, nblk),
        in_specs=[pl.BlockSpec((tc, LANES), lambda b, j: (b * nblk + j, 0)),
                  pl.BlockSpec((1, LANES), lambda b, j: (0, 0)),
                  pl.BlockSpec((tc, tc), lambda b, j: (0, 0)),
                  pl.BlockSpec(sel.shape, lambda b, j: (0, 0))],
        out_specs=pl.BlockSpec((tc, width), lambda b, j: (b * nblk + j, 0)),
        out_shape=jax.ShapeDtypeStruct((n_tok, width), BF16),
        scratch_shapes=[pltpu.VMEM((HEAD_ROWS, 1), F32)],
        compiler_params=_params("arbitrary", "arbitrary"),
        name="fox_prep",
    )(fl, b_row, tri, sel)


def _attn_kernel(*refs, bq, bk, fox, lambda_init):
    for qi in range(refs[0].shape[0] // bq):
        _attn_q_block(qi, refs, bq, bk, fox, lambda_init)


def _attn_q_block(qi, refs, bq, bk, fox, lambda_init):
    if fox:
        q_ref, k_ref, vt_ref, bl_ref, out_ref, qs_ref, st_ref, p_ref, acc_ref, stat_ref, ka_ref, kb_ref = refs
    else:
        q_ref, k_ref, vt_ref, lam_ref, gsub_ref, out_ref, qs_ref, st_ref, p_ref, acc_ref, stat_ref = refs
    q = q_ref[qi * bq:(qi + 1) * bq, :].astype(F32)
    lane = lax.broadcasted_iota(jnp.int32, q.shape, 1)
    qs_ref = qs_ref.at[qi % 2]
    if fox:
        if qi == 0:
            k = k_ref[...].astype(F32)
            bl = bl_ref[...].astype(F32)
            klane = lax.broadcasted_iota(jnp.int32, k.shape, 1)
            ka_ref[...] = jnp.where(klane < HEAD_DIM, k, bl).astype(BF16)
            kb_ref[...] = jnp.where(klane >= HEAD_DIM, k, bl).astype(BF16)

        q_streams = (jnp.where(lane < HEAD_DIM, q, jnp.where(lane < HEAD_DIM + 3, 1.0, 0.0)),
                     jnp.where(lane >= HEAD_DIM, q, jnp.where(lane < 3, 1.0, 0.0)))
        k_srcs = (ka_ref, kb_ref)
        v_rows = ((0, HEAD_DIM), (HEAD_DIM, 2 * HEAD_DIM))
    else:
        q_streams = (jnp.where(lane < HEAD_DIM, q, 0.0), jnp.where(lane >= HEAD_DIM, q, 0.0))
        k_srcs = (k_ref, k_ref)
        v_rows = ((0, 2 * HEAD_DIM), (0, 2 * HEAD_DIM))
    n_v = v_rows[0][1] - v_rows[0][0]
    ring = bq // bk
    assert ring * bk == bq and ring % 2 == 0 and ring >= 4
    for idx in range(2):
        qs_ref[idx] = q_streams[idx].astype(BF16)
        p_ref[idx, 1] = jnp.zeros((bk, bq), BF16)
        acc_ref[idx] = jnp.zeros((n_v + 16, bq), F32)
    ones_rows = jnp.ones((16, bk), BF16)
    first_diag = ring * qi

    def key_start(kj):
        return kj * bk if isinstance(kj, int) else pl.multiple_of(kj * bk, bk)

    def qk_stage(kj, idx, slot, q_lo=0):
        k0 = key_start(kj)
        st_ref[idx, slot, :, q_lo:] = _dot_nt(k_srcs[idx][pl.ds(k0, bk), :], qs_ref[idx, q_lo:, :])

    M_ROW, A_ROW = 0, 8

    def pv_stage(kj, idx, pslot, q_lo=0):
        k0 = key_start(kj)
        r0, r1 = v_rows[idx]
        v_aug = jnp.concatenate([vt_ref[r0:r1, pl.ds(k0, bk)], ones_rows], axis=0)
        alpha = stat_ref[idx, A_ROW + 8 * pslot:A_ROW + 8 * pslot + 1, q_lo:]
        acc_ref[idx, :, q_lo:] = (alpha * acc_ref[idx, :, q_lo:]
                                  + _dot(v_aug, p_ref[idx, pslot, :, q_lo:]))

    def softmax_stage(idx, slot, pslot, masked, q_lo=0):
        st = st_ref[idx, slot, :, q_lo:]
        if masked:
            key_idx = lax.broadcasted_iota(jnp.int32, st.shape, 0)
            qry_idx = lax.broadcasted_iota(jnp.int32, st.shape, 1)
            st = jnp.where(key_idx <= qry_idx, st, NEG_BIG)
        m = stat_ref[idx, M_ROW:M_ROW + 1, q_lo:]
        m_new = jnp.maximum(m, jnp.max(st, axis=0, keepdims=True))
        p_ref[idx, pslot, :, q_lo:] = jnp.exp2(st - m_new).astype(BF16)
        stat_ref[idx, M_ROW:M_ROW + 1, q_lo:] = m_new
        stat_ref[idx, A_ROW + 8 * pslot:A_ROW + 8 * pslot + 1, q_lo:] = jnp.exp2(m - m_new)

    for idx in range(2):
        stat_ref[idx, M_ROW:M_ROW + 8, :] = jnp.full((8, bq), NEG_BIG, F32)
        stat_ref[idx, A_ROW:A_ROW + 16, :] = jnp.ones((16, bq), F32)
        qk_stage(0, idx, 0)
        qk_stage(1, idx, 1)

    def block_group(j, diag_group):
        for s in range(ring):
            lo_prev = (s - 1) * bk if diag_group and s > 0 else 0
            lo_this = s * bk if diag_group else 0
            for idx in range(2):
                if not diag_group:
                    qk_stage(j + s + 2, idx, (s + 2) % ring)
                elif s + 2 < ring:
                    qk_stage(j + s + 2, idx, s + 2, (s + 2) * bk)
                prev = j + s - 1
                prev = max(prev, 0) if isinstance(prev, int) else jnp.maximum(prev, 0)
                pv_stage(prev, idx, (s - 1) % 2, lo_prev)
                softmax_stage(idx, s, s % 2, diag_group, lo_this)

    def body(i, carry):
        block_group(ring * i, False)
        return carry

    if qi > 0:
        lax.fori_loop(0, qi, body, 0)
    block_group(first_diag, True)
    outs = []
    for idx in range(2):
        pv_stage(first_diag + ring - 1, idx, (ring - 1) % 2, (ring - 1) * bk)
        acc = acc_ref[idx]
        outs.append(acc[0:n_v] * (1.0 / acc[n_v:n_v + 1]))
    o_a, o_b = outs
    if fox:
        y = jnp.concatenate([o_a, o_b], axis=0)
    else:
        lam_vec = lam_ref[...]
        lam = (jnp.exp(jnp.sum(lam_vec[0:1] * lam_vec[1:2], axis=1, keepdims=True))
               - jnp.exp(jnp.sum(lam_vec[2:3] * lam_vec[3:4], axis=1, keepdims=True))
               + lambda_init)
        y = o_a - lam * o_b
        y = y * lax.rsqrt(jnp.mean(y * y, axis=0, keepdims=True) + SUBLN_EPS)
        y = y * (gsub_ref[...] * (1.0 - lambda_init))
    out_ref[qi * bq:(qi + 1) * bq, :] = y.T.astype(out_ref.dtype)


def _attention(q, k, vt, extra, n_batch, seq, fox, lambda_init=0.0):
    n_tok, width = q.shape
    n_pairs = width // LANES
    bq, bk = ATTN_Q_BLOCK, ATTN_K_BLOCK
    n_v = HEAD_DIM if fox else 2 * HEAD_DIM
    k_spec = pl.BlockSpec((seq, LANES), lambda b, hp: (b, hp))
    q_spec = k_spec
    vt_spec = pl.BlockSpec((LANES, seq), lambda b, hp: (hp, b))
    scratch = [pltpu.VMEM((2, 2, bq, LANES), BF16),
               pltpu.VMEM((2, bq // bk, bk, bq), F32),
               pltpu.VMEM((2, 2, bk, bq), BF16),
               pltpu.VMEM((2, n_v + 16, bq), F32),
               pltpu.VMEM((2, 24, bq), F32)]
    if fox:
        extra_specs = [k_spec]
        scratch += [pltpu.VMEM((seq, LANES), BF16)] * 2
    else:
        lam_vec, g_sub = extra
        extra_specs = [pl.BlockSpec(lam_vec.shape, lambda b, hp: (0, 0)),
                       pl.BlockSpec(g_sub.shape, lambda b, hp: (0, 0))]
    return pl.pallas_call(
        functools.partial(_attn_kernel, bq=bq, bk=bk, fox=fox, lambda_init=lambda_init),
        grid=(n_batch, n_pairs),
        in_specs=[q_spec, k_spec, vt_spec] + extra_specs,
        out_specs=q_spec,
        out_shape=jax.ShapeDtypeStruct((n_tok, width), BF16),
        scratch_shapes=scratch,
        compiler_params=_params("arbitrary", "arbitrary"),
        name="fox_attn" if fox else "diff_attn",
    )(q, k, vt, *extra)


ROUTE_ROWS = 48


def _route_block(logits_t, tri, cnt_ref):
    rows, n = logits_t.shape
    row_f = lax.broadcasted_iota(jnp.int32, (rows, n), 0).astype(F32)

    def first_argmax(t):
        top = jnp.max(t, axis=0, keepdims=True)
        return top, jnp.min(jnp.where(t == top, row_f, float(rows)), axis=0, keepdims=True)

    is_group = row_f < N_GROUPS
    g_max, g_sel = first_argmax(jnp.where(is_group, logits_t, NEG_BIG))
    g_w = 1.0 / jnp.sum(jnp.where(is_group, jnp.exp(logits_t - g_max), 0.0), axis=0, keepdims=True)
    lo = N_GROUPS + EXPERTS_PER_GROUP * g_sel
    scores = jnp.where((row_f >= lo) & (row_f < lo + EXPERTS_PER_GROUP), logits_t, NEG_BIG)
    v1, i1 = first_argmax(scores)
    v2, i2 = first_argmax(jnp.where(row_f == i1, NEG_BIG, scores))
    t = jnp.exp(v2 - v1)
    w1 = g_w / (1.0 + t)
    w2 = w1 * t

    hit1 = row_f == i1
    hit2 = row_f == i2
    onehot = jnp.where(hit1 | hit2, 1.0, 0.0)
    before = _dot(onehot.astype(BF16), tri) + cnt_ref[...]
    r1 = jnp.sum(jnp.where(hit1, before, 0.0), axis=0, keepdims=True)
    r2 = jnp.sum(jnp.where(hit2, before, 0.0), axis=0, keepdims=True)
    cnt_ref[...] = before[:, n - 1:n] + onehot[:, n - 1:n]
    out_row = lax.broadcasted_iota(jnp.int32, (8, n), 0)
    out = jnp.zeros((8, n), F32)
    for pos, val in enumerate((i1 - N_GROUPS, i2 - N_GROUPS, r1, r2, w1, w2)):
        out = jnp.where(out_row == pos, val, out)
    return out


def _post_mixer_kernel(ya_ref, yb_ref, ga_ref, gb_ref, x_ref, wa_ref, wb_ref, wo_ref,
                       gc_ref, wcq_ref, kc_ref, vc_ref, wco_ref, gf_ref, wr_ref, br_ref, tri_ref,
                       x2_ref, hm_ref, route_ref, route_t_ref, cnt_out_ref, cnt_ref):
    @pl.when((pl.program_id(0) == 0) & (pl.program_id(1) == 0))
    def _():
        cnt_ref[...] = jnp.zeros_like(cnt_ref)

    merged = (_sigmoid(ga_ref[...].astype(F32)) * _dot(ya_ref[...], wa_ref[...])
              + _sigmoid(gb_ref[...].astype(F32)) * _dot(yb_ref[...], wb_ref[...]))
    x1 = x_ref[...] + _dot(merged.astype(BF16), wo_ref[...])

    hx = _rms(x1, gc_ref[...], NORM_EPS).astype(BF16)
    qc = (_dot(hx, wcq_ref[...]) * QK_SCALE).astype(BF16)
    kc = kc_ref[...]
    vc = vc_ref[...]
    lane = lax.broadcasted_iota(jnp.int32, qc.shape, 1)
    zero = jnp.zeros_like(qc)
    o = jnp.zeros(qc.shape, F32)
    for h in range(CROSS_HEADS):
        in_head = (lane >= h * HEAD_DIM) & (lane < (h + 1) * HEAD_DIM)
        s = _dot_nt(jnp.where(in_head, qc, zero), kc)
        p = jnp.exp(s - jnp.max(s, axis=1, keepdims=True))
        p = p * (1.0 / jnp.sum(p, axis=1, keepdims=True))
        o = jnp.where(in_head, _dot(p.astype(BF16), vc), o)
    x2 = x1 + _dot(o.astype(BF16), wco_ref[...])
    x2_ref[...] = x2

    hm = _rms(x2, gf_ref[...], NORM_EPS)
    hm_ref[...] = hm.astype(BF16)
    hm_hi = hm.astype(BF16)
    hm_lo = (hm - hm_hi.astype(F32)).astype(BF16)
    both = _dot_nt(wr_ref[...], hm_hi)
    logits_t = (both[0:ROUTE_ROWS] + both[ROUTE_ROWS:2 * ROUTE_ROWS]
                + _dot_nt(wr_ref[0:ROUTE_ROWS, :], hm_lo) + br_ref[...])
    route_t = _route_block(logits_t, tri_ref[...], cnt_ref)
    route_t_ref[...] = route_t
    pad_rows = jnp.zeros((LANES - route_t.shape[0], route_t.shape[1]), F32)
    route_ref[...] = jnp.concatenate([route_t, pad_rows], axis=0).T
    cnt_out_ref[...] = jnp.broadcast_to(cnt_ref[...], cnt_out_ref.shape)


def _post_mixer(ya, yb, ga, gb, x2d, wa, wb, wo, g_cross, wcq, kc, vc, wco, g_ffn, wr, br,
                n_batch, seq, mem_len, tm):
    n_tok, d = x2d.shape
    nblk = seq // tm
    tok = lambda b, j: (b * nblk + j, 0)
    const = lambda b, j: (0, 0)
    full = lambda a: pl.BlockSpec(a.shape, const)
    tri = jnp.triu(jnp.ones((tm, tm), BF16), 1)
    return pl.pallas_call(
        _post_mixer_kernel,
        grid=(n_batch, nblk),
        in_specs=[pl.BlockSpec((tm, ya.shape[1]), tok), pl.BlockSpec((tm, yb.shape[1]), tok),
                  pl.BlockSpec((tm, d), tok), pl.BlockSpec((tm, d), tok), pl.BlockSpec((tm, d), tok),
                  full(wa), full(wb), full(wo), full(g_cross), full(wcq),
                  pl.BlockSpec((mem_len, kc.shape[1]), lambda b, j: (b, 0)),
                  pl.BlockSpec((mem_len, vc.shape[1]), lambda b, j: (b, 0)),
                  full(wco), full(g_ffn), full(wr), full(br),
                  full(tri)],
        out_specs=[pl.BlockSpec((tm, d), tok), pl.BlockSpec((tm, d), tok),
                   pl.BlockSpec((tm, LANES), tok),
                   pl.BlockSpec((8, tm), lambda b, j: (0, b * nblk + j)),
                   pl.BlockSpec((ROUTE_ROWS, LANES), const)],
        out_shape=[jax.ShapeDtypeStruct((n_tok, d), F32),
                   jax.ShapeDtypeStruct((n_tok, d), BF16),
                   jax.ShapeDtypeStruct((n_tok, LANES), F32),
                   jax.ShapeDtypeStruct((8, n_tok), F32),
                   jax.ShapeDtypeStruct((ROUTE_ROWS, LANES), F32)],
        scratch_shapes=[pltpu.VMEM((ROUTE_ROWS, 1), F32)],
        compiler_params=_params("arbitrary", "arbitrary"),
        name="post_mixer",
    )(ya, yb, ga, gb, x2d, wa, wb, wo, g_cross, wcq, kc, vc, wco, g_ffn, wr, br, tri)


def _expert_kernel(be_ref, nused_ref, xs_ref, wg_ref, wu_ref, wd_ref, ys_ref, wg_s, wu_s, wd_s):
    i = pl.program_id(0)

    @pl.when((i == 0) | (be_ref[i] != be_ref[jnp.maximum(i - 1, 0)]))
    def _():
        wg_s[...] = wg_ref[0].astype(BF16)
        wu_s[...] = wu_ref[0].astype(BF16)
        wd_s[...] = wd_ref[0].astype(BF16)

    @pl.when(i < nused_ref[0])
    def _():
        xb = xs_ref[...]
        g = _dot(xb, wg_s[...])
        u = _dot(xb, wu_s[...])
        hb = (g * _sigmoid(g) * u).astype(BF16)
        ys_ref[...] = _dot(hb, wd_s[...]).astype(ys_ref.dtype)

    @pl.when(i >= nused_ref[0])
    def _():
        ys_ref[...] = jnp.zeros_like(ys_ref)


def _experts(block_expert, n_used, xs, wg, wu, wd):
    n_rows, d = xs.shape
    n_blocks = n_rows // MOE_BLOCK
    ff = wg.shape[2]
    grid_spec = pltpu.PrefetchScalarGridSpec(
        num_scalar_prefetch=2,
        grid=(n_blocks,),
        in_specs=[pl.BlockSpec((MOE_BLOCK, d), lambda i, be, nu: (i, 0)),
                  pl.BlockSpec((1, d, ff), lambda i, be, nu: (be[i], 0, 0)),
                  pl.BlockSpec((1, d, ff), lambda i, be, nu: (be[i], 0, 0)),
                  pl.BlockSpec((1, ff, d), lambda i, be, nu: (be[i], 0, 0))],
        out_specs=pl.BlockSpec((MOE_BLOCK, d), lambda i, be, nu: (i, 0)),
        scratch_shapes=[pltpu.VMEM((d, ff), BF16), pltpu.VMEM((d, ff), BF16),
                        pltpu.VMEM((ff, d), BF16)],
    )
    return pl.pallas_call(
        _expert_kernel,
        grid_spec=grid_spec,
        out_shape=jax.ShapeDtypeStruct((n_rows, d), BF16),
        compiler_params=_params("arbitrary"),
        name="experts",
    )(block_expert, n_used, xs, wg, wu, wd)


def _final_kernel(x_ref, y0_ref, y1_ref, route_ref, g_ref, out_ref, *, normalize):
    route = route_ref[...]
    w0 = route[:, 2 * TOP_K:2 * TOP_K + 1]
    w1 = route[:, 2 * TOP_K + 1:2 * TOP_K + 2]
    out = x_ref[...] + w0 * y0_ref[...].astype(F32) + w1 * y1_ref[...].astype(F32)
    out_ref[...] = _rms(out, g_ref[...], NORM_EPS) if normalize else out


def _final(x2, y0, y1, route, g_final, tm, normalize):
    n_tok, d = x2.shape
    tok = lambda i: (i, 0)
    return pl.pallas_call(
        functools.partial(_final_kernel, normalize=normalize),
        grid=(n_tok // tm,),
        in_specs=[pl.BlockSpec((tm, d), tok), pl.BlockSpec((tm, d), tok), pl.BlockSpec((tm, d), tok),
                  pl.BlockSpec((tm, LANES), tok), pl.BlockSpec((1, d), lambda i: (0, 0))],
        out_specs=pl.BlockSpec((tm, d), tok),
        out_shape=jax.ShapeDtypeStruct((n_tok, d), F32),
        compiler_params=_params("arbitrary"),
        name="final_norm",
    )(x2, y0, y1, route, g_final)


def _slot_kernel(tab_ref, rt_ref, dest_ref, key_ref):
    e = rt_ref[0:TOP_K, :]
    rank = rt_ref[TOP_K:2 * TOP_K, :].astype(jnp.int32)
    pad = jnp.zeros(e.shape, jnp.int32)
    seg = jnp.zeros(e.shape, jnp.int32)
    for k in range(N_EXPERTS):
        hit = e == float(k)
        pad = jnp.where(hit, tab_ref[k], pad)
        seg = jnp.where(hit, tab_ref[N_EXPERTS + k], seg)
    dest_ref[...] = pad + rank
    key_ref[...] = seg + rank


def _slots(tables, route_t, tm):
    n_tok = route_t.shape[1]
    out = jax.ShapeDtypeStruct((TOP_K, n_tok), jnp.int32)
    grid_spec = pltpu.PrefetchScalarGridSpec(
        num_scalar_prefetch=1,
        grid=(n_tok // tm,),
        in_specs=[pl.BlockSpec((route_t.shape[0], tm), lambda i, tab: (0, i))],
        out_specs=[pl.BlockSpec((TOP_K, tm), lambda i, tab: (0, i))] * 2,
    )
    return pl.pallas_call(_slot_kernel, grid_spec=grid_spec, out_shape=[out, out],
                          compiler_params=_params("arbitrary"), name="moe_slots")(tables, route_t)


def _route(route_t, counts, n_tok):
    n_assign = n_tok * TOP_K
    counts = counts[N_GROUPS:N_GROUPS + N_EXPERTS, 0].astype(jnp.int32)
    seg_start = jnp.cumsum(counts) - counts
    padded = (counts + MOE_BLOCK - 1) // MOE_BLOCK * MOE_BLOCK
    pad_end = jnp.cumsum(padded)
    pad_start = pad_end - padded
    dest, key = _slots(jnp.concatenate([pad_start, seg_start]), route_t, tm=min(8192, n_tok))
    order = jnp.argsort(key.reshape(n_assign)).astype(jnp.int32)

    n_blocks = -(-n_assign // MOE_BLOCK) + N_EXPERTS
    n_rows = n_blocks * MOE_BLOCK
    block_start = jnp.arange(n_blocks, dtype=jnp.int32) * MOE_BLOCK
    block_expert = jnp.minimum(jnp.sum(block_start[:, None] >= pad_end[None, :], axis=1),
                               N_EXPERTS - 1).astype(jnp.int32)
    onehot = block_expert[:, None] == jnp.arange(N_EXPERTS, dtype=jnp.int32)[None, :]

    def of_block(table):
        return jnp.sum(jnp.where(onehot, table[None, :], 0), axis=1)

    offset = block_start - of_block(pad_start)
    in_block = jnp.arange(MOE_BLOCK, dtype=jnp.int32)[None, :]
    valid = in_block < (of_block(counts) - offset)[:, None]
    src = jnp.clip((of_block(seg_start) + offset)[:, None] + in_block, 0, n_assign - 1)
    rows = block_start[:, None] + in_block
    row_tok = jnp.where(valid, order[src] % n_tok, rows % n_tok).reshape(n_rows)
    n_used = (pad_end[-1] // MOE_BLOCK).astype(jnp.int32).reshape(1)
    return row_tok, dest, block_expert, n_used


def kernel(x, mem, positions, g_mix, w_in, b_fgate, w_branch_a, w_branch_b, w_out, lambda_q1, lambda_k1, lambda_q2, lambda_k2, g_diff_sub, g_cross, g_mem, w_cq, w_ckv, w_co, g_ffn, w_group, b_group, w_expert, b_expert, w_exp_gate, w_exp_up, w_exp_down, g_final):
    n_batch, seq, d = x.shape
    mem_len = mem.shape[1]
    depth = g_mix.shape[0]
    n_tok = n_batch * seq
    fox_w = FOX_HEADS * HEAD_DIM
    diff_w = DIFF_HEADS * 2 * HEAD_DIM

    half = HEAD_DIM // 2
    inv_freq = 10000.0 ** (-jnp.arange(half, dtype=F32) * 2.0 / HEAD_DIM)
    ang = positions.astype(F32).reshape(n_tok, 1) * inv_freq[None, :]
    rot_t = jnp.concatenate([jnp.cos(ang), jnp.sin(ang), jnp.zeros((n_tok, LANES - 2 * half), F32)],
                            axis=1)
    sel = _bias_lane_selectors()

    x2d = x.reshape(n_tok, d)
    mem2d = mem.reshape(n_batch * mem_len, d)
    for l in range(depth):
        lambda_init = 0.8 - 0.6 * math.exp(-0.3 * l)
        o_fv = 2 * fox_w
        o_fl = o_fv + fox_w
        o_dq = o_fl + FOX_HEADS
        o_dv = o_dq + 2 * diff_w
        o_ga = o_dv + diff_w
        wl = w_in[l]
        w_main = jnp.concatenate([wl[:, :o_fv], wl[:, o_dq:o_dv], wl[:, o_ga:]], axis=1).astype(BF16)
        w_vt = jnp.concatenate([wl[:, o_fv:o_fl], wl[:, o_dv:o_ga]], axis=1).T.astype(BF16)
        w_f = jnp.pad(wl[:, o_fl:o_dq], ((0, 0), (0, LANES - FOX_HEADS))).astype(BF16)
        b_row = jnp.pad(b_fgate[l], (0, LANES - FOX_HEADS))[None]

        kc, vc = _mem_kv(mem2d, g_mem[l][None], w_ckv[l].astype(BF16), n_batch, mem_len)
        fq, fk, dq, dk, ga, gb, fvt, dvt, fl = _in_proj(
            x2d, g_mix[l][None], w_main, w_vt, w_f, rot_t, tm=TOKEN_BLOCK)
        bias_lanes = _fox_prep(fl, b_row, sel, n_batch, seq, tc=TOKEN_BLOCK)
        y_a = _attention(fq, fk, fvt, (bias_lanes,), n_batch, seq, fox=True)
        lam_vec = jnp.stack([lambda_q1[l], lambda_k1[l], lambda_q2[l], lambda_k2[l]])
        y_b = _attention(dq, dk, dvt, (lam_vec, g_diff_sub[l][:, None]), n_batch, seq,
                         fox=False, lambda_init=lambda_init)

        w_router = jnp.pad(jnp.concatenate([w_group[l], w_expert[l]], axis=1).T,
                           ((0, ROUTE_ROWS - N_GROUPS - N_EXPERTS), (0, 0)))
        w_router_hi = w_router.astype(BF16)
        w_router = jnp.concatenate(
            [w_router_hi, (w_router - w_router_hi.astype(F32)).astype(BF16)], axis=0)
        b_router = jnp.pad(jnp.concatenate([b_group[l], b_expert[l]]),
                           (0, ROUTE_ROWS - N_GROUPS - N_EXPERTS))[:, None]
        x2d, hm, route, route_t, counts = _post_mixer(
            y_a, y_b, ga, gb, x2d, w_branch_a[l].astype(BF16), w_branch_b[l].astype(BF16),
            w_out[l].astype(BF16), g_cross[l][None], w_cq[l].astype(BF16), kc, vc,
            w_co[l].astype(BF16), g_ffn[l][None], w_router, b_router,
            n_batch, seq, mem_len, tm=TOKEN_BLOCK)

        row_tok, dest, block_expert, n_used = _route(route_t, counts, n_tok)
        xs = hm[row_tok]
        ys = _experts(block_expert, n_used, xs, w_exp_gate[l], w_exp_up[l], w_exp_down[l])
        last = l + 1 == depth
        x2d = _final(x2d, ys[dest[0]], ys[dest[1]], route, g_final[None], tm=COMBINE_BLOCK,
                     normalize=last)
    return x2d.reshape(n_batch, seq, d)
```

```python
import functools
import math

import numpy as np
import jax
import jax.numpy as jnp
from jax import lax
from jax.experimental import pallas as pl
from jax.experimental.pallas import tpu as pltpu

HEAD_DIM = 64
LANES = 128
FOX_HEADS = 8
DIFF_HEADS = 4
CROSS_HEADS = 4
N_GROUPS = 4
EXPERTS_PER_GROUP = 8
N_EXPERTS = N_GROUPS * EXPERTS_PER_GROUP
TOP_K = 2
MOE_BLOCK = 512
TOKEN_BLOCK = 512
LIGHT_BLOCK = 1024
ATTN_Q_BLOCK = 1024
ATTN_K_BLOCK = 256
NORM_EPS = 1e-6
SUBLN_EPS = 1e-5
QK_SCALE = HEAD_DIM ** -0.5
LOG2E = math.log2(math.e)
Q_PRESCALE = QK_SCALE * LOG2E
NEG_BIG = -1e30
VMEM_LIMIT = 56 * 2**20

BF16 = jnp.bfloat16
F32 = jnp.float32


def _rms(t, g, eps):
    return t * lax.rsqrt(jnp.mean(t * t, axis=-1, keepdims=True) + eps) * g


def _sigmoid(t):
    return 0.5 + 0.5 * jnp.tanh(0.5 * t)


def _dot(a, b):
    return jnp.dot(a, b, preferred_element_type=F32)


def _dot_nt(a, b):
    return lax.dot_general(a, b, (((1,), (1,)), ((), ())), preferred_element_type=F32)


def _params(*sem):
    return pltpu.CompilerParams(dimension_semantics=sem, vmem_limit_bytes=VMEM_LIMIT)


def _mem_kv_kernel(mem_ref, g_ref, w_ref, k_ref, v_ref):
    h = _rms(mem_ref[...], g_ref[...], NORM_EPS).astype(BF16)
    kv = _dot(h, w_ref[...])
    width = k_ref.shape[1]
    k_ref[...] = kv[:, :width].astype(BF16)
    v_ref[...] = kv[:, width:].astype(BF16)


def _mem_kv(mem2, g_mem, w_ckv, n_batch, mem_len):
    d = mem2.shape[1]
    cw = w_ckv.shape[1] // 2
    out = jax.ShapeDtypeStruct((n_batch * mem_len, cw), BF16)
    return pl.pallas_call(
        _mem_kv_kernel,
        grid=(n_batch,),
        in_specs=[pl.BlockSpec((mem_len, d), lambda b: (b, 0)),
                  pl.BlockSpec((1, d), lambda b: (0, 0)),
                  pl.BlockSpec((d, 2 * cw), lambda b: (0, 0))],
        out_specs=[pl.BlockSpec((mem_len, cw), lambda b: (b, 0))] * 2,
        out_shape=[out, out],
        compiler_params=_params("arbitrary"),
        name="mem_kv",
    )(mem2, g_mem, w_ckv)


def _rope(t, cos, sin_signed, first_half):
    fwd = pltpu.roll(t, LANES - HEAD_DIM // 2, axis=1)
    bwd = pltpu.roll(t, HEAD_DIM // 2, axis=1)
    return t * cos + jnp.where(first_half, fwd, bwd) * sin_signed


def _in_proj_kernel(x_ref, g_ref, w_ref, wvt_ref, wf_ref, rot_ref,
                    fq_ref, fk_ref, dq_ref, dk_ref, ga_ref, gb_ref, fvt_ref, dvt_ref, fl_ref):
    h = _rms(x_ref[...], g_ref[...], NORM_EPS).astype(BF16)
    tm = h.shape[0]
    cw = fq_ref.shape[1]

    def proj(chunk):
        return _dot(h, w_ref[:, chunk * cw:(chunk + 1) * cw])

    fq_ref[...] = (proj(0) * Q_PRESCALE).astype(BF16)
    fk_ref[...] = proj(1).astype(BF16)

    lane = lax.broadcasted_iota(jnp.int32, (tm, LANES), 1)
    rot = rot_ref[...]
    quarter = HEAD_DIM // 2
    c = jnp.where(lane < quarter, rot, 0.0)
    s = jnp.where((lane >= quarter) & (lane < 2 * quarter), rot, 0.0)
    cos = c + pltpu.roll(c, quarter, axis=1) + pltpu.roll(c, 2 * quarter, axis=1) + pltpu.roll(c, 3 * quarter, axis=1)
    sin = (s + pltpu.roll(s, 2 * quarter, axis=1)) - (pltpu.roll(s, quarter, axis=1) + pltpu.roll(s, 3 * quarter, axis=1))
    first_half = (lane % HEAD_DIM) < HEAD_DIM // 2
    for out_ref, chunk, scale in ((dq_ref, 2, Q_PRESCALE), (dk_ref, 3, 1.0)):
        t = proj(chunk)
        for c in range(cw // LANES):
            blk = _rope(t[:, c * LANES:(c + 1) * LANES], cos, sin, first_half)
            out_ref[:, c * LANES:(c + 1) * LANES] = (blk * scale).astype(BF16)

    for out_ref, chunk in ((ga_ref, 4), (gb_ref, 6)):
        for c in range(2):
            out_ref[:, c * cw:(c + 1) * cw] = proj(chunk + c).astype(BF16)
    fvt_ref[...] = _dot_nt(wvt_ref[0:cw, :], h).astype(BF16)
    dvt_ref[...] = _dot_nt(wvt_ref[cw:2 * cw, :], h).astype(BF16)
    fl_ref[...] = _dot(h, wf_ref[...])


def _in_proj(x2, g_mix, w_main, w_vt, w_f, rot_t, tm):
    n_tok, d = x2.shape
    cw = 512
    tok = lambda i: (i, 0)
    tok_t = lambda i: (0, i)
    const = lambda i: (0, 0)
    o512 = jax.ShapeDtypeStruct((n_tok, cw), BF16)
    o1024 = jax.ShapeDtypeStruct((n_tok, 2 * cw), BF16)
    o512_t = jax.ShapeDtypeStruct((cw, n_tok), BF16)
    return pl.pallas_call(
        _in_proj_kernel,
        grid=(n_tok // tm,),
        in_specs=[pl.BlockSpec((tm, d), tok),
                  pl.BlockSpec((1, d), const),
                  pl.BlockSpec(w_main.shape, const, pipeline_mode=pl.Buffered(1)),
                  pl.BlockSpec(w_vt.shape, const, pipeline_mode=pl.Buffered(1)),
                  pl.BlockSpec(w_f.shape, const),
                  pl.BlockSpec((tm, LANES), tok)],
        out_specs=[pl.BlockSpec((tm, cw), tok)] * 4
                  + [pl.BlockSpec((tm, 2 * cw), tok)] * 2
                  + [pl.BlockSpec((cw, tm), tok_t)] * 2
                  + [pl.BlockSpec((tm, LANES), tok)],
        out_shape=[o512] * 4 + [o1024] * 2 + [o512_t] * 2
                  + [jax.ShapeDtypeStruct((n_tok, LANES), F32)],
        compiler_params=_params("arbitrary"),
        name="in_proj",
    )(x2, g_mix, w_main, w_vt, w_f, rot_t)


HEAD_ROWS = 16


def _bias_lane_selectors():
    sel = np.zeros((LANES, FOX_HEADS // 2 * LANES), np.float32)
    for h in range(FOX_HEADS):
        base = (h // 2) * LANES + (HEAD_DIM if h % 2 == 0 else 0)
        for piece in range(3):
            sel[HEAD_ROWS * piece + h, base + piece] = 1.0
    return jnp.asarray(sel, BF16)


def _split3(t):
    pieces = []
    for _ in range(3):
        part = t.astype(BF16)
        t = t - part.astype(F32)
        pieces.append(part)
    return pieces


def _fox_prep_kernel(fl_ref, b_ref, tri_ref, sel_ref, out_ref, carry_ref):
    @pl.when(pl.program_id(1) == 0)
    def _():
        carry_ref[...] = jnp.zeros_like(carry_ref)

    z = fl_ref[...] + b_ref[...]
    log_f = jnp.minimum(z, 0.0) - jnp.log(1.0 + jnp.exp(-jnp.abs(z)))
    tc = z.shape[0]
    log_f_t = log_f.T[0:HEAD_ROWS, :]
    c = carry_ref[...]
    for part in _split3(log_f_t):
        c = c + _dot(part, tri_ref[...])
    carry_ref[...] = c[:, tc - 1:tc]
    pieces = [p.astype(F32) for p in _split3(c * (-LOG2E))]
    stacked = jnp.concatenate(pieces + [jnp.zeros((LANES - 3 * HEAD_ROWS, tc), F32)], axis=0)
    out_ref[...] = _dot(stacked.T.astype(BF16), sel_ref[...]).astype(BF16)


def _fox_prep(fl, b_row, sel, n_batch, seq, tc):
    n_tok = fl.shape[0]
    nblk = seq // tc
    width = sel.shape[1]
    tri = jnp.triu(jnp.ones((tc, tc), BF16))
    return pl.pallas_call(
        _fox_prep_kernel,
        grid=(n_batch, nblk),
        in_specs=[pl.BlockSpec((tc, LANES), lambda b, j: (b * nblk + j, 0)),
                  pl.BlockSpec((1, LANES), lambda b, j: (0, 0)),
                  pl.BlockSpec((tc, tc), lambda b, j: (0, 0)),
                  pl.BlockSpec(sel.shape, lambda b, j: (0, 0))],
        out_specs=pl.BlockSpec((tc, width), lambda b, j: (b * nblk + j, 0)),
        out_shape=jax.ShapeDtypeStruct((n_tok, width), BF16),
        scratch_shapes=[pltpu.VMEM((HEAD_ROWS, 1), F32)],
        compiler_params=_params("arbitrary", "arbitrary"),
        name="fox_prep",
    )(fl, b_row, tri, sel)


def _attn_kernel(*refs, bq, bk, fox, lambda_init):
    for qi in range(refs[0].shape[0] // bq):
        _attn_q_block(qi, refs, bq, bk, fox, lambda_init)


def _attn_q_block(qi, refs, bq, bk, fox, lambda_init):
    if fox:
        q_ref, k_ref, vt_ref, bl_ref, out_ref, qs_ref, st_ref, p_ref, acc_ref, stat_ref, ka_ref, kb_ref = refs
    else:
        q_ref, k_ref, vt_ref, lam_ref, gsub_ref, out_ref, qs_ref, st_ref, p_ref, acc_ref, stat_ref = refs
    q = q_ref[qi * bq:(qi + 1) * bq, :].astype(F32)
    lane = lax.broadcasted_iota(jnp.int32, q.shape, 1)
    qs_ref = qs_ref.at[qi % 2]
    if fox:
        if qi == 0:
            k = k_ref[...].astype(F32)
            bl = bl_ref[...].astype(F32)
            klane = lax.broadcasted_iota(jnp.int32, k.shape, 1)
            ka_ref[...] = jnp.where(klane < HEAD_DIM, k, bl).astype(BF16)
            kb_ref[...] = jnp.where(klane >= HEAD_DIM, k, bl).astype(BF16)

        q_streams = (jnp.where(lane < HEAD_DIM, q, jnp.where(lane < HEAD_DIM + 3, 1.0, 0.0)),
                     jnp.where(lane >= HEAD_DIM, q, jnp.where(lane < 3, 1.0, 0.0)))
        k_srcs = (ka_ref, kb_ref)
        v_rows = ((0, HEAD_DIM), (HEAD_DIM, 2 * HEAD_DIM))
    else:
        q_streams = (jnp.where(lane < HEAD_DIM, q, 0.0), jnp.where(lane >= HEAD_DIM, q, 0.0))
        k_srcs = (k_ref, k_ref)
        v_rows = ((0, 2 * HEAD_DIM), (0, 2 * HEAD_DIM))
    n_v = v_rows[0][1] - v_rows[0][0]
    ring = bq // bk
    assert ring * bk == bq and ring % 2 == 0 and ring >= 4
    for idx in range(2):
        qs_ref[idx] = q_streams[idx].astype(BF16)
        p_ref[idx, 1] = jnp.zeros((bk, bq), BF16)
        acc_ref[idx] = jnp.zeros((n_v + 16, bq), F32)
    ones_rows = jnp.ones((16, bk), BF16)
    first_diag = ring * qi

    def key_start(kj):
        return kj * bk if isinstance(kj, int) else pl.multiple_of(kj * bk, bk)

    def qk_stage(kj, idx, slot, q_lo=0):
        k0 = key_start(kj)
        st_ref[idx, slot, :, q_lo:] = _dot_nt(k_srcs[idx][pl.ds(k0, bk), :], qs_ref[idx, q_lo:, :])

    M_ROW, A_ROW = 0, 8

    def pv_stage(kj, idx, pslot, q_lo=0):
        k0 = key_start(kj)
        r0, r1 = v_rows[idx]
        v_aug = jnp.concatenate([vt_ref[r0:r1, pl.ds(k0, bk)], ones_rows], axis=0)
        alpha = stat_ref[idx, A_ROW + 8 * pslot:A_ROW + 8 * pslot + 1, q_lo:]
        acc_ref[idx, :, q_lo:] = (alpha * acc_ref[idx, :, q_lo:]
                                  + _dot(v_aug, p_ref[idx, pslot, :, q_lo:]))

    def softmax_stage(idx, slot, pslot, masked, q_lo=0):
        st = st_ref[idx, slot, :, q_lo:]
        if masked:
            key_idx = lax.broadcasted_iota(jnp.int32, st.shape, 0)
            qry_idx = lax.broadcasted_iota(jnp.int32, st.shape, 1)
            st = jnp.where(key_idx <= qry_idx, st, NEG_BIG)
        m = stat_ref[idx, M_ROW:M_ROW + 1, q_lo:]
        m_new = jnp.maximum(m, jnp.max(st, axis=0, keepdims=True))
        p_ref[idx, pslot, :, q_lo:] = jnp.exp2(st - m_new).astype(BF16)
        stat_ref[idx, M_ROW:M_ROW + 1, q_lo:] = m_new
        stat_ref[idx, A_ROW + 8 * pslot:A_ROW + 8 * pslot + 1, q_lo:] = jnp.exp2(m - m_new)

    for idx in range(2):
        stat_ref[idx, M_ROW:M_ROW + 8, :] = jnp.full((8, bq), NEG_BIG, F32)
        stat_ref[idx, A_ROW:A_ROW + 16, :] = jnp.ones((16, bq), F32)
        qk_stage(0, idx, 0)
        qk_stage(1, idx, 1)

    def block_group(j, diag_group):
        for s in range(ring):
            lo_prev = (s - 1) * bk if diag_group and s > 0 else 0
            lo_this = s * bk if diag_group else 0
            for idx in range(2):
                if not diag_group:
                    qk_stage(j + s + 2, idx, (s + 2) % ring)
                elif s + 2 < ring:
                    qk_stage(j + s + 2, idx, s + 2, (s + 2) * bk)
                prev = j + s - 1
                prev = max(prev, 0) if isinstance(prev, int) else jnp.maximum(prev, 0)
                pv_stage(prev, idx, (s - 1) % 2, lo_prev)
                softmax_stage(idx, s, s % 2, diag_group, lo_this)

    def body(i, carry):
        block_group(ring * i, False)
        return carry

    if qi > 0:
        lax.fori_loop(0, qi, body, 0)
    block_group(first_diag, True)
    outs = []
    for idx in range(2):
        pv_stage(first_diag + ring - 1, idx, (ring - 1) % 2, (ring - 1) * bk)
        acc = acc_ref[idx]
        outs.append(acc[0:n_v] * (1.0 / acc[n_v:n_v + 1]))
    o_a, o_b = outs
    if fox:
        y = jnp.concatenate([o_a, o_b], axis=0)
    else:
        lam_vec = lam_ref[...]
        lam = (jnp.exp(jnp.sum(lam_vec[0:1] * lam_vec[1:2], axis=1, keepdims=True))
               - jnp.exp(jnp.sum(lam_vec[2:3] * lam_vec[3:4], axis=1, keepdims=True))
               + lambda_init)
        y = o_a - lam * o_b
        y = y * lax.rsqrt(jnp.mean(y * y, axis=0, keepdims=True) + SUBLN_EPS)
        y = y * (gsub_ref[...] * (1.0 - lambda_init))
    out_ref[qi * bq:(qi + 1) * bq, :] = y.T.astype(out_ref.dtype)


def _attention(q, k, vt, extra, n_batch, seq, fox, lambda_init=0.0):
    n_tok, width = q.shape
    n_pairs = width // LANES
    bq, bk = ATTN_Q_BLOCK, ATTN_K_BLOCK
    n_v = HEAD_DIM if fox else 2 * HEAD_DIM
    k_spec = pl.BlockSpec((seq, LANES), lambda b, hp: (b, hp))
    q_spec = k_spec
    vt_spec = pl.BlockSpec((LANES, seq), lambda b, hp: (hp, b))
    scratch = [pltpu.VMEM((2, 2, bq, LANES), BF16),
               pltpu.VMEM((2, bq // bk, bk, bq), F32),
               pltpu.VMEM((2, 2, bk, bq), BF16),
               pltpu.VMEM((2, n_v + 16, bq), F32),
               pltpu.VMEM((2, 24, bq), F32)]
    if fox:
        extra_specs = [k_spec]
        scratch += [pltpu.VMEM((seq, LANES), BF16)] * 2
    else:
        lam_vec, g_sub = extra
        extra_specs = [pl.BlockSpec(lam_vec.shape, lambda b, hp: (0, 0)),
                       pl.BlockSpec(g_sub.shape, lambda b, hp: (0, 0))]
    return pl.pallas_call(
        functools.partial(_attn_kernel, bq=bq, bk=bk, fox=fox, lambda_init=lambda_init),
        grid=(n_batch, n_pairs),
        in_specs=[q_spec, k_spec, vt_spec] + extra_specs,
        out_specs=q_spec,
        out_shape=jax.ShapeDtypeStruct((n_tok, width), BF16),
        scratch_shapes=scratch,
        compiler_params=_params("arbitrary", "arbitrary"),
        name="fox_attn" if fox else "diff_attn",
    )(q, k, vt, *extra)


ROUTE_ROWS = 48


def _route_block(logits_t, tri, cnt_ref):
    rows, n = logits_t.shape
    row_f = lax.broadcasted_iota(jnp.int32, (rows, n), 0).astype(F32)

    def first_argmax(t):
        top = jnp.max(t, axis=0, keepdims=True)
        return top, jnp.min(jnp.where(t == top, row_f, float(rows)), axis=0, keepdims=True)

    is_group = row_f < N_GROUPS
    g_max, g_sel = first_argmax(jnp.where(is_group, logits_t, NEG_BIG))
    g_w = 1.0 / jnp.sum(jnp.where(is_group, jnp.exp(logits_t - g_max), 0.0), axis=0, keepdims=True)
    lo = N_GROUPS + EXPERTS_PER_GROUP * g_sel
    scores = jnp.where((row_f >= lo) & (row_f < lo + EXPERTS_PER_GROUP), logits_t, NEG_BIG)
    v1, i1 = first_argmax(scores)
    v2, i2 = first_argmax(jnp.where(row_f == i1, NEG_BIG, scores))
    t = jnp.exp(v2 - v1)
    w1 = g_w / (1.0 + t)
    w2 = w1 * t

    hit1 = row_f == i1
    hit2 = row_f == i2
    onehot = jnp.where(hit1 | hit2, 1.0, 0.0)
    before = _dot(onehot.astype(BF16), tri) + cnt_ref[...]
    r1 = jnp.sum(jnp.where(hit1, before, 0.0), axis=0, keepdims=True)
    r2 = jnp.sum(jnp.where(hit2, before, 0.0), axis=0, keepdims=True)
    cnt_ref[...] = before[:, n - 1:n] + onehot[:, n - 1:n]
    out_row = lax.broadcasted_iota(jnp.int32, (8, n), 0)
    out = jnp.zeros((8, n), F32)
    for pos, val in enumerate((i1 - N_GROUPS, i2 - N_GROUPS, r1, r2, w1, w2)):
        out = jnp.where(out_row == pos, val, out)
    return out


def _post_mixer_kernel(ya_ref, yb_ref, ga_ref, gb_ref, x_ref, wa_ref, wb_ref, wo_ref,
                       gc_ref, wcq_ref, kc_ref, vc_ref, wco_ref, gf_ref, wr_ref, br_ref, tri_ref,
                       x2_ref, hm_ref, route_ref, route_t_ref, cnt_out_ref, cnt_ref):
    @pl.when((pl.program_id(0) == 0) & (pl.program_id(1) == 0))
    def _():
        cnt_ref[...] = jnp.zeros_like(cnt_ref)

    merged = (_sigmoid(ga_ref[...].astype(F32)) * _dot(ya_ref[...], wa_ref[...])
              + _sigmoid(gb_ref[...].astype(F32)) * _dot(yb_ref[...], wb_ref[...]))
    x1 = x_ref[...] + _dot(merged.astype(BF16), wo_ref[...])

    hx = _rms(x1, gc_ref[...], NORM_EPS).astype(BF16)
    qc = (_dot(hx, wcq_ref[...]) * QK_SCALE).astype(BF16)
    kc = kc_ref[...]
    vc = vc_ref[...]
    lane = lax.broadcasted_iota(jnp.int32, qc.shape, 1)
    zero = jnp.zeros_like(qc)
    o = jnp.zeros(qc.shape, F32)
    for h in range(CROSS_HEADS):
        in_head = (lane >= h * HEAD_DIM) & (lane < (h + 1) * HEAD_DIM)
        s = _dot_nt(jnp.where(in_head, qc, zero), kc)
        p = jnp.exp(s - jnp.max(s, axis=1, keepdims=True))
        p = p * (1.0 / jnp.sum(p, axis=1, keepdims=True))
        o = jnp.where(in_head, _dot(p.astype(BF16), vc), o)
    x2 = x1 + _dot(o.astype(BF16), wco_ref[...])
    x2_ref[...] = x2

    hm = _rms(x2, gf_ref[...], NORM_EPS)
    hm_ref[...] = hm.astype(BF16)
    hm_hi = hm.astype(BF16)
    hm_lo = (hm - hm_hi.astype(F32)).astype(BF16)
    both = _dot_nt(wr_ref[...], hm_hi)
    logits_t = (both[0:ROUTE_ROWS] + both[ROUTE_ROWS:2 * ROUTE_ROWS]
                + _dot_nt(wr_ref[0:ROUTE_ROWS, :], hm_lo) + br_ref[...])
    route_t = _route_block(logits_t, tri_ref[...], cnt_ref)
    route_t_ref[...] = route_t
    pad_rows = jnp.zeros((LANES - route_t.shape[0], route_t.shape[1]), F32)
    route_ref[...] = jnp.concatenate([route_t, pad_rows], axis=0).T
    cnt_out_ref[...] = jnp.broadcast_to(cnt_ref[...], cnt_out_ref.shape)


def _post_mixer(ya, yb, ga, gb, x2d, wa, wb, wo, g_cross, wcq, kc, vc, wco, g_ffn, wr, br,
                n_batch, seq, mem_len, tm):
    n_tok, d = x2d.shape
    nblk = seq // tm
    tok = lambda b, j: (b * nblk + j, 0)
    const = lambda b, j: (0, 0)
    full = lambda a: pl.BlockSpec(a.shape, const)
    tri = jnp.triu(jnp.ones((tm, tm), BF16), 1)
    return pl.pallas_call(
        _post_mixer_kernel,
        grid=(n_batch, nblk),
        in_specs=[pl.BlockSpec((tm, ya.shape[1]), tok), pl.BlockSpec((tm, yb.shape[1]), tok),
                  pl.BlockSpec((tm, d), tok), pl.BlockSpec((tm, d), tok), pl.BlockSpec((tm, d), tok),
                  full(wa), full(wb), full(wo), full(g_cross), full(wcq),
                  pl.BlockSpec((mem_len, kc.shape[1]), lambda b, j: (b, 0)),
                  pl.BlockSpec((mem_len, vc.shape[1]), lambda b, j: (b, 0)),
                  full(wco), full(g_ffn), full(wr), full(br),
                  full(tri)],
        out_specs=[pl.BlockSpec((tm, d), tok), pl.BlockSpec((tm, d), tok),
                   pl.BlockSpec((tm, LANES), tok),
                   pl.BlockSpec((8, tm), lambda b, j: (0, b * nblk + j)),
                   pl.BlockSpec((ROUTE_ROWS, LANES), const)],
        out_shape=[jax.ShapeDtypeStruct((n_tok, d), F32),
                   jax.ShapeDtypeStruct((n_tok, d), BF16),
                   jax.ShapeDtypeStruct((n_tok, LANES), F32),
                   jax.ShapeDtypeStruct((8, n_tok), F32),
                   jax.ShapeDtypeStruct((ROUTE_ROWS, LANES), F32)],
        scratch_shapes=[pltpu.VMEM((ROUTE_ROWS, 1), F32)],
        compiler_params=_params("arbitrary", "arbitrary"),
        name="post_mixer",
    )(ya, yb, ga, gb, x2d, wa, wb, wo, g_cross, wcq, kc, vc, wco, g_ffn, wr, br, tri)


def _expert_kernel(be_ref, nused_ref, xs_ref, wg_ref, wu_ref, wd_ref, ys_ref, wg_s, wu_s, wd_s):
    i = pl.program_id(0)

    @pl.when((i == 0) | (be_ref[i] != be_ref[jnp.maximum(i - 1, 0)]))
    def _():
        wg_s[...] = wg_ref[0].astype(BF16)
        wu_s[...] = wu_ref[0].astype(BF16)
        wd_s[...] = wd_ref[0].astype(BF16)

    @pl.when(i < nused_ref[0])
    def _():
        xb = xs_ref[...]
        g = _dot(xb, wg_s[...])
        u = _dot(xb, wu_s[...])
        hb = (g * _sigmoid(g) * u).astype(BF16)
        ys_ref[...] = _dot(hb, wd_s[...]).astype(ys_ref.dtype)

    @pl.when(i >= nused_ref[0])
    def _():
        ys_ref[...] = jnp.zeros_like(ys_ref)


def _experts(block_expert, n_used, xs, wg, wu, wd):
    n_rows, d = xs.shape
    n_blocks = n_rows // MOE_BLOCK
    ff = wg.shape[2]
    grid_spec = pltpu.PrefetchScalarGridSpec(
        num_scalar_prefetch=2,
        grid=(n_blocks,),
        in_specs=[pl.BlockSpec((MOE_BLOCK, d), lambda i, be, nu: (i, 0)),
                  pl.BlockSpec((1, d, ff), lambda i, be, nu: (be[i], 0, 0)),
                  pl.BlockSpec((1, d, ff), lambda i, be, nu: (be[i], 0, 0)),
                  pl.BlockSpec((1, ff, d), lambda i, be, nu: (be[i], 0, 0))],
        out_specs=pl.BlockSpec((MOE_BLOCK, d), lambda i, be, nu: (i, 0)),
        scratch_shapes=[pltpu.VMEM((d, ff), BF16), pltpu.VMEM((d, ff), BF16),
                        pltpu.VMEM((ff, d), BF16)],
    )
    return pl.pallas_call(
        _expert_kernel,
        grid_spec=grid_spec,
        out_shape=jax.ShapeDtypeStruct((n_rows, d), BF16),
        compiler_params=_params("arbitrary"),
        name="experts",
    )(block_expert, n_used, xs, wg, wu, wd)


def _final_kernel(x_ref, y0_ref, y1_ref, route_ref, g_ref, out_ref, *, normalize):
    route = route_ref[...]
    w0 = route[:, 2 * TOP_K:2 * TOP_K + 1]
    w1 = route[:, 2 * TOP_K + 1:2 * TOP_K + 2]
    out = x_ref[...] + w0 * y0_ref[...].astype(F32) + w1 * y1_ref[...].astype(F32)
    out_ref[...] = _rms(out, g_ref[...], NORM_EPS) if normalize else out


def _final(x2, y_both, route, g_final, tm, normalize):
    n_tok, d = x2.shape
    nblk = n_tok // tm
    tok = lambda i: (i, 0)
    return pl.pallas_call(
        functools.partial(_final_kernel, normalize=normalize),
        grid=(nblk,),
        in_specs=[pl.BlockSpec((tm, d), tok), pl.BlockSpec((tm, d), tok),
                  pl.BlockSpec((tm, d), lambda i: (i + nblk, 0)),
                  pl.BlockSpec((tm, LANES), tok), pl.BlockSpec((1, d), lambda i: (0, 0))],
        out_specs=pl.BlockSpec((tm, d), tok),
        out_shape=jax.ShapeDtypeStruct((n_tok, d), F32),
        compiler_params=_params("arbitrary"),
        name="final_norm",
    )(x2, y_both, y_both, route, g_final)


def _slot_kernel(tab_ref, rt_ref, dest_ref, key_ref):
    e = rt_ref[0:TOP_K, :]
    rank = rt_ref[TOP_K:2 * TOP_K, :].astype(jnp.int32)
    pad = jnp.zeros(e.shape, jnp.int32)
    seg = jnp.zeros(e.shape, jnp.int32)
    for k in range(N_EXPERTS):
        hit = e == float(k)
        pad = jnp.where(hit, tab_ref[k], pad)
        seg = jnp.where(hit, tab_ref[N_EXPERTS + k], seg)
    dest_ref[...] = pad + rank
    key_ref[...] = seg + rank


def _slots(tables, route_t, tm):
    n_tok = route_t.shape[1]
    out = jax.ShapeDtypeStruct((TOP_K, n_tok), jnp.int32)
    grid_spec = pltpu.PrefetchScalarGridSpec(
        num_scalar_prefetch=1,
        grid=(n_tok // tm,),
        in_specs=[pl.BlockSpec((route_t.shape[0], tm), lambda i, tab: (0, i))],
        out_specs=[pl.BlockSpec((TOP_K, tm), lambda i, tab: (0, i))] * 2,
    )
    return pl.pallas_call(_slot_kernel, grid_spec=grid_spec, out_shape=[out, out],
                          compiler_params=_params("arbitrary"), name="moe_slots")(tables, route_t)


def _route(route_t, counts, n_tok):
    n_assign = n_tok * TOP_K
    counts = counts[N_GROUPS:N_GROUPS + N_EXPERTS, 0].astype(jnp.int32)
    seg_start = jnp.cumsum(counts) - counts
    padded = (counts + MOE_BLOCK - 1) // MOE_BLOCK * MOE_BLOCK
    pad_end = jnp.cumsum(padded)
    pad_start = pad_end - padded
    dest, key = _slots(jnp.concatenate([pad_start, seg_start]), route_t, tm=min(8192, n_tok))
    order = jnp.argsort(key.reshape(n_assign)).astype(jnp.int32)

    n_blocks = -(-n_assign // MOE_BLOCK) + N_EXPERTS
    n_rows = n_blocks * MOE_BLOCK
    block_start = jnp.arange(n_blocks, dtype=jnp.int32) * MOE_BLOCK
    block_expert = jnp.minimum(jnp.sum(block_start[:, None] >= pad_end[None, :], axis=1),
                               N_EXPERTS - 1).astype(jnp.int32)
    onehot = block_expert[:, None] == jnp.arange(N_EXPERTS, dtype=jnp.int32)[None, :]

    def of_block(table):
        return jnp.sum(jnp.where(onehot, table[None, :], 0), axis=1)

    offset = block_start - of_block(pad_start)
    in_block = jnp.arange(MOE_BLOCK, dtype=jnp.int32)[None, :]
    valid = in_block < (of_block(counts) - offset)[:, None]
    src = jnp.clip((of_block(seg_start) + offset)[:, None] + in_block, 0, n_assign - 1)
    rows = block_start[:, None] + in_block
    row_tok = jnp.where(valid, order[src] % n_tok, rows % n_tok).reshape(n_rows)
    n_used = (pad_end[-1] // MOE_BLOCK).astype(jnp.int32).reshape(1)
    return row_tok, dest, block_expert, n_used


def kernel(x, mem, positions, g_mix, w_in, b_fgate, w_branch_a, w_branch_b, w_out, lambda_q1, lambda_k1, lambda_q2, lambda_k2, g_diff_sub, g_cross, g_mem, w_cq, w_ckv, w_co, g_ffn, w_group, b_group, w_expert, b_expert, w_exp_gate, w_exp_up, w_exp_down, g_final):
    n_batch, seq, d = x.shape
    mem_len = mem.shape[1]
    depth = g_mix.shape[0]
    n_tok = n_batch * seq
    fox_w = FOX_HEADS * HEAD_DIM
    diff_w = DIFF_HEADS * 2 * HEAD_DIM

    half = HEAD_DIM // 2
    inv_freq = 10000.0 ** (-jnp.arange(half, dtype=F32) * 2.0 / HEAD_DIM)
    ang = positions.astype(F32).reshape(n_tok, 1) * inv_freq[None, :]
    rot_t = jnp.concatenate([jnp.cos(ang), jnp.sin(ang), jnp.zeros((n_tok, LANES - 2 * half), F32)],
                            axis=1)
    sel = _bias_lane_selectors()

    x2d = x.reshape(n_tok, d)
    mem2d = mem.reshape(n_batch * mem_len, d)
    for l in range(depth):
        lambda_init = 0.8 - 0.6 * math.exp(-0.3 * l)
        o_fv = 2 * fox_w
        o_fl = o_fv + fox_w
        o_dq = o_fl + FOX_HEADS
        o_dv = o_dq + 2 * diff_w
        o_ga = o_dv + diff_w
        wl = w_in[l]
        w_main = jnp.concatenate([wl[:, :o_fv], wl[:, o_dq:o_dv], wl[:, o_ga:]], axis=1).astype(BF16)
        w_vt = jnp.concatenate([wl[:, o_fv:o_fl], wl[:, o_dv:o_ga]], axis=1).T.astype(BF16)
        w_f = jnp.pad(wl[:, o_fl:o_dq], ((0, 0), (0, LANES - FOX_HEADS))).astype(BF16)
        b_row = jnp.pad(b_fgate[l], (0, LANES - FOX_HEADS))[None]

        kc, vc = _mem_kv(mem2d, g_mem[l][None], w_ckv[l].astype(BF16), n_batch, mem_len)
        fq, fk, dq, dk, ga, gb, fvt, dvt, fl = _in_proj(
            x2d, g_mix[l][None], w_main, w_vt, w_f, rot_t, tm=TOKEN_BLOCK)
        bias_lanes = _fox_prep(fl, b_row, sel, n_batch, seq, tc=LIGHT_BLOCK)
        y_a = _attention(fq, fk, fvt, (bias_lanes,), n_batch, seq, fox=True)
        lam_vec = jnp.stack([lambda_q1[l], lambda_k1[l], lambda_q2[l], lambda_k2[l]])
        y_b = _attention(dq, dk, dvt, (lam_vec, g_diff_sub[l][:, None]), n_batch, seq,
                         fox=False, lambda_init=lambda_init)

        w_router = jnp.pad(jnp.concatenate([w_group[l], w_expert[l]], axis=1).T,
                           ((0, ROUTE_ROWS - N_GROUPS - N_EXPERTS), (0, 0)))
        w_router_hi = w_router.astype(BF16)
        w_router = jnp.concatenate(
            [w_router_hi, (w_router - w_router_hi.astype(F32)).astype(BF16)], axis=0)
        b_router = jnp.pad(jnp.concatenate([b_group[l], b_expert[l]]),
                           (0, ROUTE_ROWS - N_GROUPS - N_EXPERTS))[:, None]
        x2d, hm, route, route_t, counts = _post_mixer(
            y_a, y_b, ga, gb, x2d, w_branch_a[l].astype(BF16), w_branch_b[l].astype(BF16),
            w_out[l].astype(BF16), g_cross[l][None], w_cq[l].astype(BF16), kc, vc,
            w_co[l].astype(BF16), g_ffn[l][None], w_router, b_router,
            n_batch, seq, mem_len, tm=TOKEN_BLOCK)

        row_tok, dest, block_expert, n_used = _route(route_t, counts, n_tok)
        xs = hm[row_tok]
        ys = _experts(block_expert, n_used, xs, w_exp_gate[l], w_exp_up[l], w_exp_down[l])
        last = l + 1 == depth
        x2d = _final(x2d, ys[dest.reshape(TOP_K * n_tok)], route, g_final[None], tm=LIGHT_BLOCK,
                     normalize=last)
    return x2d.reshape(n_batch, seq, d)
```

```python
import functools
import math

import numpy as np
import jax
import jax.numpy as jnp
from jax import lax
from jax.experimental import pallas as pl
from jax.experimental.pallas import tpu as pltpu

HEAD_DIM = 64
LANES = 128
F32_ROWS = 8
BF16_ROWS = 16
FOX_HEADS = 8
DIFF_HEADS = 4
CROSS_HEADS = 4
N_GROUPS = 4
EXPERTS_PER_GROUP = 8
N_EXPERTS = N_GROUPS * EXPERTS_PER_GROUP
TOP_K = 2
MOE_BLOCK = 512
TOKEN_BLOCK = 512
LIGHT_BLOCK = 1024
SLOT_BLOCK = 8192
ATTN_Q_BLOCK = 1024
ATTN_K_BLOCK = 256
NORM_EPS = 1e-6
SUBLN_EPS = 1e-5
QK_SCALE = HEAD_DIM ** -0.5
LOG2E = math.log2(math.e)
Q_PRESCALE = QK_SCALE * LOG2E
NEG_BIG = -1e30
VMEM_LIMIT = 56 * 2**20

BF16 = jnp.bfloat16
F32 = jnp.float32


def _rms(t, g, eps):
    return t * lax.rsqrt(jnp.mean(t * t, axis=-1, keepdims=True) + eps) * g


def _sigmoid(t):
    return 0.5 + 0.5 * jnp.tanh(0.5 * t)


def _dot(a, b):
    return jnp.dot(a, b, preferred_element_type=F32)


def _dot_nt(a, b):
    return lax.dot_general(a, b, (((1,), (1,)), ((), ())), preferred_element_type=F32)


def _params(*sem):
    return pltpu.CompilerParams(dimension_semantics=sem, vmem_limit_bytes=VMEM_LIMIT)


def _mem_kv_kernel(mem_ref, g_ref, w_ref, k_ref, v_ref):
    h = _rms(mem_ref[...], g_ref[...], NORM_EPS).astype(BF16)
    kv = _dot(h, w_ref[...])
    width = k_ref.shape[1]
    k_ref[...] = kv[:, :width].astype(BF16)
    v_ref[...] = kv[:, width:].astype(BF16)


def _mem_kv(mem2, g_mem, w_ckv, n_batch, mem_len):
    d = mem2.shape[1]
    cw = w_ckv.shape[1] // 2
    out = jax.ShapeDtypeStruct((n_batch * mem_len, cw), BF16)
    return pl.pallas_call(
        _mem_kv_kernel,
        grid=(n_batch,),
        in_specs=[pl.BlockSpec((mem_len, d), lambda b: (b, 0)),
                  pl.BlockSpec((1, d), lambda b: (0, 0)),
                  pl.BlockSpec((d, 2 * cw), lambda b: (0, 0))],
        out_specs=[pl.BlockSpec((mem_len, cw), lambda b: (b, 0))] * 2,
        out_shape=[out, out],
        compiler_params=_params("arbitrary"),
        name="mem_kv",
    )(mem2, g_mem, w_ckv)


def _rope(t, cos, sin_signed, first_half):
    fwd = pltpu.roll(t, LANES - HEAD_DIM // 2, axis=1)
    bwd = pltpu.roll(t, HEAD_DIM // 2, axis=1)
    return t * cos + jnp.where(first_half, fwd, bwd) * sin_signed


def _in_proj_kernel(x_ref, g_ref, w_ref, wvt_ref, wf_ref, rot_ref,
                    fq_ref, fk_ref, dq_ref, dk_ref, ga_ref, gb_ref, fvt_ref, dvt_ref, fl_ref):
    h = _rms(x_ref[...], g_ref[...], NORM_EPS).astype(BF16)
    tm = h.shape[0]
    cw = fq_ref.shape[1]

    def proj(chunk):
        return _dot(h, w_ref[:, chunk * cw:(chunk + 1) * cw])

    fq_ref[...] = (proj(0) * Q_PRESCALE).astype(BF16)
    fk_ref[...] = proj(1).astype(BF16)

    lane = lax.broadcasted_iota(jnp.int32, (tm, LANES), 1)
    rot = rot_ref[...]
    quarter = HEAD_DIM // 2
    c = jnp.where(lane < quarter, rot, 0.0)
    s = jnp.where((lane >= quarter) & (lane < 2 * quarter), rot, 0.0)
    shifted = lambda t, k: pltpu.roll(t, k * quarter, axis=1)
    cos = c + shifted(c, 1) + shifted(c, 2) + shifted(c, 3)
    sin = (s + shifted(s, 2)) - (shifted(s, 1) + shifted(s, 3))
    first_half = (lane % HEAD_DIM) < HEAD_DIM // 2
    for out_ref, chunk, scale in ((dq_ref, 2, Q_PRESCALE), (dk_ref, 3, 1.0)):
        t = proj(chunk)
        for c in range(cw // LANES):
            blk = _rope(t[:, c * LANES:(c + 1) * LANES], cos, sin, first_half)
            out_ref[:, c * LANES:(c + 1) * LANES] = (blk * scale).astype(BF16)

    for out_ref, chunk in ((ga_ref, 4), (gb_ref, 6)):
        for c in range(2):
            out_ref[:, c * cw:(c + 1) * cw] = proj(chunk + c).astype(BF16)
    fvt_ref[...] = _dot_nt(wvt_ref[0:cw, :], h).astype(BF16)
    dvt_ref[...] = _dot_nt(wvt_ref[cw:2 * cw, :], h).astype(BF16)
    fl_ref[...] = _dot(h, wf_ref[...])


def _in_proj(x2, g_mix, w_main, w_vt, w_f, rot_t, tm):
    n_tok, d = x2.shape
    cw = 512
    tok = lambda i: (i, 0)
    tok_t = lambda i: (0, i)
    const = lambda i: (0, 0)
    o512 = jax.ShapeDtypeStruct((n_tok, cw), BF16)
    o1024 = jax.ShapeDtypeStruct((n_tok, 2 * cw), BF16)
    o512_t = jax.ShapeDtypeStruct((cw, n_tok), BF16)
    return pl.pallas_call(
        _in_proj_kernel,
        grid=(n_tok // tm,),
        in_specs=[pl.BlockSpec((tm, d), tok),
                  pl.BlockSpec((1, d), const),
                  pl.BlockSpec(w_main.shape, const, pipeline_mode=pl.Buffered(1)),
                  pl.BlockSpec(w_vt.shape, const, pipeline_mode=pl.Buffered(1)),
                  pl.BlockSpec(w_f.shape, const),
                  pl.BlockSpec((tm, LANES), tok)],
        out_specs=[pl.BlockSpec((tm, cw), tok)] * 4
                  + [pl.BlockSpec((tm, 2 * cw), tok)] * 2
                  + [pl.BlockSpec((cw, tm), tok_t)] * 2
                  + [pl.BlockSpec((tm, LANES), tok)],
        out_shape=[o512] * 4 + [o1024] * 2 + [o512_t] * 2
                  + [jax.ShapeDtypeStruct((n_tok, LANES), F32)],
        compiler_params=_params("arbitrary"),
        name="in_proj",
    )(x2, g_mix, w_main, w_vt, w_f, rot_t)


HEAD_ROWS = BF16_ROWS


def _bias_lane_selectors():
    sel = np.zeros((LANES, FOX_HEADS // 2 * LANES), np.float32)
    for h in range(FOX_HEADS):
        base = (h // 2) * LANES + (HEAD_DIM if h % 2 == 0 else 0)
        for piece in range(3):
            sel[HEAD_ROWS * piece + h, base + piece] = 1.0
    return jnp.asarray(sel, BF16)


def _split3(t):
    pieces = []
    for _ in range(3):
        part = t.astype(BF16)
        t = t - part.astype(F32)
        pieces.append(part)
    return pieces


def _fox_prep_kernel(fl_ref, b_ref, tri_ref, sel_ref, out_ref, carry_ref):
    @pl.when(pl.program_id(1) == 0)
    def _():
        carry_ref[...] = jnp.zeros_like(carry_ref)

    z = fl_ref[...] + b_ref[...]
    log_f = jnp.minimum(z, 0.0) - jnp.log(1.0 + jnp.exp(-jnp.abs(z)))
    tc = z.shape[0]
    log_f_t = log_f.T[0:HEAD_ROWS, :]
    c = carry_ref[...]
    for part in _split3(log_f_t):
        c = c + _dot(part, tri_ref[...])
    carry_ref[...] = c[:, tc - 1:tc]
    pieces = [p.astype(F32) for p in _split3(c * (-LOG2E))]
    stacked = jnp.concatenate(pieces + [jnp.zeros((LANES - 3 * HEAD_ROWS, tc), F32)], axis=0)
    out_ref[...] = _dot(stacked.T.astype(BF16), sel_ref[...]).astype(BF16)


def _fox_prep(fl, b_row, sel, n_batch, seq, tc):
    n_tok = fl.shape[0]
    nblk = seq // tc
    width = sel.shape[1]
    tri = jnp.triu(jnp.ones((tc, tc), BF16))
    return pl.pallas_call(
        _fox_prep_kernel,
        grid=(n_batch, nblk),
        in_specs=[pl.BlockSpec((tc, LANES), lambda b, j: (b * nblk + j, 0)),
                  pl.BlockSpec((1, LANES), lambda b, j: (0, 0)),
                  pl.BlockSpec((tc, tc), lambda b, j: (0, 0)),
                  pl.BlockSpec(sel.shape, lambda b, j: (0, 0))],
        out_specs=pl.BlockSpec((tc, width), lambda b, j: (b * nblk + j, 0)),
        out_shape=jax.ShapeDtypeStruct((n_tok, width), BF16),
        scratch_shapes=[pltpu.VMEM((HEAD_ROWS, 1), F32)],
        compiler_params=_params("arbitrary", "arbitrary"),
        name="fox_prep",
    )(fl, b_row, tri, sel)


def _attn_kernel(*refs, bq, bk, fox, lambda_init):
    for qi in range(refs[0].shape[0] // bq):
        _attn_q_block(qi, refs, bq, bk, fox, lambda_init)


def _attn_q_block(qi, refs, bq, bk, fox, lambda_init):
    if fox:
        q_ref, k_ref, vt_ref, bl_ref, out_ref, qs_ref, st_ref, p_ref, acc_ref, stat_ref, ka_ref, kb_ref = refs
    else:
        q_ref, k_ref, vt_ref, lam_ref, gsub_ref, out_ref, qs_ref, st_ref, p_ref, acc_ref, stat_ref = refs
    q = q_ref[qi * bq:(qi + 1) * bq, :].astype(F32)
    lane = lax.broadcasted_iota(jnp.int32, q.shape, 1)
    qs_ref = qs_ref.at[qi % 2]
    if fox:
        if qi == 0:
            k = k_ref[...].astype(F32)
            bl = bl_ref[...].astype(F32)
            klane = lax.broadcasted_iota(jnp.int32, k.shape, 1)
            ka_ref[...] = jnp.where(klane < HEAD_DIM, k, bl).astype(BF16)
            kb_ref[...] = jnp.where(klane >= HEAD_DIM, k, bl).astype(BF16)

        q_streams = (jnp.where(lane < HEAD_DIM, q, jnp.where(lane < HEAD_DIM + 3, 1.0, 0.0)),
                     jnp.where(lane >= HEAD_DIM, q, jnp.where(lane < 3, 1.0, 0.0)))
        k_srcs = (ka_ref, kb_ref)
        v_rows = ((0, HEAD_DIM), (HEAD_DIM, 2 * HEAD_DIM))
    else:
        q_streams = (jnp.where(lane < HEAD_DIM, q, 0.0), jnp.where(lane >= HEAD_DIM, q, 0.0))
        k_srcs = (k_ref, k_ref)
        v_rows = ((0, 2 * HEAD_DIM), (0, 2 * HEAD_DIM))
    n_v = v_rows[0][1] - v_rows[0][0]
    ring = bq // bk
    assert ring * bk == bq and ring % 2 == 0 and ring >= 4
    for idx in range(2):
        qs_ref[idx] = q_streams[idx].astype(BF16)
        p_ref[idx, 1] = jnp.zeros((bk, bq), BF16)
        acc_ref[idx] = jnp.zeros((n_v + BF16_ROWS, bq), F32)
    ones_rows = jnp.ones((BF16_ROWS, bk), BF16)
    first_diag = ring * qi

    def key_start(kj):
        return kj * bk if isinstance(kj, int) else pl.multiple_of(kj * bk, bk)

    def qk_stage(kj, idx, slot, q_lo=0):
        k0 = key_start(kj)
        st_ref[idx, slot, :, q_lo:] = _dot_nt(k_srcs[idx][pl.ds(k0, bk), :], qs_ref[idx, q_lo:, :])

    M_ROW, A_ROW = 0, F32_ROWS

    def pv_stage(kj, idx, pslot, q_lo=0):
        k0 = key_start(kj)
        r0, r1 = v_rows[idx]
        v_aug = jnp.concatenate([vt_ref[r0:r1, pl.ds(k0, bk)], ones_rows], axis=0)
        a_row = A_ROW + F32_ROWS * pslot
        alpha = stat_ref[idx, a_row:a_row + 1, q_lo:]
        acc_ref[idx, :, q_lo:] = (alpha * acc_ref[idx, :, q_lo:]
                                  + _dot(v_aug, p_ref[idx, pslot, :, q_lo:]))

    def softmax_stage(idx, slot, pslot, masked, q_lo=0):
        st = st_ref[idx, slot, :, q_lo:]
        if masked:
            key_idx = lax.broadcasted_iota(jnp.int32, st.shape, 0)
            qry_idx = lax.broadcasted_iota(jnp.int32, st.shape, 1)
            st = jnp.where(key_idx <= qry_idx, st, NEG_BIG)
        m = stat_ref[idx, M_ROW:M_ROW + 1, q_lo:]
        m_new = jnp.maximum(m, jnp.max(st, axis=0, keepdims=True))
        p_ref[idx, pslot, :, q_lo:] = jnp.exp2(st - m_new).astype(BF16)
        stat_ref[idx, M_ROW:M_ROW + 1, q_lo:] = m_new
        a_row = A_ROW + F32_ROWS * pslot
        stat_ref[idx, a_row:a_row + 1, q_lo:] = jnp.exp2(m - m_new)

    for idx in range(2):
        stat_ref[idx, M_ROW:M_ROW + F32_ROWS, :] = jnp.full((F32_ROWS, bq), NEG_BIG, F32)
        stat_ref[idx, A_ROW:A_ROW + 2 * F32_ROWS, :] = jnp.ones((2 * F32_ROWS, bq), F32)
        qk_stage(0, idx, 0)
        qk_stage(1, idx, 1)

    def block_group(j, diag_group):
        for s in range(ring):
            lo_prev = (s - 1) * bk if diag_group and s > 0 else 0
            lo_this = s * bk if diag_group else 0
            for idx in range(2):
                if not diag_group:
                    qk_stage(j + s + 2, idx, (s + 2) % ring)
                elif s + 2 < ring:
                    qk_stage(j + s + 2, idx, s + 2, (s + 2) * bk)
                prev = j + s - 1
                prev = max(prev, 0) if isinstance(prev, int) else jnp.maximum(prev, 0)
                pv_stage(prev, idx, (s - 1) % 2, lo_prev)
                softmax_stage(idx, s, s % 2, diag_group, lo_this)

    def body(i, carry):
        block_group(ring * i, False)
        return carry

    if qi > 0:
        lax.fori_loop(0, qi, body, 0)
    block_group(first_diag, True)
    outs = []
    for idx in range(2):
        pv_stage(first_diag + ring - 1, idx, (ring - 1) % 2, (ring - 1) * bk)
        acc = acc_ref[idx]
        outs.append(acc[0:n_v] * (1.0 / acc[n_v:n_v + 1]))
    o_a, o_b = outs
    if fox:
        y = jnp.concatenate([o_a, o_b], axis=0)
    else:
        lam_vec = lam_ref[...]
        lam = (jnp.exp(jnp.sum(lam_vec[0:1] * lam_vec[1:2], axis=1, keepdims=True))
               - jnp.exp(jnp.sum(lam_vec[2:3] * lam_vec[3:4], axis=1, keepdims=True))
               + lambda_init)
        y = o_a - lam * o_b
        y = y * lax.rsqrt(jnp.mean(y * y, axis=0, keepdims=True) + SUBLN_EPS)
        y = y * (gsub_ref[...] * (1.0 - lambda_init))
    out_ref[qi * bq:(qi + 1) * bq, :] = y.T.astype(out_ref.dtype)


def _attention(q, k, vt, extra, n_batch, seq, fox, lambda_init=0.0):
    n_tok, width = q.shape
    n_pairs = width // LANES
    bq, bk = ATTN_Q_BLOCK, ATTN_K_BLOCK
    n_v = HEAD_DIM if fox else 2 * HEAD_DIM
    k_spec = pl.BlockSpec((seq, LANES), lambda b, hp: (b, hp))
    q_spec = k_spec
    vt_spec = pl.BlockSpec((LANES, seq), lambda b, hp: (hp, b))
    scratch = [pltpu.VMEM((2, 2, bq, LANES), BF16),
               pltpu.VMEM((2, bq // bk, bk, bq), F32),
               pltpu.VMEM((2, 2, bk, bq), BF16),
               pltpu.VMEM((2, n_v + BF16_ROWS, bq), F32),
               pltpu.VMEM((2, 3 * F32_ROWS, bq), F32)]
    if fox:
        extra_specs = [k_spec]
        scratch += [pltpu.VMEM((seq, LANES), BF16)] * 2
    else:
        lam_vec, g_sub = extra
        extra_specs = [pl.BlockSpec(lam_vec.shape, lambda b, hp: (0, 0)),
                       pl.BlockSpec(g_sub.shape, lambda b, hp: (0, 0))]
    return pl.pallas_call(
        functools.partial(_attn_kernel, bq=bq, bk=bk, fox=fox, lambda_init=lambda_init),
        grid=(n_batch, n_pairs),
        in_specs=[q_spec, k_spec, vt_spec] + extra_specs,
        out_specs=q_spec,
        out_shape=jax.ShapeDtypeStruct((n_tok, width), BF16),
        scratch_shapes=scratch,
        compiler_params=_params("arbitrary", "arbitrary"),
        name="fox_attn" if fox else "diff_attn",
    )(q, k, vt, *extra)


ROUTE_ROWS = 48


def _route_block(logits_t, tri, cnt_ref):
    rows, n = logits_t.shape
    row_f = lax.broadcasted_iota(jnp.int32, (rows, n), 0).astype(F32)

    def first_argmax(t):
        top = jnp.max(t, axis=0, keepdims=True)
        return top, jnp.min(jnp.where(t == top, row_f, float(rows)), axis=0, keepdims=True)

    is_group = row_f < N_GROUPS
    g_max, g_sel = first_argmax(jnp.where(is_group, logits_t, NEG_BIG))
    g_w = 1.0 / jnp.sum(jnp.where(is_group, jnp.exp(logits_t - g_max), 0.0), axis=0, keepdims=True)
    lo = N_GROUPS + EXPERTS_PER_GROUP * g_sel
    scores = jnp.where((row_f >= lo) & (row_f < lo + EXPERTS_PER_GROUP), logits_t, NEG_BIG)
    v1, i1 = first_argmax(scores)
    v2, i2 = first_argmax(jnp.where(row_f == i1, NEG_BIG, scores))
    t = jnp.exp(v2 - v1)
    w1 = g_w / (1.0 + t)
    w2 = w1 * t

    hit1 = row_f == i1
    hit2 = row_f == i2
    onehot = jnp.where(hit1 | hit2, 1.0, 0.0)
    before = _dot(onehot.astype(BF16), tri) + cnt_ref[...]
    r1 = jnp.sum(jnp.where(hit1, before, 0.0), axis=0, keepdims=True)
    r2 = jnp.sum(jnp.where(hit2, before, 0.0), axis=0, keepdims=True)
    cnt_ref[...] = before[:, n - 1:n] + onehot[:, n - 1:n]
    out_row = lax.broadcasted_iota(jnp.int32, (8, n), 0)
    out = jnp.zeros((8, n), F32)
    for pos, val in enumerate((i1 - N_GROUPS, i2 - N_GROUPS, r1, r2, w1, w2)):
        out = jnp.where(out_row == pos, val, out)
    return out


def _post_mixer_kernel(ya_ref, yb_ref, ga_ref, gb_ref, x_ref, wa_ref, wb_ref, wo_ref,
                       gc_ref, wcq_ref, kc_ref, vc_ref, wco_ref, gf_ref, wr_ref, br_ref, tri_ref,
                       x2_ref, hm_ref, route_ref, route_t_ref, cnt_out_ref, cnt_ref):
    @pl.when((pl.program_id(0) == 0) & (pl.program_id(1) == 0))
    def _():
        cnt_ref[...] = jnp.zeros_like(cnt_ref)

    merged = (_sigmoid(ga_ref[...].astype(F32)) * _dot(ya_ref[...], wa_ref[...])
              + _sigmoid(gb_ref[...].astype(F32)) * _dot(yb_ref[...], wb_ref[...]))
    x1 = x_ref[...] + _dot(merged.astype(BF16), wo_ref[...])

    hx = _rms(x1, gc_ref[...], NORM_EPS).astype(BF16)
    qc = (_dot(hx, wcq_ref[...]) * QK_SCALE).astype(BF16)
    kc = kc_ref[...]
    vc = vc_ref[...]
    lane = lax.broadcasted_iota(jnp.int32, qc.shape, 1)
    zero = jnp.zeros_like(qc)
    o = jnp.zeros(qc.shape, F32)
    for h in range(CROSS_HEADS):
        in_head = (lane >= h * HEAD_DIM) & (lane < (h + 1) * HEAD_DIM)
        s = _dot_nt(jnp.where(in_head, qc, zero), kc)
        p = jnp.exp(s - jnp.max(s, axis=1, keepdims=True))
        p = p * (1.0 / jnp.sum(p, axis=1, keepdims=True))
        o = jnp.where(in_head, _dot(p.astype(BF16), vc), o)
    x2 = x1 + _dot(o.astype(BF16), wco_ref[...])
    x2_ref[...] = x2

    hm = _rms(x2, gf_ref[...], NORM_EPS)
    hm_ref[...] = hm.astype(BF16)
    hm_hi = hm.astype(BF16)
    hm_lo = (hm - hm_hi.astype(F32)).astype(BF16)
    both = _dot_nt(wr_ref[...], hm_hi)
    logits_t = (both[0:ROUTE_ROWS] + both[ROUTE_ROWS:2 * ROUTE_ROWS]
                + _dot_nt(wr_ref[0:ROUTE_ROWS, :], hm_lo) + br_ref[...])
    route_t = _route_block(logits_t, tri_ref[...], cnt_ref)
    route_t_ref[...] = route_t
    pad_rows = jnp.zeros((LANES - route_t.shape[0], route_t.shape[1]), F32)
    route_ref[...] = jnp.concatenate([route_t, pad_rows], axis=0).T
    cnt_out_ref[...] = jnp.broadcast_to(cnt_ref[...], cnt_out_ref.shape)


def _post_mixer(ya, yb, ga, gb, x2d, wa, wb, wo, g_cross, wcq, kc, vc, wco, g_ffn, wr, br,
                n_batch, seq, mem_len, tm):
    n_tok, d = x2d.shape
    nblk = seq // tm
    tok = lambda b, j: (b * nblk + j, 0)
    const = lambda b, j: (0, 0)
    full = lambda a: pl.BlockSpec(a.shape, const)
    tri = jnp.triu(jnp.ones((tm, tm), BF16), 1)
    return pl.pallas_call(
        _post_mixer_kernel,
        grid=(n_batch, nblk),
        in_specs=[pl.BlockSpec((tm, ya.shape[1]), tok), pl.BlockSpec((tm, yb.shape[1]), tok),
                  pl.BlockSpec((tm, d), tok), pl.BlockSpec((tm, d), tok), pl.BlockSpec((tm, d), tok),
                  full(wa), full(wb), full(wo), full(g_cross), full(wcq),
                  pl.BlockSpec((mem_len, kc.shape[1]), lambda b, j: (b, 0)),
                  pl.BlockSpec((mem_len, vc.shape[1]), lambda b, j: (b, 0)),
                  full(wco), full(g_ffn), full(wr), full(br),
                  full(tri)],
        out_specs=[pl.BlockSpec((tm, d), tok), pl.BlockSpec((tm, d), tok),
                   pl.BlockSpec((tm, LANES), tok),
                   pl.BlockSpec((8, tm), lambda b, j: (0, b * nblk + j)),
                   pl.BlockSpec((ROUTE_ROWS, LANES), const)],
        out_shape=[jax.ShapeDtypeStruct((n_tok, d), F32),
                   jax.ShapeDtypeStruct((n_tok, d), BF16),
                   jax.ShapeDtypeStruct((n_tok, LANES), F32),
                   jax.ShapeDtypeStruct((8, n_tok), F32),
                   jax.ShapeDtypeStruct((ROUTE_ROWS, LANES), F32)],
        scratch_shapes=[pltpu.VMEM((ROUTE_ROWS, 1), F32)],
        compiler_params=_params("arbitrary", "arbitrary"),
        name="post_mixer",
    )(ya, yb, ga, gb, x2d, wa, wb, wo, g_cross, wcq, kc, vc, wco, g_ffn, wr, br, tri)


def _expert_kernel(be_ref, nused_ref, xs_ref, wg_ref, wu_ref, wd_ref, ys_ref, wg_s, wu_s, wd_s):
    i = pl.program_id(0)

    @pl.when((i == 0) | (be_ref[i] != be_ref[jnp.maximum(i - 1, 0)]))
    def _():
        wg_s[...] = wg_ref[0].astype(BF16)
        wu_s[...] = wu_ref[0].astype(BF16)
        wd_s[...] = wd_ref[0].astype(BF16)

    @pl.when(i < nused_ref[0])
    def _():
        xb = xs_ref[...]
        g = _dot(xb, wg_s[...])
        u = _dot(xb, wu_s[...])
        hb = (g * _sigmoid(g) * u).astype(BF16)
        ys_ref[...] = _dot(hb, wd_s[...]).astype(ys_ref.dtype)

    @pl.when(i >= nused_ref[0])
    def _():
        ys_ref[...] = jnp.zeros_like(ys_ref)


def _experts(block_expert, n_used, xs, wg, wu, wd):
    n_rows, d = xs.shape
    n_blocks = n_rows // MOE_BLOCK
    ff = wg.shape[2]
    grid_spec = pltpu.PrefetchScalarGridSpec(
        num_scalar_prefetch=2,
        grid=(n_blocks,),
        in_specs=[pl.BlockSpec((MOE_BLOCK, d), lambda i, be, nu: (i, 0)),
                  pl.BlockSpec((1, d, ff), lambda i, be, nu: (be[i], 0, 0)),
                  pl.BlockSpec((1, d, ff), lambda i, be, nu: (be[i], 0, 0)),
                  pl.BlockSpec((1, ff, d), lambda i, be, nu: (be[i], 0, 0))],
        out_specs=pl.BlockSpec((MOE_BLOCK, d), lambda i, be, nu: (i, 0)),
        scratch_shapes=[pltpu.VMEM((d, ff), BF16), pltpu.VMEM((d, ff), BF16),
                        pltpu.VMEM((ff, d), BF16)],
    )
    return pl.pallas_call(
        _expert_kernel,
        grid_spec=grid_spec,
        out_shape=jax.ShapeDtypeStruct((n_rows, d), BF16),
        compiler_params=_params("arbitrary"),
        name="experts",
    )(block_expert, n_used, xs, wg, wu, wd)


def _final_kernel(x_ref, y0_ref, y1_ref, route_ref, g_ref, out_ref, *, normalize):
    route = route_ref[...]
    w0 = route[:, 2 * TOP_K:2 * TOP_K + 1]
    w1 = route[:, 2 * TOP_K + 1:2 * TOP_K + 2]
    out = x_ref[...] + w0 * y0_ref[...].astype(F32) + w1 * y1_ref[...].astype(F32)
    out_ref[...] = _rms(out, g_ref[...], NORM_EPS) if normalize else out


def _final(x2, y_both, route, g_final, tm, normalize):
    n_tok, d = x2.shape
    nblk = n_tok // tm
    tok = lambda i: (i, 0)
    return pl.pallas_call(
        functools.partial(_final_kernel, normalize=normalize),
        grid=(nblk,),
        in_specs=[pl.BlockSpec((tm, d), tok), pl.BlockSpec((tm, d), tok),
                  pl.BlockSpec((tm, d), lambda i: (i + nblk, 0)),
                  pl.BlockSpec((tm, LANES), tok), pl.BlockSpec((1, d), lambda i: (0, 0))],
        out_specs=pl.BlockSpec((tm, d), tok),
        out_shape=jax.ShapeDtypeStruct((n_tok, d), F32),
        compiler_params=_params("arbitrary"),
        name="final_norm",
    )(x2, y_both, y_both, route, g_final)


def _slot_kernel(tab_ref, rt_ref, dest_ref, key_ref):
    e = rt_ref[0:TOP_K, :]
    rank = rt_ref[TOP_K:2 * TOP_K, :].astype(jnp.int32)
    pad = jnp.zeros(e.shape, jnp.int32)
    seg = jnp.zeros(e.shape, jnp.int32)
    for k in range(N_EXPERTS):
        hit = e == float(k)
        pad = jnp.where(hit, tab_ref[k], pad)
        seg = jnp.where(hit, tab_ref[N_EXPERTS + k], seg)
    dest_ref[...] = pad + rank
    key_ref[...] = seg + rank


def _slots(tables, route_t, tm):
    n_tok = route_t.shape[1]
    out = jax.ShapeDtypeStruct((TOP_K, n_tok), jnp.int32)
    grid_spec = pltpu.PrefetchScalarGridSpec(
        num_scalar_prefetch=1,
        grid=(n_tok // tm,),
        in_specs=[pl.BlockSpec((route_t.shape[0], tm), lambda i, tab: (0, i))],
        out_specs=[pl.BlockSpec((TOP_K, tm), lambda i, tab: (0, i))] * 2,
    )
    return pl.pallas_call(_slot_kernel, grid_spec=grid_spec, out_shape=[out, out],
                          compiler_params=_params("arbitrary"), name="moe_slots")(tables, route_t)


def _route(route_t, counts, n_tok):
    n_assign = n_tok * TOP_K
    counts = counts[N_GROUPS:N_GROUPS + N_EXPERTS, 0].astype(jnp.int32)
    seg_start = jnp.cumsum(counts) - counts
    padded = (counts + MOE_BLOCK - 1) // MOE_BLOCK * MOE_BLOCK
    pad_end = jnp.cumsum(padded)
    pad_start = pad_end - padded
    dest, key = _slots(jnp.concatenate([pad_start, seg_start]), route_t, tm=min(SLOT_BLOCK, n_tok))
    order = jnp.argsort(key.reshape(n_assign)).astype(jnp.int32)

    n_blocks = -(-n_assign // MOE_BLOCK) + N_EXPERTS
    n_rows = n_blocks * MOE_BLOCK
    block_start = jnp.arange(n_blocks, dtype=jnp.int32) * MOE_BLOCK
    block_expert = jnp.minimum(jnp.sum(block_start[:, None] >= pad_end[None, :], axis=1),
                               N_EXPERTS - 1).astype(jnp.int32)
    onehot = block_expert[:, None] == jnp.arange(N_EXPERTS, dtype=jnp.int32)[None, :]

    def of_block(table):
        return jnp.sum(jnp.where(onehot, table[None, :], 0), axis=1)

    offset = block_start - of_block(pad_start)
    in_block = jnp.arange(MOE_BLOCK, dtype=jnp.int32)[None, :]
    valid = in_block < (of_block(counts) - offset)[:, None]
    src = jnp.clip((of_block(seg_start) + offset)[:, None] + in_block, 0, n_assign - 1)
    rows = block_start[:, None] + in_block
    row_tok = jnp.where(valid, order[src] % n_tok, rows % n_tok).reshape(n_rows)
    n_used = (pad_end[-1] // MOE_BLOCK).astype(jnp.int32).reshape(1)
    return row_tok, dest, block_expert, n_used


def kernel(x, mem, positions, g_mix, w_in, b_fgate, w_branch_a, w_branch_b, w_out, lambda_q1, lambda_k1, lambda_q2, lambda_k2, g_diff_sub, g_cross, g_mem, w_cq, w_ckv, w_co, g_ffn, w_group, b_group, w_expert, b_expert, w_exp_gate, w_exp_up, w_exp_down, g_final):
    n_batch, seq, d = x.shape
    mem_len = mem.shape[1]
    depth = g_mix.shape[0]
    n_tok = n_batch * seq
    fox_w = FOX_HEADS * HEAD_DIM
    diff_w = DIFF_HEADS * 2 * HEAD_DIM

    half = HEAD_DIM // 2
    inv_freq = 10000.0 ** (-jnp.arange(half, dtype=F32) * 2.0 / HEAD_DIM)
    ang = positions.astype(F32).reshape(n_tok, 1) * inv_freq[None, :]
    rot_t = jnp.concatenate([jnp.cos(ang), jnp.sin(ang), jnp.zeros((n_tok, LANES - 2 * half), F32)],
                            axis=1)
    sel = _bias_lane_selectors()

    x2d = x.reshape(n_tok, d)
    mem2d = mem.reshape(n_batch * mem_len, d)
    for l in range(depth):
        lambda_init = 0.8 - 0.6 * math.exp(-0.3 * l)
        o_fv = 2 * fox_w
        o_fl = o_fv + fox_w
        o_dq = o_fl + FOX_HEADS
        o_dv = o_dq + 2 * diff_w
        o_ga = o_dv + diff_w
        wl = w_in[l]
        w_main = jnp.concatenate([wl[:, :o_fv], wl[:, o_dq:o_dv], wl[:, o_ga:]], axis=1).astype(BF16)
        w_vt = jnp.concatenate([wl[:, o_fv:o_fl], wl[:, o_dv:o_ga]], axis=1).T.astype(BF16)
        w_f = jnp.pad(wl[:, o_fl:o_dq], ((0, 0), (0, LANES - FOX_HEADS))).astype(BF16)
        b_row = jnp.pad(b_fgate[l], (0, LANES - FOX_HEADS))[None]

        kc, vc = _mem_kv(mem2d, g_mem[l][None], w_ckv[l].astype(BF16), n_batch, mem_len)
        fq, fk, dq, dk, ga, gb, fvt, dvt, fl = _in_proj(
            x2d, g_mix[l][None], w_main, w_vt, w_f, rot_t, tm=TOKEN_BLOCK)
        bias_lanes = _fox_prep(fl, b_row, sel, n_batch, seq, tc=LIGHT_BLOCK)
        y_a = _attention(fq, fk, fvt, (bias_lanes,), n_batch, seq, fox=True)
        lam_vec = jnp.stack([lambda_q1[l], lambda_k1[l], lambda_q2[l], lambda_k2[l]])
        y_b = _attention(dq, dk, dvt, (lam_vec, g_diff_sub[l][:, None]), n_batch, seq,
                         fox=False, lambda_init=lambda_init)

        w_router = jnp.pad(jnp.concatenate([w_group[l], w_expert[l]], axis=1).T,
                           ((0, ROUTE_ROWS - N_GROUPS - N_EXPERTS), (0, 0)))
        w_router_hi = w_router.astype(BF16)
        w_router = jnp.concatenate(
            [w_router_hi, (w_router - w_router_hi.astype(F32)).astype(BF16)], axis=0)
        b_router = jnp.pad(jnp.concatenate([b_group[l], b_expert[l]]),
                           (0, ROUTE_ROWS - N_GROUPS - N_EXPERTS))[:, None]
        x2d, hm, route, route_t, counts = _post_mixer(
            y_a, y_b, ga, gb, x2d, w_branch_a[l].astype(BF16), w_branch_b[l].astype(BF16),
            w_out[l].astype(BF16), g_cross[l][None], w_cq[l].astype(BF16), kc, vc,
            w_co[l].astype(BF16), g_ffn[l][None], w_router, b_router,
            n_batch, seq, mem_len, tm=TOKEN_BLOCK)

        row_tok, dest, block_expert, n_used = _route(route_t, counts, n_tok)
        xs = hm[row_tok]
        ys = _experts(block_expert, n_used, xs, w_exp_gate[l], w_exp_up[l], w_exp_down[l])
        last = l + 1 == depth
        x2d = _final(x2d, ys[dest.reshape(TOP_K * n_tok)], route, g_final[None], tm=LIGHT_BLOCK,
                     normalize=last)
    return x2d.reshape(n_batch, seq, d)
```

```python
import functools
import math

import numpy as np
import jax
import jax.numpy as jnp
from jax import lax
from jax.experimental import pallas as pl
from jax.experimental.pallas import tpu as pltpu

HEAD_DIM = 64
LANES = 128
F32_ROWS = 8
BF16_ROWS = 16
FOX_HEADS = 8
DIFF_HEADS = 4
CROSS_HEADS = 4
N_GROUPS = 4
EXPERTS_PER_GROUP = 8
N_EXPERTS = N_GROUPS * EXPERTS_PER_GROUP
TOP_K = 2
MOE_BLOCK = 512
TOKEN_BLOCK = 512
LIGHT_BLOCK = 1024
SLOT_BLOCK = 8192
ATTN_Q_BLOCK = 1024
ATTN_K_BLOCK = 256
NORM_EPS = 1e-6
SUBLN_EPS = 1e-5
QK_SCALE = HEAD_DIM ** -0.5
LOG2E = math.log2(math.e)
Q_PRESCALE = QK_SCALE * LOG2E
NEG_BIG = -1e30
VMEM_LIMIT = 56 * 2**20

BF16 = jnp.bfloat16
F32 = jnp.float32


def _rms(t, g, eps):
    return t * lax.rsqrt(jnp.mean(t * t, axis=-1, keepdims=True) + eps) * g


def _sigmoid(t):
    return 0.5 + 0.5 * jnp.tanh(0.5 * t)


def _dot(a, b):
    return jnp.dot(a, b, preferred_element_type=F32)


def _dot_nt(a, b):
    return lax.dot_general(a, b, (((1,), (1,)), ((), ())), preferred_element_type=F32)


def _params(*sem):
    return pltpu.CompilerParams(dimension_semantics=sem, vmem_limit_bytes=VMEM_LIMIT)


def _mem_kv_kernel(mem_ref, g_ref, w_ref, k_ref, v_ref):
    h = _rms(mem_ref[...], g_ref[...], NORM_EPS).astype(BF16)
    kv = _dot(h, w_ref[...])
    width = k_ref.shape[1]
    k_ref[...] = kv[:, :width].astype(BF16)
    v_ref[...] = kv[:, width:].astype(BF16)


def _mem_kv(mem2, g_mem, w_ckv, n_batch, mem_len):
    d = mem2.shape[1]
    cw = w_ckv.shape[1] // 2
    out = jax.ShapeDtypeStruct((n_batch * mem_len, cw), BF16)
    return pl.pallas_call(
        _mem_kv_kernel,
        grid=(n_batch,),
        in_specs=[pl.BlockSpec((mem_len, d), lambda b: (b, 0)),
                  pl.BlockSpec((1, d), lambda b: (0, 0)),
                  pl.BlockSpec((d, 2 * cw), lambda b: (0, 0))],
        out_specs=[pl.BlockSpec((mem_len, cw), lambda b: (b, 0))] * 2,
        out_shape=[out, out],
        compiler_params=_params("arbitrary"),
        name="mem_kv",
    )(mem2, g_mem, w_ckv)


def _rope(t, cos, sin_signed, first_half):
    fwd = pltpu.roll(t, LANES - HEAD_DIM // 2, axis=1)
    bwd = pltpu.roll(t, HEAD_DIM // 2, axis=1)
    return t * cos + jnp.where(first_half, fwd, bwd) * sin_signed


def _in_proj_kernel(x_ref, g_ref, w_ref, wvt_ref, wf_ref, rot_ref,
                    fq_ref, fk_ref, dq_ref, dk_ref, ga_ref, gb_ref, fvt_ref, dvt_ref, fl_ref):
    h = _rms(x_ref[...], g_ref[...], NORM_EPS).astype(BF16)
    tm = h.shape[0]
    cw = fq_ref.shape[1]

    def proj(chunk):
        return _dot(h, w_ref[:, chunk * cw:(chunk + 1) * cw])

    fq_ref[...] = (proj(0) * Q_PRESCALE).astype(BF16)
    fk_ref[...] = proj(1).astype(BF16)

    lane = lax.broadcasted_iota(jnp.int32, (tm, LANES), 1)
    rot = rot_ref[...]
    quarter = HEAD_DIM // 2
    c = jnp.where(lane < quarter, rot, 0.0)
    s = jnp.where((lane >= quarter) & (lane < 2 * quarter), rot, 0.0)
    shifted = lambda t, k: pltpu.roll(t, k * quarter, axis=1)
    cos = c + shifted(c, 1) + shifted(c, 2) + shifted(c, 3)
    sin = (s + shifted(s, 2)) - (shifted(s, 1) + shifted(s, 3))
    first_half = (lane % HEAD_DIM) < HEAD_DIM // 2
    for out_ref, chunk, scale in ((dq_ref, 2, Q_PRESCALE), (dk_ref, 3, 1.0)):
        t = proj(chunk)
        for c in range(cw // LANES):
            blk = _rope(t[:, c * LANES:(c + 1) * LANES], cos, sin, first_half)
            out_ref[:, c * LANES:(c + 1) * LANES] = (blk * scale).astype(BF16)

    for out_ref, chunk in ((ga_ref, 4), (gb_ref, 6)):
        for c in range(2):
            out_ref[:, c * cw:(c + 1) * cw] = proj(chunk + c).astype(BF16)
    fvt_ref[...] = _dot_nt(wvt_ref[0:cw, :], h).astype(BF16)
    dvt_ref[...] = _dot_nt(wvt_ref[cw:2 * cw, :], h).astype(BF16)
    fl_ref[...] = _dot(h, wf_ref[...])


def _in_proj(x2, g_mix, w_main, w_vt, w_f, rot_t, tm):
    n_tok, d = x2.shape
    cw = 512
    tok = lambda i: (i, 0)
    tok_t = lambda i: (0, i)
    const = lambda i: (0, 0)
    o512 = jax.ShapeDtypeStruct((n_tok, cw), BF16)
    o1024 = jax.ShapeDtypeStruct((n_tok, 2 * cw), BF16)
    o512_t = jax.ShapeDtypeStruct((cw, n_tok), BF16)
    return pl.pallas_call(
        _in_proj_kernel,
        grid=(n_tok // tm,),
        in_specs=[pl.BlockSpec((tm, d), tok),
                  pl.BlockSpec((1, d), const),
                  pl.BlockSpec(w_main.shape, const, pipeline_mode=pl.Buffered(1)),
                  pl.BlockSpec(w_vt.shape, const, pipeline_mode=pl.Buffered(1)),
                  pl.BlockSpec(w_f.shape, const),
                  pl.BlockSpec((tm, LANES), tok)],
        out_specs=[pl.BlockSpec((tm, cw), tok)] * 4
                  + [pl.BlockSpec((tm, 2 * cw), tok)] * 2
                  + [pl.BlockSpec((cw, tm), tok_t)] * 2
                  + [pl.BlockSpec((tm, LANES), tok)],
        out_shape=[o512] * 4 + [o1024] * 2 + [o512_t] * 2
                  + [jax.ShapeDtypeStruct((n_tok, LANES), F32)],
        compiler_params=_params("arbitrary"),
        name="in_proj",
    )(x2, g_mix, w_main, w_vt, w_f, rot_t)


HEAD_ROWS = BF16_ROWS


def _bias_lane_selectors():
    sel = np.zeros((LANES, FOX_HEADS // 2 * LANES), np.float32)
    for h in range(FOX_HEADS):
        base = (h // 2) * LANES + (HEAD_DIM if h % 2 == 0 else 0)
        for piece in range(3):
            sel[HEAD_ROWS * piece + h, base + piece] = 1.0
    return jnp.asarray(sel, BF16)


def _split3(t):
    pieces = []
    for _ in range(3):
        part = t.astype(BF16)
        t = t - part.astype(F32)
        pieces.append(part)
    return pieces


def _fox_prep_kernel(fl_ref, b_ref, tri_ref, sel_ref, out_ref, carry_ref):
    @pl.when(pl.program_id(1) == 0)
    def _():
        carry_ref[...] = jnp.zeros_like(carry_ref)

    z = fl_ref[...] + b_ref[...]
    log_f = jnp.minimum(z, 0.0) - jnp.log(1.0 + jnp.exp(-jnp.abs(z)))
    tc = z.shape[0]
    log_f_t = log_f.T[0:HEAD_ROWS, :]
    c = carry_ref[...]
    for part in _split3(log_f_t):
        c = c + _dot(part, tri_ref[...])
    carry_ref[...] = c[:, tc - 1:tc]
    pieces = [p.astype(F32) for p in _split3(c * (-LOG2E))]
    stacked = jnp.concatenate(pieces + [jnp.zeros((LANES - 3 * HEAD_ROWS, tc), F32)], axis=0)
    out_ref[...] = _dot(stacked.T.astype(BF16), sel_ref[...]).astype(BF16)


def _fox_prep(fl, b_row, sel, n_batch, seq, tc):
    n_tok = fl.shape[0]
    nblk = seq // tc
    width = sel.shape[1]
    tri = jnp.triu(jnp.ones((tc, tc), BF16))
    return pl.pallas_call(
        _fox_prep_kernel,
        grid=(n_batch, nblk),
        in_specs=[pl.BlockSpec((tc, LANES), lambda b, j: (b * nblk + j, 0)),
                  pl.BlockSpec((1, LANES), lambda b, j: (0, 0)),
                  pl.BlockSpec((tc, tc), lambda b, j: (0, 0)),
                  pl.BlockSpec(sel.shape, lambda b, j: (0, 0))],
        out_specs=pl.BlockSpec((tc, width), lambda b, j: (b * nblk + j, 0)),
        out_shape=jax.ShapeDtypeStruct((n_tok, width), BF16),
        scratch_shapes=[pltpu.VMEM((HEAD_ROWS, 1), F32)],
        compiler_params=_params("arbitrary", "arbitrary"),
        name="fox_prep",
    )(fl, b_row, tri, sel)


def _attn_kernel(*refs, bq, bk, fox, lambda_init):
    for qi in range(refs[0].shape[0] // bq):
        _attn_q_block(qi, refs, bq, bk, fox, lambda_init)


def _attn_q_block(qi, refs, bq, bk, fox, lambda_init):
    if fox:
        q_ref, k_ref, vt_ref, bl_ref, out_ref, qs_ref, st_ref, p_ref, acc_ref, stat_ref, ka_ref, kb_ref = refs
    else:
        q_ref, k_ref, vt_ref, lam_ref, gsub_ref, out_ref, qs_ref, st_ref, p_ref, acc_ref, stat_ref = refs
    q = q_ref[qi * bq:(qi + 1) * bq, :].astype(F32)
    lane = lax.broadcasted_iota(jnp.int32, q.shape, 1)
    qs_ref = qs_ref.at[qi % 2]
    if fox:
        if qi == 0:
            k = k_ref[...].astype(F32)
            bl = bl_ref[...].astype(F32)
            klane = lax.broadcasted_iota(jnp.int32, k.shape, 1)
            ka_ref[...] = jnp.where(klane < HEAD_DIM, k, bl).astype(BF16)
            kb_ref[...] = jnp.where(klane >= HEAD_DIM, k, bl).astype(BF16)

        q_streams = (jnp.where(lane < HEAD_DIM, q, jnp.where(lane < HEAD_DIM + 3, 1.0, 0.0)),
                     jnp.where(lane >= HEAD_DIM, q, jnp.where(lane < 3, 1.0, 0.0)))
        k_srcs = (ka_ref, kb_ref)
        v_rows = ((0, HEAD_DIM), (HEAD_DIM, 2 * HEAD_DIM))
    else:
        q_streams = (jnp.where(lane < HEAD_DIM, q, 0.0), jnp.where(lane >= HEAD_DIM, q, 0.0))
        k_srcs = (k_ref, k_ref)
        v_rows = ((0, 2 * HEAD_DIM), (0, 2 * HEAD_DIM))
    n_v = v_rows[0][1] - v_rows[0][0]
    ring = bq // bk
    assert ring * bk == bq and ring % 2 == 0 and ring >= 4
    for idx in range(2):
        qs_ref[idx] = q_streams[idx].astype(BF16)
        p_ref[idx, 1] = jnp.zeros((bk, bq), BF16)
        acc_ref[idx] = jnp.zeros((n_v + BF16_ROWS, bq), F32)
    ones_rows = jnp.ones((BF16_ROWS, bk), BF16)
    first_diag = ring * qi

    def key_start(kj):
        return kj * bk if isinstance(kj, int) else pl.multiple_of(kj * bk, bk)

    def qk_stage(kj, idx, slot, q_lo=0):
        k0 = key_start(kj)
        st_ref[idx, slot, :, q_lo:] = _dot_nt(k_srcs[idx][pl.ds(k0, bk), :], qs_ref[idx, q_lo:, :])

    M_ROW, A_ROW = 0, F32_ROWS

    def pv_stage(kj, idx, pslot, q_lo=0):
        k0 = key_start(kj)
        r0, r1 = v_rows[idx]
        v_aug = jnp.concatenate([vt_ref[r0:r1, pl.ds(k0, bk)], ones_rows], axis=0)
        a_row = A_ROW + F32_ROWS * pslot
        alpha = stat_ref[idx, a_row:a_row + 1, q_lo:]
        acc_ref[idx, :, q_lo:] = (alpha * acc_ref[idx, :, q_lo:]
                                  + _dot(v_aug, p_ref[idx, pslot, :, q_lo:]))

    def softmax_stage(idx, slot, pslot, masked, q_lo=0):
        st = st_ref[idx, slot, :, q_lo:]
        if masked:
            key_idx = lax.broadcasted_iota(jnp.int32, st.shape, 0)
            qry_idx = lax.broadcasted_iota(jnp.int32, st.shape, 1)
            st = jnp.where(key_idx <= qry_idx, st, NEG_BIG)
        m = stat_ref[idx, M_ROW:M_ROW + 1, q_lo:]
        m_new = jnp.maximum(m, jnp.max(st, axis=0, keepdims=True))
        p_ref[idx, pslot, :, q_lo:] = jnp.exp2(st - m_new).astype(BF16)
        stat_ref[idx, M_ROW:M_ROW + 1, q_lo:] = m_new
        a_row = A_ROW + F32_ROWS * pslot
        stat_ref[idx, a_row:a_row + 1, q_lo:] = jnp.exp2(m - m_new)

    for idx in range(2):
        stat_ref[idx, M_ROW:M_ROW + F32_ROWS, :] = jnp.full((F32_ROWS, bq), NEG_BIG, F32)
        stat_ref[idx, A_ROW:A_ROW + 2 * F32_ROWS, :] = jnp.ones((2 * F32_ROWS, bq), F32)
        qk_stage(0, idx, 0)
        qk_stage(1, idx, 1)

    def block_group(j, diag_group):
        for s in range(ring):
            lo_prev = (s - 1) * bk if diag_group and s > 0 else 0
            lo_this = s * bk if diag_group else 0
            for idx in range(2):
                if not diag_group:
                    qk_stage(j + s + 2, idx, (s + 2) % ring)
                elif s + 2 < ring:
                    qk_stage(j + s + 2, idx, s + 2, (s + 2) * bk)
                prev = j + s - 1
                prev = max(prev, 0) if isinstance(prev, int) else jnp.maximum(prev, 0)
                pv_stage(prev, idx, (s - 1) % 2, lo_prev)
                softmax_stage(idx, s, s % 2, diag_group, lo_this)

    def body(i, carry):
        block_group(ring * i, False)
        return carry

    if qi > 0:
        lax.fori_loop(0, qi, body, 0)
    block_group(first_diag, True)
    outs = []
    for idx in range(2):
        pv_stage(first_diag + ring - 1, idx, (ring - 1) % 2, (ring - 1) * bk)
        acc = acc_ref[idx]
        outs.append(acc[0:n_v] * (1.0 / acc[n_v:n_v + 1]))
    o_a, o_b = outs
    if fox:
        y = jnp.concatenate([o_a, o_b], axis=0)
    else:
        lam_vec = lam_ref[...]
        lam = (jnp.exp(jnp.sum(lam_vec[0:1] * lam_vec[1:2], axis=1, keepdims=True))
               - jnp.exp(jnp.sum(lam_vec[2:3] * lam_vec[3:4], axis=1, keepdims=True))
               + lambda_init)
        y = o_a - lam * o_b
        y = y * lax.rsqrt(jnp.mean(y * y, axis=0, keepdims=True) + SUBLN_EPS)
        y = y * (gsub_ref[...] * (1.0 - lambda_init))
    out_ref[qi * bq:(qi + 1) * bq, :] = y.T.astype(out_ref.dtype)


def _attention(q, k, vt, extra, n_batch, seq, fox, lambda_init=0.0):
    n_tok, width = q.shape
    n_pairs = width // LANES
    bq, bk = ATTN_Q_BLOCK, ATTN_K_BLOCK
    n_v = HEAD_DIM if fox else 2 * HEAD_DIM
    k_spec = pl.BlockSpec((seq, LANES), lambda b, hp: (b, hp))
    q_spec = k_spec
    vt_spec = pl.BlockSpec((LANES, seq), lambda b, hp: (hp, b))
    scratch = [pltpu.VMEM((2, 2, bq, LANES), BF16),
               pltpu.VMEM((2, bq // bk, bk, bq), F32),
               pltpu.VMEM((2, 2, bk, bq), BF16),
               pltpu.VMEM((2, n_v + BF16_ROWS, bq), F32),
               pltpu.VMEM((2, 3 * F32_ROWS, bq), F32)]
    if fox:
        extra_specs = [k_spec]
        scratch += [pltpu.VMEM((seq, LANES), BF16)] * 2
    else:
        lam_vec, g_sub = extra
        extra_specs = [pl.BlockSpec(lam_vec.shape, lambda b, hp: (0, 0)),
                       pl.BlockSpec(g_sub.shape, lambda b, hp: (0, 0))]
    return pl.pallas_call(
        functools.partial(_attn_kernel, bq=bq, bk=bk, fox=fox, lambda_init=lambda_init),
        grid=(n_batch, n_pairs),
        in_specs=[q_spec, k_spec, vt_spec] + extra_specs,
        out_specs=q_spec,
        out_shape=jax.ShapeDtypeStruct((n_tok, width), BF16),
        scratch_shapes=scratch,
        compiler_params=_params("arbitrary", "arbitrary"),
        name="fox_attn" if fox else "diff_attn",
    )(q, k, vt, *extra)


ROUTE_ROWS = 48


def _route_block(logits_t, tri, cnt_ref):
    rows, n = logits_t.shape
    row_f = lax.broadcasted_iota(jnp.int32, (rows, n), 0).astype(F32)

    def first_argmax(t):
        top = jnp.max(t, axis=0, keepdims=True)
        return top, jnp.min(jnp.where(t == top, row_f, float(rows)), axis=0, keepdims=True)

    is_group = row_f < N_GROUPS
    g_max, g_sel = first_argmax(jnp.where(is_group, logits_t, NEG_BIG))
    g_w = 1.0 / jnp.sum(jnp.where(is_group, jnp.exp(logits_t - g_max), 0.0), axis=0, keepdims=True)
    lo = N_GROUPS + EXPERTS_PER_GROUP * g_sel
    scores = jnp.where((row_f >= lo) & (row_f < lo + EXPERTS_PER_GROUP), logits_t, NEG_BIG)
    v1, i1 = first_argmax(scores)
    v2, i2 = first_argmax(jnp.where(row_f == i1, NEG_BIG, scores))
    t = jnp.exp(v2 - v1)
    w1 = g_w / (1.0 + t)
    w2 = w1 * t

    hit1 = row_f == i1
    hit2 = row_f == i2
    onehot = jnp.where(hit1 | hit2, 1.0, 0.0)
    before = _dot(onehot.astype(BF16), tri) + cnt_ref[...]
    r1 = jnp.sum(jnp.where(hit1, before, 0.0), axis=0, keepdims=True)
    r2 = jnp.sum(jnp.where(hit2, before, 0.0), axis=0, keepdims=True)
    cnt_ref[...] = before[:, n - 1:n] + onehot[:, n - 1:n]
    out_row = lax.broadcasted_iota(jnp.int32, (8, n), 0)
    out = jnp.zeros((8, n), F32)
    for pos, val in enumerate((i1 - N_GROUPS, i2 - N_GROUPS, r1, r2, w1, w2)):
        out = jnp.where(out_row == pos, val, out)
    return out


def _post_mixer_kernel(ya_ref, yb_ref, ga_ref, gb_ref, x_ref, wa_ref, wb_ref, wo_ref,
                       gc_ref, wcq_ref, kc_ref, vc_ref, wco_ref, gf_ref, wr_ref, br_ref, tri_ref,
                       x2_ref, hm_ref, route_ref, route_t_ref, cnt_out_ref, cnt_ref):
    @pl.when((pl.program_id(0) == 0) & (pl.program_id(1) == 0))
    def _():
        cnt_ref[...] = jnp.zeros_like(cnt_ref)

    merged = (_sigmoid(ga_ref[...].astype(F32)) * _dot(ya_ref[...], wa_ref[...])
              + _sigmoid(gb_ref[...].astype(F32)) * _dot(yb_ref[...], wb_ref[...]))
    x1 = x_ref[...] + _dot(merged.astype(BF16), wo_ref[...])

    hx = _rms(x1, gc_ref[...], NORM_EPS).astype(BF16)
    qc = (_dot(hx, wcq_ref[...]) * QK_SCALE).astype(BF16)
    kc = kc_ref[...]
    vc = vc_ref[...]
    lane = lax.broadcasted_iota(jnp.int32, qc.shape, 1)
    zero = jnp.zeros_like(qc)
    o = jnp.zeros(qc.shape, F32)
    for h in range(CROSS_HEADS):
        in_head = (lane >= h * HEAD_DIM) & (lane < (h + 1) * HEAD_DIM)
        s = _dot_nt(jnp.where(in_head, qc, zero), kc)
        p = jnp.exp(s - jnp.max(s, axis=1, keepdims=True))
        p = p * (1.0 / jnp.sum(p, axis=1, keepdims=True))
        o = jnp.where(in_head, _dot(p.astype(BF16), vc), o)
    x2 = x1 + _dot(o.astype(BF16), wco_ref[...])
    x2_ref[...] = x2

    hm = _rms(x2, gf_ref[...], NORM_EPS)
    hm_ref[...] = hm.astype(BF16)
    hm_hi = hm.astype(BF16)
    hm_lo = (hm - hm_hi.astype(F32)).astype(BF16)
    both = _dot_nt(wr_ref[...], hm_hi)
    logits_t = (both[0:ROUTE_ROWS] + both[ROUTE_ROWS:2 * ROUTE_ROWS]
                + _dot_nt(wr_ref[0:ROUTE_ROWS, :], hm_lo) + br_ref[...])
    route_t = _route_block(logits_t, tri_ref[...], cnt_ref)
    route_t_ref[...] = route_t
    pad_rows = jnp.zeros((LANES - route_t.shape[0], route_t.shape[1]), F32)
    route_ref[...] = jnp.concatenate([route_t, pad_rows], axis=0).T
    cnt_out_ref[...] = jnp.broadcast_to(cnt_ref[...], cnt_out_ref.shape)


def _post_mixer(ya, yb, ga, gb, x2d, wa, wb, wo, g_cross, wcq, kc, vc, wco, g_ffn, wr, br,
                n_batch, seq, mem_len, tm):
    n_tok, d = x2d.shape
    nblk = seq // tm
    tok = lambda b, j: (b * nblk + j, 0)
    const = lambda b, j: (0, 0)
    full = lambda a: pl.BlockSpec(a.shape, const)
    tri = jnp.triu(jnp.ones((tm, tm), BF16), 1)
    return pl.pallas_call(
        _post_mixer_kernel,
        grid=(n_batch, nblk),
        in_specs=[pl.BlockSpec((tm, ya.shape[1]), tok), pl.BlockSpec((tm, yb.shape[1]), tok),
                  pl.BlockSpec((tm, d), tok), pl.BlockSpec((tm, d), tok), pl.BlockSpec((tm, d), tok),
                  full(wa), full(wb), full(wo), full(g_cross), full(wcq),
                  pl.BlockSpec((mem_len, kc.shape[1]), lambda b, j: (b, 0)),
                  pl.BlockSpec((mem_len, vc.shape[1]), lambda b, j: (b, 0)),
                  full(wco), full(g_ffn), full(wr), full(br),
                  full(tri)],
        out_specs=[pl.BlockSpec((tm, d), tok), pl.BlockSpec((tm, d), tok),
                   pl.BlockSpec((tm, LANES), tok),
                   pl.BlockSpec((8, tm), lambda b, j: (0, b * nblk + j)),
                   pl.BlockSpec((ROUTE_ROWS, LANES), const)],
        out_shape=[jax.ShapeDtypeStruct((n_tok, d), F32),
                   jax.ShapeDtypeStruct((n_tok, d), BF16),
                   jax.ShapeDtypeStruct((n_tok, LANES), F32),
                   jax.ShapeDtypeStruct((8, n_tok), F32),
                   jax.ShapeDtypeStruct((ROUTE_ROWS, LANES), F32)],
        scratch_shapes=[pltpu.VMEM((ROUTE_ROWS, 1), F32)],
        compiler_params=_params("arbitrary", "arbitrary"),
        name="post_mixer",
    )(ya, yb, ga, gb, x2d, wa, wb, wo, g_cross, wcq, kc, vc, wco, g_ffn, wr, br, tri)


def _expert_kernel(be_ref, nused_ref, xs_ref, wg_ref, wu_ref, wd_ref, ys_ref, wg_s, wu_s, wd_s):
    i = pl.program_id(0)

    @pl.when((i == 0) | (be_ref[i] != be_ref[jnp.maximum(i - 1, 0)]))
    def _():
        wg_s[...] = wg_ref[0].astype(BF16)
        wu_s[...] = wu_ref[0].astype(BF16)
        wd_s[...] = wd_ref[0].astype(BF16)

    @pl.when(i < nused_ref[0])
    def _():
        xb = xs_ref[...]
        g = _dot(xb, wg_s[...])
        u = _dot(xb, wu_s[...])
        hb = (g * _sigmoid(g) * u).astype(BF16)
        ys_ref[...] = _dot(hb, wd_s[...]).astype(ys_ref.dtype)

    @pl.when(i >= nused_ref[0])
    def _():
        ys_ref[...] = jnp.zeros_like(ys_ref)


def _experts(block_expert, n_used, xs, wg, wu, wd):
    n_rows, d = xs.shape
    n_blocks = n_rows // MOE_BLOCK
    ff = wg.shape[2]
    grid_spec = pltpu.PrefetchScalarGridSpec(
        num_scalar_prefetch=2,
        grid=(n_blocks,),
        in_specs=[pl.BlockSpec((MOE_BLOCK, d), lambda i, be, nu: (i, 0)),
                  pl.BlockSpec((1, d, ff), lambda i, be, nu: (be[i], 0, 0)),
                  pl.BlockSpec((1, d, ff), lambda i, be, nu: (be[i], 0, 0)),
                  pl.BlockSpec((1, ff, d), lambda i, be, nu: (be[i], 0, 0))],
        out_specs=pl.BlockSpec((MOE_BLOCK, d), lambda i, be, nu: (i, 0)),
        scratch_shapes=[pltpu.VMEM((d, ff), BF16), pltpu.VMEM((d, ff), BF16),
                        pltpu.VMEM((ff, d), BF16)],
    )
    return pl.pallas_call(
        _expert_kernel,
        grid_spec=grid_spec,
        out_shape=jax.ShapeDtypeStruct((n_rows, d), BF16),
        compiler_params=_params("arbitrary"),
        name="experts",
    )(block_expert, n_used, xs, wg, wu, wd)


def _final_kernel(x_ref, y0_ref, y1_ref, route_ref, g_ref, out_ref, *, normalize):
    route = route_ref[...]
    w0 = route[:, 2 * TOP_K:2 * TOP_K + 1]
    w1 = route[:, 2 * TOP_K + 1:2 * TOP_K + 2]
    out = x_ref[...] + w0 * y0_ref[...].astype(F32) + w1 * y1_ref[...].astype(F32)
    out_ref[...] = _rms(out, g_ref[...], NORM_EPS) if normalize else out


def _final(x2, y_both, route, g_final, tm, normalize):
    n_tok, d = x2.shape
    nblk = n_tok // tm
    tok = lambda i: (i, 0)
    return pl.pallas_call(
        functools.partial(_final_kernel, normalize=normalize),
        grid=(nblk,),
        in_specs=[pl.BlockSpec((tm, d), tok), pl.BlockSpec((tm, d), tok),
                  pl.BlockSpec((tm, d), lambda i: (i + nblk, 0)),
                  pl.BlockSpec((tm, LANES), tok), pl.BlockSpec((1, d), lambda i: (0, 0))],
        out_specs=pl.BlockSpec((tm, d), tok),
        out_shape=jax.ShapeDtypeStruct((n_tok, d), F32),
        compiler_params=_params("arbitrary"),
        name="final_norm",
    )(x2, y_both, y_both, route, g_final)


def _slot_kernel(tab_ref, rt_ref, dest_ref, key_ref):
    e = rt_ref[0:TOP_K, :]
    rank = rt_ref[TOP_K:2 * TOP_K, :].astype(jnp.int32)
    pad = jnp.zeros(e.shape, jnp.int32)
    seg = jnp.zeros(e.shape, jnp.int32)
    for k in range(N_EXPERTS):
        hit = e == float(k)
        pad = jnp.where(hit, tab_ref[k], pad)
        seg = jnp.where(hit, tab_ref[N_EXPERTS + k], seg)
    dest_ref[...] = pad + rank
    key_ref[...] = seg + rank


def _slots(tables, route_t, tm):
    n_tok = route_t.shape[1]
    out = jax.ShapeDtypeStruct((TOP_K, n_tok), jnp.int32)
    grid_spec = pltpu.PrefetchScalarGridSpec(
        num_scalar_prefetch=1,
        grid=(n_tok // tm,),
        in_specs=[pl.BlockSpec((route_t.shape[0], tm), lambda i, tab: (0, i))],
        out_specs=[pl.BlockSpec((TOP_K, tm), lambda i, tab: (0, i))] * 2,
    )
    return pl.pallas_call(_slot_kernel, grid_spec=grid_spec, out_shape=[out, out],
                          compiler_params=_params("arbitrary"), name="moe_slots")(tables, route_t)


def _route(route_t, counts, n_tok):
    n_assign = n_tok * TOP_K
    counts = counts[N_GROUPS:N_GROUPS + N_EXPERTS, 0].astype(jnp.int32)
    seg_start = jnp.cumsum(counts) - counts
    padded = (counts + MOE_BLOCK - 1) // MOE_BLOCK * MOE_BLOCK
    pad_end = jnp.cumsum(padded)
    pad_start = pad_end - padded
    dest, key = _slots(jnp.concatenate([pad_start, seg_start]), route_t, tm=min(SLOT_BLOCK, n_tok))
    order = jnp.argsort(key.reshape(n_assign)).astype(jnp.int32)

    n_blocks = -(-n_assign // MOE_BLOCK) + N_EXPERTS
    n_rows = n_blocks * MOE_BLOCK
    block_start = jnp.arange(n_blocks, dtype=jnp.int32) * MOE_BLOCK
    block_expert = jnp.minimum(jnp.sum(block_start[:, None] >= pad_end[None, :], axis=1),
                               N_EXPERTS - 1).astype(jnp.int32)
    onehot = block_expert[:, None] == jnp.arange(N_EXPERTS, dtype=jnp.int32)[None, :]

    def of_block(table):
        return jnp.sum(jnp.where(onehot, table[None, :], 0), axis=1)

    offset = block_start - of_block(pad_start)
    in_block = jnp.arange(MOE_BLOCK, dtype=jnp.int32)[None, :]
    valid = in_block < (of_block(counts) - offset)[:, None]
    src = jnp.clip((of_block(seg_start) + offset)[:, None] + in_block, 0, n_assign - 1)
    rows = block_start[:, None] + in_block
    row_tok = jnp.where(valid, order[src] % n_tok, rows % n_tok).reshape(n_rows)
    n_used = (pad_end[-1] // MOE_BLOCK).astype(jnp.int32).reshape(1)
    return row_tok, dest, block_expert, n_used


def kernel(x, mem, positions, g_mix, w_in, b_fgate, w_branch_a, w_branch_b, w_out, lambda_q1, lambda_k1, lambda_q2, lambda_k2, g_diff_sub, g_cross, g_mem, w_cq, w_ckv, w_co, g_ffn, w_group, b_group, w_expert, b_expert, w_exp_gate, w_exp_up, w_exp_down, g_final):
    n_batch, seq, d = x.shape
    mem_len = mem.shape[1]
    depth = g_mix.shape[0]
    n_tok = n_batch * seq
    fox_w = FOX_HEADS * HEAD_DIM
    diff_w = DIFF_HEADS * 2 * HEAD_DIM

    half = HEAD_DIM // 2
    inv_freq = 10000.0 ** (-jnp.arange(half, dtype=F32) * 2.0 / HEAD_DIM)
    inv_lanes = jnp.concatenate([inv_freq, inv_freq, jnp.zeros((LANES - 2 * half,), F32)])
    phase = jnp.concatenate([jnp.zeros((half,), F32), jnp.full((half,), 0.5 * math.pi, F32),
                             jnp.zeros((LANES - 2 * half,), F32)])
    ang = positions.astype(F32).reshape(n_tok, 1) * inv_lanes[None, :]
    rot_t = jnp.cos(ang - phase[None, :])
    sel = _bias_lane_selectors()

    x2d = x.reshape(n_tok, d)
    mem2d = mem.reshape(n_batch * mem_len, d)
    for l in range(depth):
        lambda_init = 0.8 - 0.6 * math.exp(-0.3 * l)
        o_fv = 2 * fox_w
        o_fl = o_fv + fox_w
        o_dq = o_fl + FOX_HEADS
        o_dv = o_dq + 2 * diff_w
        o_ga = o_dv + diff_w
        wl = w_in[l]
        w_main = jnp.concatenate([wl[:, :o_fv], wl[:, o_dq:o_dv], wl[:, o_ga:]], axis=1).astype(BF16)
        w_vt = jnp.concatenate([wl[:, o_fv:o_fl], wl[:, o_dv:o_ga]], axis=1).T.astype(BF16)
        w_f = jnp.pad(wl[:, o_fl:o_dq], ((0, 0), (0, LANES - FOX_HEADS))).astype(BF16)
        b_row = jnp.pad(b_fgate[l], (0, LANES - FOX_HEADS))[None]

        kc, vc = _mem_kv(mem2d, g_mem[l][None], w_ckv[l].astype(BF16), n_batch, mem_len)
        fq, fk, dq, dk, ga, gb, fvt, dvt, fl = _in_proj(
            x2d, g_mix[l][None], w_main, w_vt, w_f, rot_t, tm=TOKEN_BLOCK)
        bias_lanes = _fox_prep(fl, b_row, sel, n_batch, seq, tc=LIGHT_BLOCK)
        y_a = _attention(fq, fk, fvt, (bias_lanes,), n_batch, seq, fox=True)
        lam_vec = jnp.stack([lambda_q1[l], lambda_k1[l], lambda_q2[l], lambda_k2[l]])
        y_b = _attention(dq, dk, dvt, (lam_vec, g_diff_sub[l][:, None]), n_batch, seq,
                         fox=False, lambda_init=lambda_init)

        w_router = jnp.pad(jnp.concatenate([w_group[l], w_expert[l]], axis=1).T,
                           ((0, ROUTE_ROWS - N_GROUPS - N_EXPERTS), (0, 0)))
        w_router_hi = w_router.astype(BF16)
        w_router = jnp.concatenate(
            [w_router_hi, (w_router - w_router_hi.astype(F32)).astype(BF16)], axis=0)
        b_router = jnp.pad(jnp.concatenate([b_group[l], b_expert[l]]),
                           (0, ROUTE_ROWS - N_GROUPS - N_EXPERTS))[:, None]
        x2d, hm, route, route_t, counts = _post_mixer(
            y_a, y_b, ga, gb, x2d, w_branch_a[l].astype(BF16), w_branch_b[l].astype(BF16),
            w_out[l].astype(BF16), g_cross[l][None], w_cq[l].astype(BF16), kc, vc,
            w_co[l].astype(BF16), g_ffn[l][None], w_router, b_router,
            n_batch, seq, mem_len, tm=TOKEN_BLOCK)

        row_tok, dest, block_expert, n_used = _route(route_t, counts, n_tok)
        xs = hm[row_tok]
        ys = _experts(block_expert, n_used, xs, w_exp_gate[l], w_exp_up[l], w_exp_down[l])
        last = l + 1 == depth
        x2d = _final(x2d, ys[dest.reshape(TOP_K * n_tok)], route, g_final[None], tm=LIGHT_BLOCK,
                     normalize=last)
    return x2d.reshape(n_batch, seq, d)
```
